```python
import math
import jax, jax.numpy as jnp
from jax import lax
import numpy as np

D_MODEL = 1024
BATCH = 4
SEQ = 4096
DEPTH = 2
DEC_BATCH = 32
DEC_SEQ = 4
PAST_LEN = 8192
PAGE_SIZE = 128

N_EVEN = (DEPTH + 1) // 2
N_ODD = DEPTH // 2
HALF_W = D_MODEL // 2
A_HD = 64
A_HEADS = HALF_W // A_HD
A_W = A_HEADS * A_HD
MOBA_BLOCK = 256
MOBA_TOPK = 3
MOBA_QBLOCK = 64
B_GROUPS = 8
B_GD = HALF_W // B_GROUPS
B_W = B_GROUPS * B_GD
B_CHUNK = 128
POOL_WINDOWS = (2, 4, 8, 16)
C_GROUPS = len(POOL_WINDOWS)
C_GD = HALF_W // C_GROUPS
C_W = C_GROUPS * C_GD
POOL_BUF = max(POOL_WINDOWS) - 1
D_HK = 128
D_HV = 128
D_HEADS = HALF_W // D_HK
D_W = D_HEADS * D_HK
HGRN_CHUNK = 64
D_FF = 4 * D_MODEL
EVEN_IN = 3 * A_W + 2 * B_W
ODD_IN = C_W + 4 * D_W
EVEN_OUT = A_W + B_W
ODD_OUT = C_W + D_W
ALPHA = (2 * DEPTH) ** 0.25
BETA = (8 * DEPTH) ** -0.25
LN_EPS = 1e-5
RMS_EPS = 1e-6
NEG = -1e30

kernel_name = "moba_gmlp_pool_hgrn2_hybrid_step"


def layer_norm(x, g, b):
    xf = x.astype(jnp.float32)
    mu = jnp.mean(xf, -1, keepdims=True)
    var = jnp.mean(jnp.square(xf - mu), -1, keepdims=True)
    return ((xf - mu) * lax.rsqrt(var + LN_EPS) * g + b).astype(x.dtype)


def sq_relu_mlp(x, w1, w2):
    return jnp.square(jax.nn.relu(x @ w1)) @ w2


def to_blocks(t):
    bsz, length = t.shape[:2]
    nb = -(-length // MOBA_BLOCK)
    t = jnp.pad(t, ((0, 0), (0, nb * MOBA_BLOCK - length), (0, 0), (0, 0)))
    return t.reshape(bsz, nb, MOBA_BLOCK, A_HEADS, A_HD).transpose(0, 3, 1, 2, 4).astype(jnp.float32)


def moba_attend(q, kb, vb, kmean, q_pos):
    bsz, nq = q.shape[:2]
    nb = kb.shape[2]
    qh = q.astype(jnp.float32).transpose(0, 2, 1, 3) * (A_HD ** -0.5)
    q_blk = q_pos // MOBA_BLOCK
    gate = jnp.einsum('bhqd,bhnd->bhqn', qh, kmean)
    fully_past = jnp.arange(nb)[None, :] < q_blk[:, None]
    gate = jnp.where(fully_past[None, None], gate, NEG)
    if nb < MOBA_TOPK:
        gate = jnp.pad(gate, ((0, 0), (0, 0), (0, 0), (0, MOBA_TOPK - nb)), constant_values=NEG)
    _, sel = lax.top_k(gate, MOBA_TOPK)
    sel = jnp.minimum(sel, nb - 1)
    sel_ok = jnp.arange(MOBA_TOPK)[None, :] < jnp.minimum(q_blk, MOBA_TOPK)[:, None]
    gather = jax.vmap(jax.vmap(lambda t, s: t[s]))
    k_sel = gather(kb, sel)
    v_sel = gather(vb, sel)
    k_own = kb[:, :, q_blk]
    v_own = vb[:, :, q_blk]
    s_sel = jnp.einsum('bhqd,bhqjkd->bhqjk', qh, k_sel)
    s_sel = jnp.where(sel_ok[None, None, :, :, None], s_sel, NEG)
    s_sel = s_sel.reshape(bsz, A_HEADS, nq, MOBA_TOPK * MOBA_BLOCK)
    own_pos = q_blk[:, None] * MOBA_BLOCK + jnp.arange(MOBA_BLOCK)[None, :]
    s_own = jnp.einsum('bhqd,bhqkd->bhqk', qh, k_own)
    s_own = jnp.where((own_pos <= q_pos[:, None])[None, None], s_own, NEG)
    p = jax.nn.softmax(jnp.concatenate([s_sel, s_own], -1), axis=-1)
    p_sel = p[..., :MOBA_TOPK * MOBA_BLOCK].reshape(bsz, A_HEADS, nq, MOBA_TOPK, MOBA_BLOCK)
    p_own = p[..., MOBA_TOPK * MOBA_BLOCK:]
    o = (jnp.einsum('bhqjk,bhqjkd->bhqd', p_sel, v_sel)
         + jnp.einsum('bhqk,bhqkd->bhqd', p_own, v_own))
    return o.transpose(0, 2, 1, 3).astype(q.dtype)


def moba_prompt(q, k, v):
    bsz, length = q.shape[:2]
    kb, vb = to_blocks(k), to_blocks(v)
    kmean = jnp.mean(kb, axis=3)
    nqb = length // MOBA_QBLOCK
    qs = q.reshape(bsz, nqb, MOBA_QBLOCK, A_HEADS, A_HD).transpose(1, 0, 2, 3, 4)
    pos = jnp.arange(length, dtype=jnp.int32).reshape(nqb, MOBA_QBLOCK)
    o = lax.map(lambda a: moba_attend(a[0], kb, vb, kmean, a[1]), (qs, pos))
    return o.transpose(1, 0, 2, 3, 4).reshape(bsz, length, A_HEADS, A_HD)


def moba_sample(q, k_new, v_new, cache_k_l, cache_v_l, page_table):
    dbsz, n_new = q.shape[:2]
    k_past = cache_k_l[page_table].reshape(dbsz, -1, A_HEADS, A_HD)
    v_past = cache_v_l[page_table].reshape(dbsz, -1, A_HEADS, A_HD)
    kb = to_blocks(jnp.concatenate([k_past, k_new.astype(k_past.dtype)], 1))
    vb = to_blocks(jnp.concatenate([v_past, v_new.astype(v_past.dtype)], 1))
    kmean = jnp.mean(kb, axis=3)
    q_pos = PAST_LEN + jnp.arange(n_new, dtype=jnp.int32)
    return moba_attend(q, kb, vb, kmean, q_pos)


def spatial_gate(u, v, ws, bs, ln_g, ln_b):
    bsz, length, _ = v.shape
    vg = v.reshape(bsz, length, B_GROUPS, B_GD)
    vn = layer_norm(vg, ln_g.reshape(B_GROUPS, B_GD), ln_b.reshape(B_GROUPS, B_GD))
    nc = -(-length // B_CHUNK)
    vp = jnp.pad(vn, ((0, 0), (0, nc * B_CHUNK - length), (0, 0), (0, 0)))
    vp = vp.reshape(bsz, nc, B_CHUNK, B_GROUPS, B_GD)
    tril = jnp.tril(jnp.ones((B_CHUNK, B_CHUNK), bool))
    w = jnp.where(tril[None], ws, 0.0)
    mixed = jnp.einsum('gts,bcsgd->bctgd', w, vp) + bs.T[None, None, :, :, None]
    mixed = mixed.reshape(bsz, nc * B_CHUNK, B_W)[:, :length]
    return u * mixed, vn.reshape(bsz, length, B_W)


def even_mixer(x, w_in, w_out, ws, bs, ln_g, ln_b, attend):
    bsz, length, _ = x.shape
    h = x @ w_in
    q, k, v, u, gv = jnp.split(h, [A_W, 2 * A_W, 3 * A_W, 3 * A_W + B_W], axis=-1)
    q, k, v = (t.reshape(bsz, length, A_HEADS, A_HD) for t in (q, k, v))
    o_a = attend(q, k, v).reshape(bsz, length, A_W)
    o_b, v_rows = spatial_gate(jax.nn.gelu(u), jax.nn.gelu(gv), ws, bs, ln_g, ln_b)
    y = jnp.concatenate([o_a, o_b.astype(o_a.dtype)], -1) @ w_out
    return y, k, v, v_rows


def pool_mix(xc, buf, pos0, w_pool, scale):
    bsz, length, _ = xc.shape
    ext = jnp.concatenate([buf.astype(xc.dtype), xc], 1)
    cs = jnp.cumsum(ext.astype(jnp.float32), axis=1)
    cs = jnp.pad(cs, ((0, 0), (1, 0), (0, 0)))
    pos = pos0 + jnp.arange(length)
    outs = []
    for g, win in enumerate(POOL_WINDOWS):
        sl = slice(g * C_GD, (g + 1) * C_GD)
        end = cs[:, POOL_BUF + 1:POOL_BUF + 1 + length, sl]
        start = cs[:, POOL_BUF + 1 - win:POOL_BUF + 1 - win + length, sl]
        cnt = jnp.minimum(win, pos + 1).astype(jnp.float32)
        outs.append((end - start) / cnt[None, :, None])
    pooled = (jnp.concatenate(outs, -1) - xc.astype(jnp.float32)).astype(xc.dtype)
    y = jnp.einsum('blgc,gce->blge', pooled.reshape(bsz, length, C_GROUPS, C_GD), w_pool)
    y = y.reshape(bsz, length, C_W) * scale
    return y, ext[:, -POOL_BUF:]


def hgrn2_recurrence(q, f_logit, i, lb, s0):
    bsz, length, nh, _ = q.shape
    c = math.gcd(length, HGRN_CHUNK)
    nc = length // c
    f = lb + (1.0 - lb) * jax.nn.sigmoid(f_logit.astype(jnp.float32))
    logf = jnp.log(f)
    k = 1.0 - f

    def chunks(t):
        return t.astype(jnp.float32).reshape(bsz, nc, c, nh, t.shape[-1]).transpose(1, 0, 2, 3, 4)

    causal = jnp.tril(jnp.ones((c, c), bool))

    def step(s, inp):
        qc, kc, vc, gc = inp
        cg = jnp.cumsum(gc, axis=1)
        o_inter = jnp.einsum('bthk,bhkv->bthv', qc * jnp.exp(cg), s)
        diff = cg[:, :, None] - cg[:, None, :]
        decay = jnp.exp(jnp.where(causal[None, :, :, None, None], diff, NEG))
        attn = jnp.einsum('bthk,btshk,bshk->bhts', qc, decay, kc)
        o_intra = jnp.einsum('bhts,bshv->bthv', attn, vc)
        g_last = cg[:, -1]
        s_new = (s * jnp.exp(g_last)[..., None]
                 + jnp.einsum('bshk,bshv->bhkv', kc * jnp.exp(g_last[:, None] - cg), vc))
        return s_new, o_inter + o_intra

    s_fin, o = lax.scan(step, s0.astype(jnp.float32), (chunks(q), chunks(k), chunks(i), chunks(logf)))
    return o.transpose(1, 0, 2, 3, 4).reshape(bsz, length, nh, -1), s_fin


def odd_mixer(x, w_in, w_out, w_pool, pool_scale, lb, norm_g, buf, s0, pos0):
    bsz, length, _ = x.shape
    h = x @ w_in
    xc, q, f, i, g = jnp.split(h, [C_W, C_W + D_W, C_W + 2 * D_W, C_W + 3 * D_W], axis=-1)
    o_c, new_buf = pool_mix(xc, buf, pos0, w_pool, pool_scale)
    q = jax.nn.silu(q).reshape(bsz, length, D_HEADS, D_HK)
    f = f.reshape(bsz, length, D_HEADS, D_HK)
    i = i.reshape(bsz, length, D_HEADS, D_HV)
    o, s_new = hgrn2_recurrence(q, f, i, lb, s0)
    o = o * lax.rsqrt(jnp.mean(o * o, -1, keepdims=True) + RMS_EPS) * norm_g
    o = o * jax.nn.silu(g.astype(jnp.float32)).reshape(bsz, length, D_HEADS, D_HV)
    y = jnp.concatenate([o_c, o.reshape(bsz, length, D_W).astype(o_c.dtype)], -1) @ w_out
    return y, new_buf, s_new.astype(s0.dtype)


def setup_inputs(seed: int = 0) -> dict:
    key = jax.random.key(seed)
    ks = jax.random.split(key, 32)
    f32 = jnp.float32

    def nrm(k, shape, scale=1.0):
        return jax.random.normal(k, shape, f32) * scale

    n_pages = PAST_LEN // PAGE_SIZE
    n_used = DEC_BATCH * n_pages
    n_pool = n_used + (n_used + 3) // 4
    perm = jax.random.permutation(ks[0], n_pool).astype(jnp.int32)
    page_table = perm[:n_used].reshape(DEC_BATCH, n_pages)
    return {
        "x_prompt": nrm(ks[1], (BATCH, SEQ, D_MODEL)),
        "x_sample": nrm(ks[2], (DEC_BATCH, DEC_SEQ, D_MODEL)),
        "cache_k": nrm(ks[3], (N_EVEN, n_pool, PAGE_SIZE, A_HEADS, A_HD)),
        "cache_v": nrm(ks[4], (N_EVEN, n_pool, PAGE_SIZE, A_HEADS, A_HD)),
        "state_pool": nrm(ks[5], (N_ODD, DEC_BATCH, POOL_BUF, C_W)),
        "state_hgrn": nrm(ks[6], (N_ODD, DEC_BATCH, D_HEADS, D_HK, D_HV), 0.3),
        "page_table": page_table,
        "w_in_even": nrm(ks[7], (N_EVEN, D_MODEL, EVEN_IN), D_MODEL ** -0.5),
        "w_out_even": nrm(ks[8], (N_EVEN, EVEN_OUT, D_MODEL), BETA * EVEN_OUT ** -0.5),
        "gmlp_ws": nrm(ks[9], (N_EVEN, B_GROUPS, B_CHUNK, B_CHUNK), B_CHUNK ** -0.5),
        "gmlp_bs": 1.0 + nrm(ks[10], (N_EVEN, B_GROUPS, B_CHUNK), 0.02),
        "gmlp_ln_g": 1.0 + nrm(ks[11], (N_EVEN, B_W), 0.02),
        "gmlp_ln_b": nrm(ks[12], (N_EVEN, B_W), 0.02),
        "w_in_odd": nrm(ks[13], (N_ODD, D_MODEL, ODD_IN), D_MODEL ** -0.5),
        "w_out_odd": nrm(ks[14], (N_ODD, ODD_OUT, D_MODEL), BETA * ODD_OUT ** -0.5),
        "pool_w": nrm(ks[15], (N_ODD, C_GROUPS, C_GD, C_GD), C_GD ** -0.5),
        "pool_scale": 1.0 + nrm(ks[16], (N_ODD, C_W), 0.1),
        "hgrn_lb_param": nrm(ks[17], (DEPTH, D_W), 0.1),
        "hgrn_norm_g": 1.0 + nrm(ks[18], (N_ODD, D_HV), 0.02),
        "ln_mix_g": 1.0 + nrm(ks[19], (DEPTH, D_MODEL), 0.02),
        "ln_mix_b": nrm(ks[20], (DEPTH, D_MODEL), 0.02),
        "ln_ffn_g": 1.0 + nrm(ks[21], (DEPTH, D_MODEL), 0.02),
        "ln_ffn_b": nrm(ks[22], (DEPTH, D_MODEL), 0.02),
        "ffn_w1": nrm(ks[23], (DEPTH, D_MODEL, D_FF), D_MODEL ** -0.5),
        "ffn_w2": nrm(ks[24], (DEPTH, D_FF, D_MODEL), BETA * D_FF ** -0.5),
    }


def reference(x_prompt, x_sample, cache_k, cache_v, state_pool, state_hgrn, page_table,
              w_in_even, w_out_even, gmlp_ws, gmlp_bs, gmlp_ln_g, gmlp_ln_b,
              w_in_odd, w_out_odd, pool_w, pool_scale, hgrn_lb_param, hgrn_norm_g,
              ln_mix_g, ln_mix_b, ln_ffn_g, ln_ffn_b, ffn_w1, ffn_w2):
    xp, xs = x_prompt, x_sample
    lbs = jnp.cumsum(jax.nn.softmax(hgrn_lb_param.astype(jnp.float32), axis=0), axis=0)
    kp_l, vp_l, ks_l, vs_l, gv_l = [], [], [], [], []
    bp_l, bs_l, sp_l, ss_l = [], [], [], []
    for l in range(DEPTH):
        j = l // 2
        if l % 2 == 0:
            mp, kp, vp, _ = even_mixer(xp, w_in_even[j], w_out_even[j], gmlp_ws[j], gmlp_bs[j],
                                       gmlp_ln_g[j], gmlp_ln_b[j], moba_prompt)
            att_s = lambda q, k, v, j=j: moba_sample(q, k, v, cache_k[j], cache_v[j], page_table)
            ms, ksmp, vsmp, gvs = even_mixer(xs, w_in_even[j], w_out_even[j], gmlp_ws[j], gmlp_bs[j],
                                             gmlp_ln_g[j], gmlp_ln_b[j], att_s)
            kp_l.append(kp); vp_l.append(vp); ks_l.append(ksmp); vs_l.append(vsmp); gv_l.append(gvs)
        else:
            lb = lbs[l - 1].reshape(D_HEADS, D_HK)
            buf0 = jnp.zeros((xp.shape[0], POOL_BUF, C_W), xp.dtype)
            s00 = jnp.zeros((xp.shape[0], D_HEADS, D_HK, D_HV), xp.dtype)
            mp, bufp, sp = odd_mixer(xp, w_in_odd[j], w_out_odd[j], pool_w[j], pool_scale[j], lb,
                                     hgrn_norm_g[j], buf0, s00, 0)
            ms, bufs, ss = odd_mixer(xs, w_in_odd[j], w_out_odd[j], pool_w[j], pool_scale[j], lb,
                                     hgrn_norm_g[j], state_pool[j], state_hgrn[j], PAST_LEN)
            bp_l.append(bufp); bs_l.append(bufs); sp_l.append(sp); ss_l.append(ss)
        xp = layer_norm(ALPHA * xp + mp.astype(xp.dtype), ln_mix_g[l], ln_mix_b[l])
        xs = layer_norm(ALPHA * xs + ms.astype(xs.dtype), ln_mix_g[l], ln_mix_b[l])
        xp = layer_norm(ALPHA * xp + sq_relu_mlp(xp, ffn_w1[l], ffn_w2[l]), ln_ffn_g[l], ln_ffn_b[l])
        xs = layer_norm(ALPHA * xs + sq_relu_mlp(xs, ffn_w1[l], ffn_w2[l]), ln_ffn_g[l], ln_ffn_b[l])
    new_k_prompt = jnp.stack(kp_l)
    new_v_prompt = jnp.stack(vp_l)
    new_k_sample = jnp.stack(ks_l)
    new_v_sample = jnp.stack(vs_l)
    new_gmlp_v_sample = jnp.stack(gv_l)
    new_pool_prompt = jnp.stack(bp_l)
    new_pool_sample = jnp.stack(bs_l)
    new_hgrn_prompt = jnp.stack(sp_l)
    new_hgrn_sample = jnp.stack(ss_l)
    return (xp, xs, new_k_prompt, new_v_prompt, new_k_sample, new_v_sample, new_gmlp_v_sample,
            new_pool_prompt, new_pool_sample, new_hgrn_prompt, new_hgrn_sample)
```

```python
import functools

import jax
import jax.numpy as jnp
from jax import lax
from jax.experimental import pallas as pl
from jax.experimental.pallas import tpu as pltpu

F32 = jnp.float32
BF16 = jnp.bfloat16

D_MODEL = 1024
BATCH = 4
SEQ = 4096
DEPTH = 2
DEC_BATCH = 32
DEC_SEQ = 4
PAST_LEN = 8192
PAGE_SIZE = 128
HALF_W = D_MODEL // 2
A_HD = 64
A_HEADS = HALF_W // A_HD
MOBA_BLOCK = 256
MOBA_TOPK = 3
B_GROUPS = 8
B_GD = HALF_W // B_GROUPS
B_CHUNK = 128
POOL_WINDOWS = (2, 4, 8, 16)
C_GD = HALF_W // len(POOL_WINDOWS)
POOL_BUF = max(POOL_WINDOWS) - 1
D_HK = 128
D_HEADS = HALF_W // D_HK
D_FF = 4 * D_MODEL
EVEN_IN = 5 * HALF_W
ODD_IN = 5 * HALF_W
ALPHA = (2 * DEPTH) ** 0.25
LN_EPS = 1e-5
RMS_EPS = 1e-6
NEG = -1e30

N_PROMPT = BATCH * SEQ
N_SAMPLE = DEC_BATCH * DEC_SEQ
N_ROWS = N_PROMPT + N_SAMPLE
N_PAGES = PAST_LEN // PAGE_SIZE
N_PAST_BLOCKS = PAST_LEN // MOBA_BLOCK
PAGES_PER_BLOCK = MOBA_BLOCK // PAGE_SIZE

LANES = 128
SUBLANES = 8
ROW_TILE = 384
FF_CHUNK = 1024
PAGES_PER_STEP = 8
HGRN_CHUNK = 128
HGRN_SUB = 16
HGRN_ROWS = 512
POOL_ROWS = 512
POOL_HALO = 16
MIB = 1024 * 1024


def _dot(a, b):
    return jnp.dot(a, b, preferred_element_type=F32)


def _dot_nt(a, b):
    return lax.dot_general(a, b, (((1,), (1,)), ((), ())), preferred_element_type=F32)


def _split3(x):
    p0 = x.astype(BF16)
    r1 = x - p0.astype(F32)
    p1 = r1.astype(BF16)
    p2 = (r1 - p1.astype(F32)).astype(BF16)
    return p0, p1, p2


def _dot_exact_lhs(m_bf16, x):
    p0, p1, p2 = _split3(x)
    return _dot(m_bf16, p0) + _dot(m_bf16, p1) + _dot(m_bf16, p2)


def _dot_exact_rhs(x, m_bf16):
    p0, p1, p2 = _split3(x)
    return _dot(p0, m_bf16) + _dot(p1, m_bf16) + _dot(p2, m_bf16)


def _gelu_tanh(x):
    return 0.5 * x * (1.0 + jnp.tanh(0.7978845608028654 * (x + 0.044715 * (x * x * x))))


def _sigmoid(x):
    return 1.0 / (1.0 + jnp.exp(-x))


def _layer_norm(z, g, b):
    mu = jnp.mean(z, axis=-1, keepdims=True)
    zc = z - mu
    var = jnp.mean(zc * zc, axis=-1, keepdims=True)
    return zc * lax.rsqrt(var + LN_EPS) * g + b


def _params(sem, vmem_mib):
    return pltpu.CompilerParams(dimension_semantics=sem, vmem_limit_bytes=int(vmem_mib * MIB))


def _proj_kernel(x_ref, w_ref, h_ref, *hb_refs, n_bf16):
    xb = x_ref[...].astype(BF16)
    n_out = w_ref.shape[1]
    for c in range(n_out // HALF_W):
        sl = slice(c * HALF_W, (c + 1) * HALF_W)
        h = _dot(xb, w_ref[:, sl])
        h_ref[:, sl] = h
        if c * HALF_W < n_bf16:
            hb_refs[0][:, sl] = h.astype(BF16)


def _proj(x, w_bf16, n_bf16):
    n, d = x.shape
    n_out = w_bf16.shape[1]
    out_shape = [jax.ShapeDtypeStruct((n, n_out), F32)]
    out_specs = [pl.BlockSpec((ROW_TILE, n_out), lambda i: (i, 0))]
    if n_bf16:
        out_shape.append(jax.ShapeDtypeStruct((n, n_bf16), BF16))
        out_specs.append(pl.BlockSpec((ROW_TILE, n_bf16), lambda i: (i, 0)))
    vmem = 2 * (ROW_TILE * d * 4 + d * n_out * 2 + ROW_TILE * n_out * 4 + ROW_TILE * n_bf16 * 2)
    res = pl.pallas_call(
        functools.partial(_proj_kernel, n_bf16=n_bf16),
        grid=(n // ROW_TILE,),
        in_specs=[pl.BlockSpec((ROW_TILE, d), lambda i: (i, 0)),
                  pl.BlockSpec((d, n_out), lambda i: (0, 0))],
        out_specs=out_specs,
        out_shape=out_shape,
        compiler_params=_params(("parallel",), vmem / MIB + 8),
        name="in_proj",
    )(x, w_bf16)
    return res if n_bf16 else (res[0], None)


def _mix_out_kernel(oa_ref, ob_ref, x_ref, w_ref, g_ref, b_ref, y_ref):
    acc = _dot(oa_ref[...].astype(BF16), w_ref[:HALF_W, :])
    acc = acc + _dot(ob_ref[...].astype(BF16), w_ref[HALF_W:, :])
    y_ref[...] = _layer_norm(ALPHA * x_ref[...] + acc, g_ref[...], b_ref[...])


def _mix_out(oa, ob, x, w_bf16, g, b):
    n, d = x.shape
    row = lambda i: (i, 0)
    fixed = lambda i: (0, 0)
    vmem = 2 * (2 * ROW_TILE * HALF_W * 4 + 2 * ROW_TILE * d * 4 + d * d * 2)
    return pl.pallas_call(
        _mix_out_kernel,
        grid=(n // ROW_TILE,),
        in_specs=[pl.BlockSpec((ROW_TILE, HALF_W), row), pl.BlockSpec((ROW_TILE, HALF_W), row),
                  pl.BlockSpec((ROW_TILE, d), row), pl.BlockSpec((d, d), fixed),
                  pl.BlockSpec((1, d), fixed), pl.BlockSpec((1, d), fixed)],
        out_specs=pl.BlockSpec((ROW_TILE, d), row),
        out_shape=jax.ShapeDtypeStruct((n, d), F32),
        compiler_params=_params(("parallel",), vmem / MIB + 8),
        name="mix_out_ln",
    )(oa, ob, x, w_bf16, g.reshape(1, d), b.reshape(1, d))


def _ffn_kernel(x_ref, w1_ref, w2_ref, g_ref, b_ref, y_ref):
    x = x_ref[...]
    xb = x.astype(BF16)
    acc = jnp.zeros(x.shape, F32)
    for c in range(D_FF // FF_CHUNK):
        sl = slice(c * FF_CHUNK, (c + 1) * FF_CHUNK)
        hid = jnp.maximum(_dot(xb, w1_ref[:, sl]), 0.0)
        acc = acc + _dot((hid * hid).astype(BF16), w2_ref[sl, :])
    y_ref[...] = _layer_norm(ALPHA * x + acc, g_ref[...], b_ref[...])


def _ffn(x, w1_bf16, w2_bf16, g, b):
    n, d = x.shape
    row = lambda i: (i, 0)
    fixed = lambda i: (0, 0)
    vmem = 2 * (2 * ROW_TILE * d * 4 + 2 * d * D_FF * 2) + 3 * ROW_TILE * FF_CHUNK * 4
    return pl.pallas_call(
        _ffn_kernel,
        grid=(n // ROW_TILE,),
        in_specs=[pl.BlockSpec((ROW_TILE, d), row),
                  pl.BlockSpec((d, D_FF), fixed), pl.BlockSpec((D_FF, d), fixed),
                  pl.BlockSpec((1, d), fixed), pl.BlockSpec((1, d), fixed)],
        out_specs=pl.BlockSpec((ROW_TILE, d), row),
        out_shape=jax.ShapeDtypeStruct((n, d), F32),
        compiler_params=_params(("parallel",), vmem / MIB + 8),
        name="ffn_ln",
    )(x, w1_bf16, w2_bf16, g.reshape(1, d), b.reshape(1, d))


def _moba_prompt_kernel(qt_ref, k_ref, vt_ref, o_ref,
                        kmt_scr, qtz_scr, selb_scr, m_scr, l_scr, acc_scr):
    j = pl.program_id(1)
    nb = k_ref.shape[0]
    blk = MOBA_BLOCK
    pair_w = 2 * A_HD

    @pl.when(j == 0)
    def _():
        rows = [jnp.sum(k_ref[n].astype(F32), axis=0, keepdims=True) * (1.0 / blk)
                for n in range(nb)]
        kmean = jnp.concatenate(rows, axis=0)
        tiled = jnp.concatenate([kmean] * A_HEADS, axis=0)
        rh = lax.broadcasted_iota(jnp.int32, tiled.shape, 0) // nb
        ch = lax.broadcasted_iota(jnp.int32, tiled.shape, 1) // A_HD
        kmt_scr[...] = jnp.where(rh == ch, tiled, 0.0)

    qts = qt_ref[...] * (A_HD ** -0.5)
    k0, k1, k2 = _split3(kmt_scr[...])
    gate = _dot(k0, qts) + _dot(k1, qts) + _dot(k2, qts)

    n_io = lax.broadcasted_iota(jnp.int32, (nb, blk), 0)
    half = lax.broadcasted_iota(jnp.int32, (pair_w, blk), 0) // A_HD
    for h in range(A_HEADS):
        g = jnp.where(n_io < j, gate[h * nb:(h + 1) * nb, :], NEG)
        bias = jnp.full((nb, blk), NEG, F32)
        for _ in range(MOBA_TOPK):
            mx = jnp.max(g, axis=0, keepdims=True)
            cand = jnp.where((g == mx) & (mx > 0.5 * NEG), n_io, nb)
            pick = n_io == jnp.min(cand, axis=0, keepdims=True)
            bias = jnp.where(pick, 0.0, bias)
            g = jnp.where(pick, NEG, g)
        selb_scr[h * nb:(h + 1) * nb, :] = bias
        pr = h // 2
        qpair = qts[pr * pair_w:(pr + 1) * pair_w, :]
        qtz_scr[h] = jnp.where(half == (h % 2), qpair, jnp.zeros_like(qpair))

    causal = (lax.broadcasted_iota(jnp.int32, (blk, blk), 0)
              <= lax.broadcasted_iota(jnp.int32, (blk, blk), 1))
    for h in range(A_HEADS):
        pr = h // 2
        s = _dot(k_ref[j, :, pr * pair_w:(pr + 1) * pair_w], qtz_scr[h])
        s = jnp.where(causal, s, NEG)
        m = jnp.max(s, axis=0, keepdims=True)
        p = jnp.exp(s - m)
        m_scr[h:h + 1, :] = m
        l_scr[h:h + 1, :] = jnp.sum(p, axis=0, keepdims=True)
        acc_scr[h * A_HD:(h + 1) * A_HD, :] = _dot(vt_ref[j, h * A_HD:(h + 1) * A_HD, :],
                                                   p.astype(BF16))

    def past_block(n, carry):
        for h in range(A_HEADS):
            pr = h // 2
            s = _dot(k_ref[n, :, pr * pair_w:(pr + 1) * pair_w], qtz_scr[h])
            s = s + selb_scr[pl.ds(h * nb + n, 1), :]
            m_old = m_scr[h:h + 1, :]
            m_new = jnp.maximum(m_old, jnp.max(s, axis=0, keepdims=True))
            a = jnp.exp(m_old - m_new)
            p = jnp.exp(s - m_new)
            m_scr[h:h + 1, :] = m_new
            l_scr[h:h + 1, :] = a * l_scr[h:h + 1, :] + jnp.sum(p, axis=0, keepdims=True)
            hs = slice(h * A_HD, (h + 1) * A_HD)
            acc_scr[hs, :] = a * acc_scr[hs, :] + _dot(vt_ref[n, hs, :], p.astype(BF16))
        return carry

    lax.fori_loop(0, j, past_block, 0)

    for h in range(A_HEADS):
        hs = slice(h * A_HD, (h + 1) * A_HD)
        acc_scr[hs, :] = acc_scr[hs, :] / l_scr[h:h + 1, :]
    o_ref[...] = acc_scr[...].T


def _moba_prompt(qt, kb, vt):
    bsz, nb = qt.shape[:2]
    w = A_HEADS * A_HD
    vmem = 2 * (2 * nb * MOBA_BLOCK * w * 2 + w * MOBA_BLOCK * 2 + MOBA_BLOCK * w * 4) + 4 * MIB
    return pl.pallas_call(
        _moba_prompt_kernel,
        grid=(bsz, nb),
        in_specs=[pl.BlockSpec((None, None, w, MOBA_BLOCK), lambda b, j: (b, j, 0, 0)),
                  pl.BlockSpec((None, nb, MOBA_BLOCK, w), lambda b, j: (b, 0, 0, 0)),
                  pl.BlockSpec((None, nb, w, MOBA_BLOCK), lambda b, j: (b, 0, 0, 0))],
        out_specs=pl.BlockSpec((MOBA_BLOCK, w), lambda b, j: (b * nb + j, 0)),
        out_shape=jax.ShapeDtypeStruct((bsz * nb * MOBA_BLOCK, w), F32),
        scratch_shapes=[pltpu.VMEM((A_HEADS * nb, w), F32),
                        pltpu.VMEM((A_HEADS, 2 * A_HD, MOBA_BLOCK), BF16),
                        pltpu.VMEM((A_HEADS * nb, MOBA_BLOCK), F32),
                        pltpu.VMEM((A_HEADS, MOBA_BLOCK), F32),
                        pltpu.VMEM((A_HEADS, MOBA_BLOCK), F32),
                        pltpu.VMEM((w, MOBA_BLOCK), F32)],
        compiler_params=_params(("parallel", "arbitrary"), vmem / MIB + 8),
        name="moba_prompt",
    )(qt, kb, vt)


def _moba_sample_kernel(pt_ref, q_ref, kn_ref, vn_ref, *refs):
    del pt_ref
    k_refs = refs[:PAGES_PER_STEP]
    v_refs = refs[PAGES_PER_STEP:2 * PAGES_PER_STEP]
    o_ref = refs[2 * PAGES_PER_STEP]
    m_scr, l_scr, gs_scr, gate_scr, acc_scr = refs[2 * PAGES_PER_STEP + 1:]
    g = pl.program_id(1)
    w = A_HEADS * A_HD
    nrow = DEC_SEQ * A_HEADS
    wide = lambda c: jnp.broadcast_to(c, (nrow, LANES))

    head_mask = (lax.broadcasted_iota(jnp.int32, (A_HEADS, w), 1) // A_HD
                 == lax.broadcasted_iota(jnp.int32, (A_HEADS, w), 0))
    q = q_ref[...] * (A_HD ** -0.5)
    qbd = jnp.concatenate(
        [jnp.where(head_mask, jnp.broadcast_to(q[i:i + 1, :], (A_HEADS, w)), 0.0)
         for i in range(DEC_SEQ)], axis=0).astype(BF16)

    for i in range(PAGES_PER_STEP):
        page = g * PAGES_PER_STEP + i
        s = _dot_nt(qbd, k_refs[i][...].astype(BF16))
        m = jnp.max(s, axis=-1, keepdims=True)
        p = jnp.exp(s - m)
        m_scr[page] = wide(m)
        l_scr[page] = wide(jnp.sum(p, axis=-1, keepdims=True))
        gs_scr[page] = wide(jnp.sum(s, axis=-1, keepdims=True))
        acc_scr[page] = _dot(p.astype(BF16), v_refs[i][...].astype(BF16))

    @pl.when(g == pl.num_programs(1) - 1)
    def _():
        tile4 = lambda c: jnp.concatenate([c] * (w // LANES), axis=1)
        s = _dot_nt(qbd, kn_ref[...].astype(BF16))
        col = lax.broadcasted_iota(jnp.int32, (nrow, LANES), 1)
        qi = lax.broadcasted_iota(jnp.int32, (nrow, LANES), 0) // A_HEADS
        s = jnp.where(col <= qi, s, NEG)
        m_own = wide(jnp.max(s, axis=-1, keepdims=True))
        p = jnp.exp(s - m_own)
        l_own = wide(jnp.sum(p, axis=-1, keepdims=True))
        o_own = _dot(p.astype(BF16), vn_ref[...].astype(BF16))

        for n in range(N_PAST_BLOCKS):
            tot = gs_scr[PAGES_PER_BLOCK * n]
            for r in range(1, PAGES_PER_BLOCK):
                tot = tot + gs_scr[PAGES_PER_BLOCK * n + r]
            gate_scr[n] = tot * (1.0 / MOBA_BLOCK)
        for _ in range(MOBA_TOPK):
            best = jnp.full((nrow, LANES), NEG, F32)
            bidx = jnp.zeros((nrow, LANES), jnp.int32)
            for n in range(N_PAST_BLOCKS):
                gn = gate_scr[n]
                upd = gn > best
                best = jnp.where(upd, gn, best)
                bidx = jnp.where(upd, n, bidx)
            for n in range(N_PAST_BLOCKS):
                gate_scr[n] = jnp.where(bidx == n, -jnp.inf, gate_scr[n])

        m_all = m_own
        for pg in range(N_PAGES):
            picked = gate_scr[pg // PAGES_PER_BLOCK] == -jnp.inf
            m_all = jnp.maximum(m_all, jnp.where(picked, m_scr[pg], NEG))
        w_own = jnp.exp(m_own - m_all)
        l_all = w_own * l_own
        o_all = tile4(w_own) * o_own
        for pg in range(N_PAGES):
            picked = gate_scr[pg // PAGES_PER_BLOCK] == -jnp.inf
            wp = jnp.where(picked, jnp.exp(m_scr[pg] - m_all), 0.0)
            l_all = l_all + wp * l_scr[pg]
            o_all = o_all + tile4(wp) * acc_scr[pg]
        out = o_all / tile4(l_all)
        for i in range(DEC_SEQ):
            rows = out[i * A_HEADS:(i + 1) * A_HEADS, :]
            o_ref[i:i + 1, :] = jnp.sum(jnp.where(head_mask, rows, 0.0), axis=0, keepdims=True)


def _moba_sample(page_table, q, k_new_pad, v_new_pad, cache_k, cache_v):
    w = A_HEADS * A_HD
    nrow = DEC_SEQ * A_HEADS
    page_spec = lambda i: pl.BlockSpec(
        (None, PAGE_SIZE, w), lambda b, g, pt: (pt[b, g * PAGES_PER_STEP + i], 0, 0))
    per_sample = lambda rows: pl.BlockSpec((None, rows, w), lambda b, g, pt: (b, 0, 0))
    grid_spec = pltpu.PrefetchScalarGridSpec(
        num_scalar_prefetch=1,
        grid=(DEC_BATCH, N_PAGES // PAGES_PER_STEP),
        in_specs=([per_sample(DEC_SEQ), per_sample(PAGE_SIZE), per_sample(PAGE_SIZE)]
                  + [page_spec(i) for i in range(PAGES_PER_STEP)]
                  + [page_spec(i) for i in range(PAGES_PER_STEP)]),
        out_specs=per_sample(DEC_SEQ),
        scratch_shapes=[pltpu.VMEM((N_PAGES, nrow, LANES), F32),
                        pltpu.VMEM((N_PAGES, nrow, LANES), F32),
                        pltpu.VMEM((N_PAGES, nrow, LANES), F32),
                        pltpu.VMEM((N_PAST_BLOCKS, nrow, LANES), F32),
                        pltpu.VMEM((N_PAGES, nrow, w), F32)],
    )
    vmem = (2 * 2 * PAGES_PER_STEP * PAGE_SIZE * w * 4 + 4 * PAGE_SIZE * w * 4
            + 4 * N_PAGES * nrow * LANES * 4 + N_PAGES * nrow * w * 4)
    return pl.pallas_call(
        _moba_sample_kernel,
        grid_spec=grid_spec,
        out_shape=jax.ShapeDtypeStruct((DEC_BATCH, DEC_SEQ, w), F32),
        compiler_params=_params(("parallel", "arbitrary"), vmem / MIB + 8),
        name="moba_sample",
    )(page_table, q, k_new_pad, v_new_pad,
      *([cache_k] * PAGES_PER_STEP), *([cache_v] * PAGES_PER_STEP))


def _gmlp_kernel(u_ref, gv_ref, w_ref, bias_ref, avg_ref, lng_ref, lnb_ref, ob_ref, vn_ref):
    u = _gelu_tanh(u_ref[...])
    gv = _gelu_tanh(gv_ref[...])
    avg = avg_ref[...]
    mu = _dot_exact_rhs(gv, avg)
    gc = gv - mu
    var = _dot_exact_rhs(gc * gc, avg)
    vn = gc * lax.rsqrt(var + LN_EPS) * lng_ref[...] + lnb_ref[...]
    vn_ref[...] = vn
    vb = vn.astype(BF16)
    pair_w = 2 * B_GD
    lane = lax.broadcasted_iota(jnp.int32, (B_CHUNK, pair_w), 1)
    for pr in range(B_GROUPS // 2):
        sl = slice(pr * pair_w, (pr + 1) * pair_w)
        vp = vb[:, sl]
        zero = jnp.zeros_like(vp)
        mixed = (_dot(w_ref[2 * pr], jnp.where(lane < B_GD, vp, zero))
                 + _dot(w_ref[2 * pr + 1], jnp.where(lane >= B_GD, vp, zero)))
        ob_ref[:, sl] = u[:, sl] * (mixed + bias_ref[:, sl])


def _gmlp(h, w_sets, bias_sets, avg, ln_g, ln_b):
    n_chunks = h.shape[0] // B_CHUNK
    n_prompt_chunks = N_PROMPT // B_CHUNK
    w = B_GROUPS * B_GD
    which = lambda c: c // n_prompt_chunks
    fixed = lambda c: (0, 0)
    o_b, vn = pl.pallas_call(
        _gmlp_kernel,
        grid=(n_chunks,),
        in_specs=[pl.BlockSpec((B_CHUNK, w), lambda c: (c, 3)),
                  pl.BlockSpec((B_CHUNK, w), lambda c: (c, 4)),
                  pl.BlockSpec((None, B_GROUPS, B_CHUNK, B_CHUNK), lambda c: (which(c), 0, 0, 0)),
                  pl.BlockSpec((None, B_CHUNK, w), lambda c: (which(c), 0, 0)),
                  pl.BlockSpec((w, w), fixed),
                  pl.BlockSpec((1, w), fixed), pl.BlockSpec((1, w), fixed)],
        out_specs=[pl.BlockSpec((B_CHUNK, w), lambda c: (c, 0)),
                   pl.BlockSpec((B_CHUNK, w), lambda c: (which(c), 0))],
        out_shape=[jax.ShapeDtypeStruct((h.shape[0], w), F32),
                   jax.ShapeDtypeStruct((2 * B_CHUNK, w), F32)],
        compiler_params=_params(("arbitrary",), 24),
        name="gmlp_gate",
    )(h, h, w_sets, bias_sets, avg, ln_g.reshape(1, w), ln_b.reshape(1, w))
    return o_b, vn[B_CHUNK:]


def _pool_kernel(halo_ref, x_ref, w_ref, sc_ref, y_ref, *, rows, pos0, tiles_per_seq, fresh):
    t = pl.program_id(0) % tiles_per_seq
    halo = halo_ref[...]
    if fresh:
        halo = jnp.where(t == 0, 0.0, halo)
    x = x_ref[...]
    ext = jnp.concatenate([halo, x], axis=0)
    pos = pos0 + t * rows + lax.broadcasted_iota(jnp.int32, (rows, C_GD), 0)
    for gi, win in enumerate(POOL_WINDOWS):
        sl = slice(gi * C_GD, (gi + 1) * C_GD)
        s = ext[:, sl]
        sh = 1
        while sh < win:
            s = s + pltpu.roll(s, sh, 0)
            sh *= 2
        cnt = jnp.minimum(win, pos + 1).astype(F32)
        pooled = s[POOL_HALO:, :] / cnt - x[:, sl]
        y_ref[:, sl] = _dot(pooled.astype(BF16), w_ref[gi]) * sc_ref[:, sl]


def _pool(halo_src, x_src, x_col, w_bf16, scale, *, n_rows, rows, pos0, tiles_per_seq, fresh,
          halo_rows_per_tile):
    w = len(POOL_WINDOWS) * C_GD
    if fresh:
        step = rows // POOL_HALO
        halo_map = lambda i: (jnp.maximum(i * step - 1, 0), x_col)
    else:
        halo_map = lambda i: (i * (halo_rows_per_tile // POOL_HALO), 0)
    return pl.pallas_call(
        functools.partial(_pool_kernel, rows=rows, pos0=pos0, tiles_per_seq=tiles_per_seq,
                          fresh=fresh),
        grid=(n_rows // rows,),
        in_specs=[pl.BlockSpec((POOL_HALO, w), halo_map),
                  pl.BlockSpec((rows, w), lambda i: (i, x_col)),
                  pl.BlockSpec((len(POOL_WINDOWS), C_GD, C_GD), lambda i: (0, 0, 0)),
                  pl.BlockSpec((1, w), lambda i: (0, 0))],
        out_specs=pl.BlockSpec((rows, w), lambda i: (i, 0)),
        out_shape=jax.ShapeDtypeStruct((n_rows, w), F32),
        compiler_params=_params(("arbitrary",), 24),
        name="pool_mix",
    )(halo_src, x_src, w_bf16, scale.reshape(1, w))


def _hgrn_kernel(q_ref, f_ref, i_ref, g_ref, s0_ref, lb_ref, ng_ref, o_ref, sfin_ref, s_scr,
                 *, rows, valid):
    t = pl.program_id(1)
    c_rows = HGRN_CHUNK

    @pl.when(t == 0)
    def _():
        s_scr[...] = s0_ref[...]

    r_io = lax.broadcasted_iota(jnp.int32, (c_rows, c_rows), 0)
    c_io = lax.broadcasted_iota(jnp.int32, (c_rows, c_rows), 1)
    causal = r_io >= c_io
    ltri = jnp.where(causal, 1.0, 0.0).astype(BF16)
    lgrp = jnp.where(c_io < (r_io // HGRN_SUB) * HGRN_SUB, 1.0, 0.0).astype(BF16)
    eye = r_io == c_io
    row_id = lax.broadcasted_iota(jnp.int32, (c_rows, D_HK), 0)

    def chunk(c, carry):
        r0 = pl.multiple_of(c * c_rows, c_rows)
        for hd in range(D_HEADS):
            sl = slice(hd * D_HK, (hd + 1) * D_HK)
            lb = lb_ref[:, sl]
            f = lb + (1.0 - lb) * _sigmoid(f_ref[pl.ds(r0, c_rows), sl])
            logf = jnp.log(f)
            kk = 1.0 - f
            qr = q_ref[pl.ds(r0, c_rows), sl]
            q = qr * _sigmoid(qr)
            if valid < rows:
                live = (t * rows + r0 + row_id) < valid
                logf = jnp.where(live, logf, 0.0)
                kk = jnp.where(live, kk, 0.0)
            cg = _dot_exact_lhs(ltri, logf)
            ref = _dot_exact_lhs(lgrp, logf)
            vb = i_ref[pl.ds(r0, c_rows), sl].astype(BF16)
            state = s_scr[hd]
            o = _dot((q * jnp.exp(cg)).astype(BF16), state.astype(BF16))
            qd = (q * jnp.exp(cg - ref)).astype(BF16)
            blocks = []
            for i in range(c_rows // HGRN_SUB):
                ref_i = ref[i * HGRN_SUB:i * HGRN_SUB + 1, :]
                e = jnp.where(row_id < (i + 1) * HGRN_SUB, ref_i - cg, 0.0)
                k_i = (kk * jnp.exp(e)).astype(BF16)
                blocks.append(_dot_nt(qd[i * HGRN_SUB:(i + 1) * HGRN_SUB, :], k_i))
            attn = jnp.where(causal, jnp.concatenate(blocks, axis=0), 0.0)
            o = o + _dot(attn.astype(BF16), vb)
            g_last = cg[c_rows - 1:c_rows, :]
            kd = kk * jnp.exp(g_last - cg)
            decay_col = jnp.sum(
                jnp.where(eye, jnp.broadcast_to(jnp.exp(g_last), (c_rows, D_HK)), 0.0),
                axis=1, keepdims=True)
            s_scr[hd] = state * decay_col + _dot(kd.T.astype(BF16), vb)
            o = o * lax.rsqrt(jnp.mean(o * o, axis=-1, keepdims=True) + RMS_EPS) * ng_ref[...]
            gr = g_ref[pl.ds(r0, c_rows), sl]
            o_ref[pl.ds(r0, c_rows), sl] = o * (gr * _sigmoid(gr))
        return carry

    lax.fori_loop(0, rows // c_rows, chunk, 0)

    @pl.when(t == pl.num_programs(1) - 1)
    def _():
        sfin_ref[...] = s_scr[...]


def _hgrn(src, col0, s0, lb, norm_g, *, bsz, length, rows, valid):
    w = D_HEADS * D_HK
    tiles = length // rows
    col = lambda k: pl.BlockSpec((rows, w), lambda b, t: (b * tiles + t, col0 + k))
    state_spec = pl.BlockSpec((None, D_HEADS, D_HK, D_HK), lambda b, t: (b, 0, 0, 0))
    return pl.pallas_call(
        functools.partial(_hgrn_kernel, rows=rows, valid=valid),
        grid=(bsz, tiles),
        in_specs=[col(0), col(1), col(2), col(3), state_spec,
                  pl.BlockSpec((1, w), lambda b, t: (0, 0)),
                  pl.BlockSpec((1, D_HK), lambda b, t: (0, 0))],
        out_specs=[pl.BlockSpec((rows, w), lambda b, t: (b * tiles + t, 0)), state_spec],
        out_shape=[jax.ShapeDtypeStruct((bsz * length, w), F32),
                   jax.ShapeDtypeStruct((bsz, D_HEADS, D_HK, D_HK), F32)],
        scratch_shapes=[pltpu.VMEM((D_HEADS, D_HK, D_HK), F32)],
        compiler_params=_params(("parallel", "arbitrary"), 32),
        name="hgrn2",
    )(src, src, src, src, s0, lb.reshape(1, w), norm_g.reshape(1, D_HK))


def kernel(x_prompt, x_sample, cache_k, cache_v, state_pool, state_hgrn, page_table, w_in_even, w_out_even, gmlp_ws, gmlp_bs, gmlp_ln_g, gmlp_ln_b, w_in_odd, w_out_odd, pool_w, pool_scale, hgrn_lb_param, hgrn_norm_g, ln_mix_g, ln_mix_b, ln_ffn_g, ln_ffn_b, ffn_w1, ffn_w2):
    w = HALF_W
    nb = SEQ // MOBA_BLOCK
    x = jnp.concatenate([x_prompt.reshape(N_PROMPT, D_MODEL), x_sample.reshape(N_SAMPLE, D_MODEL)], 0)

    h, hb = _proj(x, w_in_even[0].astype(BF16), 3 * w)
    hp = hb[:N_PROMPT].reshape(BATCH, nb, MOBA_BLOCK, 3 * w)
    qt = hp[..., :w].transpose(0, 1, 3, 2)
    kb = hp[..., w:2 * w]
    vt = hp[..., 2 * w:].transpose(0, 1, 3, 2)
    oa_prompt = _moba_prompt(qt, kb, vt)

    hs = h[N_PROMPT:].reshape(DEC_BATCH, DEC_SEQ, EVEN_IN)
    pad_rows = ((0, 0), (0, PAGE_SIZE - DEC_SEQ), (0, 0))
    oa_sample = _moba_sample(page_table, hs[..., :w],
                             jnp.pad(hs[..., w:2 * w], pad_rows), jnp.pad(hs[..., 2 * w:3 * w], pad_rows),
                             cache_k[0].reshape(-1, PAGE_SIZE, w), cache_v[0].reshape(-1, PAGE_SIZE, w))
    o_a = jnp.concatenate([oa_prompt, oa_sample.reshape(N_SAMPLE, w)], 0)

    tril = jnp.tril(jnp.ones((B_CHUNK, B_CHUNK), bool))
    ws_prompt = jnp.where(tril[None], gmlp_ws[0], 0.0)
    ws_sample = jax.vmap(lambda m: jnp.kron(jnp.eye(DEC_BATCH, dtype=F32), m[:DEC_SEQ, :DEC_SEQ]))(ws_prompt)
    bias_prompt = jnp.repeat(gmlp_bs[0].T, B_GD, axis=1)
    bias_sample = jnp.tile(bias_prompt[:DEC_SEQ], (DEC_BATCH, 1))
    grp = jnp.arange(w) // B_GD
    avg = jnp.where(grp[:, None] == grp[None, :], 1.0 / B_GD, 0.0).astype(BF16)
    o_b, vn_sample = _gmlp(h, jnp.stack([ws_prompt, ws_sample]).astype(BF16),
                           jnp.stack([bias_prompt, bias_sample]), avg, gmlp_ln_g[0], gmlp_ln_b[0])

    x = _mix_out(o_a, o_b, x, w_out_even[0].astype(BF16), ln_mix_g[0], ln_mix_b[0])
    x = _ffn(x, ffn_w1[0].astype(BF16), ffn_w2[0].astype(BF16), ln_ffn_g[0], ln_ffn_b[0])

    new_k_prompt = h[:N_PROMPT, w:2 * w].reshape(1, BATCH, SEQ, A_HEADS, A_HD)
    new_v_prompt = h[:N_PROMPT, 2 * w:3 * w].reshape(1, BATCH, SEQ, A_HEADS, A_HD)
    new_k_sample = hs[..., w:2 * w].reshape(1, DEC_BATCH, DEC_SEQ, A_HEADS, A_HD)
    new_v_sample = hs[..., 2 * w:3 * w].reshape(1, DEC_BATCH, DEC_SEQ, A_HEADS, A_HD)
    new_gmlp_v_sample = vn_sample.reshape(1, DEC_BATCH, DEC_SEQ, w)

    h, _ = _proj(x, w_in_odd[0].astype(BF16), 0)
    lb = jax.nn.softmax(hgrn_lb_param.astype(F32), axis=0)[0]
    pool_wb = pool_w[0].astype(BF16)

    oc_prompt = _pool(h, h, 0, pool_wb, pool_scale[0], n_rows=N_PROMPT, rows=POOL_ROWS, pos0=0,
                      tiles_per_seq=SEQ // POOL_ROWS, fresh=True, halo_rows_per_tile=POOL_HALO)
    hs = h[N_PROMPT:].reshape(DEC_BATCH, DEC_SEQ, ODD_IN)
    xc_sample = hs[..., :w]
    halo_s = jnp.pad(state_pool[0], ((0, 0), (POOL_HALO - POOL_BUF, 0), (0, 0)))
    xs_pad = jnp.pad(xc_sample, ((0, 0), (0, SUBLANES - DEC_SEQ), (0, 0)))
    oc_sample = _pool(halo_s.reshape(DEC_BATCH * POOL_HALO, w), xs_pad.reshape(DEC_BATCH * SUBLANES, w), 0,
                      pool_wb, pool_scale[0], n_rows=DEC_BATCH * SUBLANES, rows=SUBLANES, pos0=PAST_LEN,
                      tiles_per_seq=1, fresh=False, halo_rows_per_tile=POOL_HALO)
    oc_sample = oc_sample.reshape(DEC_BATCH, SUBLANES, w)[:, :DEC_SEQ].reshape(N_SAMPLE, w)

    od_prompt, s_prompt = _hgrn(h, 1, jnp.zeros((BATCH, D_HEADS, D_HK, D_HK), F32), lb, hgrn_norm_g[0],
                                bsz=BATCH, length=SEQ, rows=HGRN_ROWS, valid=SEQ)
    hs_pad = jnp.pad(hs[..., w:], ((0, 0), (0, HGRN_CHUNK - DEC_SEQ), (0, 0)))
    od_sample, s_sample = _hgrn(hs_pad.reshape(DEC_BATCH * HGRN_CHUNK, 4 * w), 0, state_hgrn[0], lb,
                                hgrn_norm_g[0], bsz=DEC_BATCH, length=HGRN_CHUNK, rows=HGRN_CHUNK,
                                valid=DEC_SEQ)
    od_sample = od_sample.reshape(DEC_BATCH, HGRN_CHUNK, w)[:, :DEC_SEQ].reshape(N_SAMPLE, w)

    o_c = jnp.concatenate([oc_prompt, oc_sample], 0)
    o_d = jnp.concatenate([od_prompt, od_sample], 0)
    x = _mix_out(o_c, o_d, x, w_out_odd[0].astype(BF16), ln_mix_g[1], ln_mix_b[1])
    x = _ffn(x, ffn_w1[1].astype(BF16), ffn_w2[1].astype(BF16), ln_ffn_g[1], ln_ffn_b[1])

    new_pool_prompt = h[:N_PROMPT, :w].reshape(BATCH, SEQ, w)[:, SEQ - POOL_BUF:][None]
    new_pool_sample = jnp.concatenate([state_pool[0][:, DEC_SEQ:], xc_sample], axis=1)[None]
    return (x[:N_PROMPT].reshape(BATCH, SEQ, D_MODEL), x[N_PROMPT:].reshape(DEC_BATCH, DEC_SEQ, D_MODEL),
            new_k_prompt, new_v_prompt, new_k_sample, new_v_sample, new_gmlp_v_sample,
            new_pool_prompt, new_pool_sample, s_prompt[None], s_sample[None])
```

```python
import functools
import math

import jax
import jax.numpy as jnp
from jax import lax
from jax.experimental import pallas as pl
from jax.experimental.pallas import tpu as pltpu

F32 = jnp.float32
BF16 = jnp.bfloat16

D_MODEL = 1024
BATCH = 4
SEQ = 4096
DEPTH = 2
DEC_BATCH = 32
DEC_SEQ = 4
PAST_LEN = 8192
PAGE_SIZE = 128
HALF_W = D_MODEL // 2
A_HD = 64
A_HEADS = HALF_W // A_HD
MOBA_BLOCK = 256
MOBA_TOPK = 3
B_GROUPS = 8
B_GD = HALF_W // B_GROUPS
B_CHUNK = 128
POOL_WINDOWS = (2, 4, 8, 16)
C_GD = HALF_W // len(POOL_WINDOWS)
POOL_BUF = max(POOL_WINDOWS) - 1
D_HK = 128
D_HEADS = HALF_W // D_HK
D_FF = 4 * D_MODEL
EVEN_IN = 5 * HALF_W
ODD_IN = 5 * HALF_W
ALPHA = (2 * DEPTH) ** 0.25
LN_EPS = 1e-5
RMS_EPS = 1e-6
NEG = -1e30

N_PROMPT = BATCH * SEQ
N_SAMPLE = DEC_BATCH * DEC_SEQ
N_ROWS = N_PROMPT + N_SAMPLE
N_PAGES = PAST_LEN // PAGE_SIZE
N_PAST_BLOCKS = PAST_LEN // MOBA_BLOCK
PAGES_PER_BLOCK = MOBA_BLOCK // PAGE_SIZE

LANES = 128
SUBLANES = 8
ROW_TILE = 384
FF_CHUNK = 1024
PAGES_PER_STEP = 8
KEY_SUB = 64
SCORE_SCALE = (A_HD ** -0.5) * math.log2(math.e)
HGRN_CHUNK = 128
HGRN_SUB = 16
HGRN_ROWS = 512
POOL_ROWS = 512
POOL_HALO = 16
MIB = 1024 * 1024


def _dot(a, b):
    return jnp.dot(a, b, preferred_element_type=F32)


def _dot_nt(a, b):
    return lax.dot_general(a, b, (((1,), (1,)), ((), ())), preferred_element_type=F32)


def _split3(x):
    p0 = x.astype(BF16)
    r1 = x - p0.astype(F32)
    p1 = r1.astype(BF16)
    p2 = (r1 - p1.astype(F32)).astype(BF16)
    return p0, p1, p2


def _dot_exact_lhs(m_bf16, x):
    p0, p1, p2 = _split3(x)
    return _dot(m_bf16, p0) + _dot(m_bf16, p1) + _dot(m_bf16, p2)


def _dot_exact_rhs(x, m_bf16):
    p0, p1, p2 = _split3(x)
    return _dot(p0, m_bf16) + _dot(p1, m_bf16) + _dot(p2, m_bf16)


def _gelu_tanh(x):
    return 0.5 * x * (1.0 + jnp.tanh(0.7978845608028654 * (x + 0.044715 * (x * x * x))))


def _sigmoid(x):
    return 1.0 / (1.0 + jnp.exp(-x))


def _layer_norm(z, g, b):
    mu = jnp.mean(z, axis=-1, keepdims=True)
    zc = z - mu
    var = jnp.mean(zc * zc, axis=-1, keepdims=True)
    return zc * lax.rsqrt(var + LN_EPS) * g + b


def _params(sem, vmem_mib):
    return pltpu.CompilerParams(dimension_semantics=sem, vmem_limit_bytes=int(vmem_mib * MIB))


def _proj_kernel(x_ref, w_ref, h_ref, *hb_refs, n_bf16):
    xb = x_ref[...].astype(BF16)
    n_out = w_ref.shape[1]
    for c in range(n_out // HALF_W):
        sl = slice(c * HALF_W, (c + 1) * HALF_W)
        h = _dot(xb, w_ref[:, sl])
        h_ref[:, sl] = h
        if c * HALF_W < n_bf16:
            hb_refs[0][:, sl] = h.astype(BF16)


def _proj(x, w_bf16, n_bf16):
    n, d = x.shape
    n_out = w_bf16.shape[1]
    out_shape = [jax.ShapeDtypeStruct((n, n_out), F32)]
    out_specs = [pl.BlockSpec((ROW_TILE, n_out), lambda i: (i, 0))]
    if n_bf16:
        out_shape.append(jax.ShapeDtypeStruct((n, n_bf16), BF16))
        out_specs.append(pl.BlockSpec((ROW_TILE, n_bf16), lambda i: (i, 0)))
    vmem = 2 * (ROW_TILE * d * 4 + d * n_out * 2 + ROW_TILE * n_out * 4 + ROW_TILE * n_bf16 * 2)
    res = pl.pallas_call(
        functools.partial(_proj_kernel, n_bf16=n_bf16),
        grid=(n // ROW_TILE,),
        in_specs=[pl.BlockSpec((ROW_TILE, d), lambda i: (i, 0)),
                  pl.BlockSpec((d, n_out), lambda i: (0, 0))],
        out_specs=out_specs,
        out_shape=out_shape,
        compiler_params=_params(("parallel",), vmem / MIB + 8),
        name="in_proj",
    )(x, w_bf16)
    return res if n_bf16 else (res[0], None)


def _mix_out_kernel(oa_ref, ob_ref, x_ref, w_ref, g_ref, b_ref, y_ref):
    acc = _dot(oa_ref[...].astype(BF16), w_ref[:HALF_W, :])
    acc = acc + _dot(ob_ref[...].astype(BF16), w_ref[HALF_W:, :])
    y_ref[...] = _layer_norm(ALPHA * x_ref[...] + acc, g_ref[...], b_ref[...])


def _mix_out(oa, ob, x, w_bf16, g, b):
    n, d = x.shape
    row = lambda i: (i, 0)
    fixed = lambda i: (0, 0)
    vmem = 2 * (2 * ROW_TILE * HALF_W * 4 + 2 * ROW_TILE * d * 4 + d * d * 2)
    return pl.pallas_call(
        _mix_out_kernel,
        grid=(n // ROW_TILE,),
        in_specs=[pl.BlockSpec((ROW_TILE, HALF_W), row), pl.BlockSpec((ROW_TILE, HALF_W), row),
                  pl.BlockSpec((ROW_TILE, d), row), pl.BlockSpec((d, d), fixed),
                  pl.BlockSpec((1, d), fixed), pl.BlockSpec((1, d), fixed)],
        out_specs=pl.BlockSpec((ROW_TILE, d), row),
        out_shape=jax.ShapeDtypeStruct((n, d), F32),
        compiler_params=_params(("parallel",), vmem / MIB + 8),
        name="mix_out_ln",
    )(oa, ob, x, w_bf16, g.reshape(1, d), b.reshape(1, d))


def _ffn_kernel(x_ref, w1_ref, w2_ref, g_ref, b_ref, y_ref):
    x = x_ref[...]
    xb = x.astype(BF16)
    acc = jnp.zeros(x.shape, F32)
    for c in range(D_FF // FF_CHUNK):
        sl = slice(c * FF_CHUNK, (c + 1) * FF_CHUNK)
        hid = jnp.maximum(_dot(xb, w1_ref[:, sl]), 0.0)
        acc = acc + _dot((hid * hid).astype(BF16), w2_ref[sl, :])
    y_ref[...] = _layer_norm(ALPHA * x + acc, g_ref[...], b_ref[...])


def _ffn(x, w1_bf16, w2_bf16, g, b):
    n, d = x.shape
    row = lambda i: (i, 0)
    fixed = lambda i: (0, 0)
    vmem = 2 * (2 * ROW_TILE * d * 4 + 2 * d * D_FF * 2) + 3 * ROW_TILE * FF_CHUNK * 4
    return pl.pallas_call(
        _ffn_kernel,
        grid=(n // ROW_TILE,),
        in_specs=[pl.BlockSpec((ROW_TILE, d), row),
                  pl.BlockSpec((d, D_FF), fixed), pl.BlockSpec((D_FF, d), fixed),
                  pl.BlockSpec((1, d), fixed), pl.BlockSpec((1, d), fixed)],
        out_specs=pl.BlockSpec((ROW_TILE, d), row),
        out_shape=jax.ShapeDtypeStruct((n, d), F32),
        compiler_params=_params(("parallel",), vmem / MIB + 8),
        name="ffn_ln",
    )(x, w1_bf16, w2_bf16, g.reshape(1, d), b.reshape(1, d))


def _moba_prompt_kernel(qt_ref, k_ref, vt_ref, o_ref,
                        kmt_scr, qtz_scr, selb_scr, m_scr, l_scr, acc_scr):
    j = pl.program_id(1)
    nb = k_ref.shape[0]
    blk = MOBA_BLOCK
    pair_w = 2 * A_HD

    @pl.when(j == 0)
    def _():
        rows = [jnp.sum(k_ref[n].astype(F32), axis=0, keepdims=True) * (1.0 / blk)
                for n in range(nb)]
        kmean = jnp.concatenate(rows, axis=0)
        tiled = jnp.concatenate([kmean] * A_HEADS, axis=0)
        rh = lax.broadcasted_iota(jnp.int32, tiled.shape, 0) // nb
        ch = lax.broadcasted_iota(jnp.int32, tiled.shape, 1) // A_HD
        kmt_scr[...] = jnp.where(rh == ch, tiled, 0.0)

    qts = qt_ref[...]
    k0, k1, k2 = _split3(kmt_scr[...])
    gate = _dot(k0, qts) + _dot(k1, qts) + _dot(k2, qts)

    n_io = lax.broadcasted_iota(jnp.int32, (nb, blk), 0)
    half = lax.broadcasted_iota(jnp.int32, (pair_w, blk), 0) // A_HD
    for h in range(A_HEADS):
        g = jnp.where(n_io < j, gate[h * nb:(h + 1) * nb, :], NEG)
        bias = jnp.full((nb, blk), NEG, F32)
        for _ in range(MOBA_TOPK):
            mx = jnp.max(g, axis=0, keepdims=True)
            cand = jnp.where((g == mx) & (mx > 0.5 * NEG), n_io, nb)
            pick = n_io == jnp.min(cand, axis=0, keepdims=True)
            bias = jnp.where(pick, 0.0, bias)
            g = jnp.where(pick, NEG, g)
        selb_scr[h * nb:(h + 1) * nb, :] = bias
        pr = h // 2
        qpair = qts[pr * pair_w:(pr + 1) * pair_w, :]
        qtz_scr[h] = jnp.where(half == (h % 2), qpair, jnp.zeros_like(qpair))

    def scores(n, h, sub):
        pr = h // 2
        return _dot(k_ref[n, sub * KEY_SUB:(sub + 1) * KEY_SUB, pr * pair_w:(pr + 1) * pair_w],
                    qtz_scr[h])

    def update(n, h, sub, s, first):
        ks = slice(sub * KEY_SUB, (sub + 1) * KEY_SUB)
        hs = slice(h * A_HD, (h + 1) * A_HD)
        s_max = jnp.max(s, axis=0, keepdims=True)
        if first:
            m_new = s_max
        else:
            m_old = m_scr[h]
            m_new = jnp.maximum(m_old, s_max)
            a = jnp.exp2(m_old - m_new)
        p = jnp.exp2(s - m_new)
        p_sum = jnp.sum(p, axis=0, keepdims=True)
        pv = _dot(vt_ref[n, hs, ks], p.astype(BF16))
        m_scr[h] = m_new
        if first:
            l_scr[h] = p_sum
            acc_scr[hs, :] = pv
        else:
            l_scr[h] = a * l_scr[h] + p_sum
            acc_scr[hs, :] = a * acc_scr[hs, :] + pv

    k_io = lax.broadcasted_iota(jnp.int32, (KEY_SUB, blk), 0)
    q_io = lax.broadcasted_iota(jnp.int32, (KEY_SUB, blk), 1)
    n_sub = blk // KEY_SUB
    own_steps = [(h, sub) for sub in range(n_sub) for h in range(A_HEADS)]
    past_steps = [(h, sub) for h in range(A_HEADS) for sub in range(n_sub)]

    def run_steps(n, steps, mask, first_sub0):
        s_next = scores(n, *steps[0])
        for idx, (h, sub) in enumerate(steps):
            s_cur = s_next
            if idx + 1 < len(steps):
                s_next = scores(n, *steps[idx + 1])
            update(n, h, sub, mask(h, sub, s_cur), first=first_sub0 and sub == 0)

    run_steps(j, own_steps,
              lambda h, sub, s: jnp.where((k_io + sub * KEY_SUB) <= q_io, s, NEG), True)

    def past_block(n, carry):
        run_steps(n, past_steps,
                  lambda h, sub, s: s + selb_scr[pl.ds(h * nb + n, 1), :], False)
        return carry

    lax.fori_loop(0, j, past_block, 0)

    for h in range(A_HEADS):
        hs = slice(h * A_HD, (h + 1) * A_HD)
        acc_scr[hs, :] = acc_scr[hs, :] / l_scr[h]
    o_ref[...] = acc_scr[...].T


def _moba_prompt(qt, kb, vt):
    bsz, nb = qt.shape[:2]
    w = A_HEADS * A_HD
    vmem = 2 * (2 * nb * MOBA_BLOCK * w * 2 + w * MOBA_BLOCK * 2 + MOBA_BLOCK * w * 4) + 4 * MIB
    return pl.pallas_call(
        _moba_prompt_kernel,
        grid=(bsz, nb),
        in_specs=[pl.BlockSpec((None, None, w, MOBA_BLOCK), lambda b, j: (b, j, 0, 0)),
                  pl.BlockSpec((None, nb, MOBA_BLOCK, w), lambda b, j: (b, 0, 0, 0)),
                  pl.BlockSpec((None, nb, w, MOBA_BLOCK), lambda b, j: (b, 0, 0, 0))],
        out_specs=pl.BlockSpec((MOBA_BLOCK, w), lambda b, j: (b * nb + j, 0)),
        out_shape=jax.ShapeDtypeStruct((bsz * nb * MOBA_BLOCK, w), F32),
        scratch_shapes=[pltpu.VMEM((A_HEADS * nb, w), F32),
                        pltpu.VMEM((A_HEADS, 2 * A_HD, MOBA_BLOCK), BF16),
                        pltpu.VMEM((A_HEADS * nb, MOBA_BLOCK), F32),
                        pltpu.VMEM((A_HEADS, 1, MOBA_BLOCK), F32),
                        pltpu.VMEM((A_HEADS, 1, MOBA_BLOCK), F32),
                        pltpu.VMEM((w, MOBA_BLOCK), F32)],
        compiler_params=_params(("parallel", "arbitrary"), vmem / MIB + 8),
        name="moba_prompt",
    )(qt, kb, vt)


def _moba_sample_kernel(pt_ref, q_ref, kn_ref, vn_ref, *refs):
    del pt_ref
    k_refs = refs[:PAGES_PER_STEP]
    v_refs = refs[PAGES_PER_STEP:2 * PAGES_PER_STEP]
    o_ref = refs[2 * PAGES_PER_STEP]
    m_scr, l_scr, gs_scr, gate_scr, acc_scr = refs[2 * PAGES_PER_STEP + 1:]
    g = pl.program_id(1)
    nrow = DEC_SEQ * A_HEADS
    ncol = PAGE_SIZE * A_HEADS
    wide = lambda c: jnp.broadcast_to(c, (nrow, LANES))

    qm = q_ref[...].astype(BF16)
    same_head = ((lax.broadcasted_iota(jnp.int32, (nrow, ncol), 0) % A_HEADS)
                 == (lax.broadcasted_iota(jnp.int32, (nrow, ncol), 1) % A_HEADS))

    for i in range(PAGES_PER_STEP):
        page = g * PAGES_PER_STEP + i
        k2 = k_refs[i][...].reshape(ncol, A_HD).astype(BF16)
        v2 = v_refs[i][...].reshape(ncol, A_HD).astype(BF16)
        s = _dot_nt(qm, k2)
        sm = jnp.where(same_head, s, NEG)
        m = jnp.max(sm, axis=-1, keepdims=True)
        p = jnp.exp2(sm - m)
        m_scr[page] = wide(m)
        l_scr[page] = wide(jnp.sum(p, axis=-1, keepdims=True))
        gs_scr[page] = wide(jnp.sum(jnp.where(same_head, s, 0.0), axis=-1, keepdims=True))
        acc_scr[page] = _dot(p.astype(BF16), v2)

    @pl.when(g == pl.num_programs(1) - 1)
    def _():
        s = _dot_nt(qm, kn_ref[...].astype(BF16))
        r_io = lax.broadcasted_iota(jnp.int32, (nrow, nrow), 0)
        c_io = lax.broadcasted_iota(jnp.int32, (nrow, nrow), 1)
        ok = ((r_io % A_HEADS) == (c_io % A_HEADS)) & (c_io // A_HEADS <= r_io // A_HEADS)
        s = jnp.where(ok, s, NEG)
        m_own = wide(jnp.max(s, axis=-1, keepdims=True))
        p = jnp.exp2(s - m_own[:, :nrow])
        l_own = wide(jnp.sum(p, axis=-1, keepdims=True))
        o_own = _dot(p.astype(BF16), vn_ref[...].astype(BF16))

        for n in range(N_PAST_BLOCKS):
            tot = gs_scr[PAGES_PER_BLOCK * n]
            for r in range(1, PAGES_PER_BLOCK):
                tot = tot + gs_scr[PAGES_PER_BLOCK * n + r]
            gate_scr[n] = tot * (1.0 / MOBA_BLOCK)
        for _ in range(MOBA_TOPK):
            best = jnp.full((nrow, LANES), NEG, F32)
            bidx = jnp.zeros((nrow, LANES), jnp.int32)
            for n in range(N_PAST_BLOCKS):
                gn = gate_scr[n]
                upd = gn > best
                best = jnp.where(upd, gn, best)
                bidx = jnp.where(upd, n, bidx)
            for n in range(N_PAST_BLOCKS):
                gate_scr[n] = jnp.where(bidx == n, -jnp.inf, gate_scr[n])

        m_all = m_own
        for pg in range(N_PAGES):
            picked = gate_scr[pg // PAGES_PER_BLOCK] == -jnp.inf
            m_all = jnp.maximum(m_all, jnp.where(picked, m_scr[pg], NEG))
        w_own = jnp.exp2(m_own - m_all)
        l_all = w_own * l_own
        o_all = w_own[:, :A_HD] * o_own
        for pg in range(N_PAGES):
            picked = gate_scr[pg // PAGES_PER_BLOCK] == -jnp.inf
            wp = jnp.where(picked, jnp.exp2(m_scr[pg] - m_all), 0.0)
            l_all = l_all + wp * l_scr[pg]
            o_all = o_all + wp[:, :A_HD] * acc_scr[pg]
        o_ref[...] = o_all / l_all[:, :A_HD]


def _moba_sample(page_table, q, k_new, v_new, cache_k, cache_v):
    nrow = DEC_SEQ * A_HEADS
    page_spec = lambda i: pl.BlockSpec(
        (None, PAGE_SIZE, A_HEADS, A_HD),
        lambda b, g, pt: (pt[b, g * PAGES_PER_STEP + i], 0, 0, 0))
    per_sample = pl.BlockSpec((None, nrow, A_HD), lambda b, g, pt: (b, 0, 0))
    grid_spec = pltpu.PrefetchScalarGridSpec(
        num_scalar_prefetch=1,
        grid=(DEC_BATCH, N_PAGES // PAGES_PER_STEP),
        in_specs=([per_sample, per_sample, per_sample]
                  + [page_spec(i) for i in range(PAGES_PER_STEP)]
                  + [page_spec(i) for i in range(PAGES_PER_STEP)]),
        out_specs=per_sample,
        scratch_shapes=[pltpu.VMEM((N_PAGES, nrow, LANES), F32),
                        pltpu.VMEM((N_PAGES, nrow, LANES), F32),
                        pltpu.VMEM((N_PAGES, nrow, LANES), F32),
                        pltpu.VMEM((N_PAST_BLOCKS, nrow, LANES), F32),
                        pltpu.VMEM((N_PAGES, nrow, A_HD), F32)],
    )
    page_vmem = PAGE_SIZE * A_HEADS * LANES * 4
    vmem = 2 * 2 * PAGES_PER_STEP * page_vmem + 5 * N_PAGES * nrow * LANES * 4
    return pl.pallas_call(
        _moba_sample_kernel,
        grid_spec=grid_spec,
        out_shape=jax.ShapeDtypeStruct((DEC_BATCH, nrow, A_HD), F32),
        compiler_params=_params(("parallel", "arbitrary"), vmem / MIB + 8),
        name="moba_sample",
    )(page_table, q, k_new, v_new,
      *([cache_k] * PAGES_PER_STEP), *([cache_v] * PAGES_PER_STEP))


def _gmlp_kernel(u_ref, gv_ref, w_ref, bias_ref, avg_ref, lng_ref, lnb_ref, ob_ref, vn_ref):
    u = _gelu_tanh(u_ref[...])
    gv = _gelu_tanh(gv_ref[...])
    avg = avg_ref[...]
    mu = _dot_exact_rhs(gv, avg)
    gc = gv - mu
    var = _dot_exact_rhs(gc * gc, avg)
    vn = gc * lax.rsqrt(var + LN_EPS) * lng_ref[...] + lnb_ref[...]
    vn_ref[...] = vn
    vb = vn.astype(BF16)
    pair_w = 2 * B_GD
    lane = lax.broadcasted_iota(jnp.int32, (B_CHUNK, pair_w), 1)
    for pr in range(B_GROUPS // 2):
        sl = slice(pr * pair_w, (pr + 1) * pair_w)
        vp = vb[:, sl]
        zero = jnp.zeros_like(vp)
        mixed = (_dot(w_ref[2 * pr], jnp.where(lane < B_GD, vp, zero))
                 + _dot(w_ref[2 * pr + 1], jnp.where(lane >= B_GD, vp, zero)))
        ob_ref[:, sl] = u[:, sl] * (mixed + bias_ref[:, sl])


def _gmlp(h, w_sets, bias_sets, avg, ln_g, ln_b):
    n_chunks = h.shape[0] // B_CHUNK
    n_prompt_chunks = N_PROMPT // B_CHUNK
    w = B_GROUPS * B_GD
    which = lambda c: c // n_prompt_chunks
    fixed = lambda c: (0, 0)
    o_b, vn = pl.pallas_call(
        _gmlp_kernel,
        grid=(n_chunks,),
        in_specs=[pl.BlockSpec((B_CHUNK, w), lambda c: (c, 3)),
                  pl.BlockSpec((B_CHUNK, w), lambda c: (c, 4)),
                  pl.BlockSpec((None, B_GROUPS, B_CHUNK, B_CHUNK), lambda c: (which(c), 0, 0, 0)),
                  pl.BlockSpec((None, B_CHUNK, w), lambda c: (which(c), 0, 0)),
                  pl.BlockSpec((w, w), fixed),
                  pl.BlockSpec((1, w), fixed), pl.BlockSpec((1, w), fixed)],
        out_specs=[pl.BlockSpec((B_CHUNK, w), lambda c: (c, 0)),
                   pl.BlockSpec((B_CHUNK, w), lambda c: (which(c), 0))],
        out_shape=[jax.ShapeDtypeStruct((h.shape[0], w), F32),
                   jax.ShapeDtypeStruct((2 * B_CHUNK, w), F32)],
        compiler_params=_params(("arbitrary",), 24),
        name="gmlp_gate",
    )(h, h, w_sets, bias_sets, avg, ln_g.reshape(1, w), ln_b.reshape(1, w))
    return o_b, vn[B_CHUNK:]


def _pool_kernel(halo_ref, x_ref, w_ref, sc_ref, y_ref, *, rows, pos0, tiles_per_seq, fresh):
    t = pl.program_id(0) % tiles_per_seq
    halo = halo_ref[...]
    if fresh:
        halo = jnp.where(t == 0, 0.0, halo)
    x = x_ref[...]
    ext = jnp.concatenate([halo, x], axis=0)
    pos = pos0 + t * rows + lax.broadcasted_iota(jnp.int32, (rows, C_GD), 0)
    for gi, win in enumerate(POOL_WINDOWS):
        sl = slice(gi * C_GD, (gi + 1) * C_GD)
        s = ext[:, sl]
        sh = 1
        while sh < win:
            s = s + pltpu.roll(s, sh, 0)
            sh *= 2
        cnt = jnp.minimum(win, pos + 1).astype(F32)
        pooled = s[POOL_HALO:, :] / cnt - x[:, sl]
        y_ref[:, sl] = _dot(pooled.astype(BF16), w_ref[gi]) * sc_ref[:, sl]


def _pool(halo_src, x_src, x_col, w_bf16, scale, *, n_rows, rows, pos0, tiles_per_seq, fresh,
          halo_rows_per_tile):
    w = len(POOL_WINDOWS) * C_GD
    if fresh:
        step = rows // POOL_HALO
        halo_map = lambda i: (jnp.maximum(i * step - 1, 0), x_col)
    else:
        halo_map = lambda i: (i * (halo_rows_per_tile // POOL_HALO), 0)
    return pl.pallas_call(
        functools.partial(_pool_kernel, rows=rows, pos0=pos0, tiles_per_seq=tiles_per_seq,
                          fresh=fresh),
        grid=(n_rows // rows,),
        in_specs=[pl.BlockSpec((POOL_HALO, w), halo_map),
                  pl.BlockSpec((rows, w), lambda i: (i, x_col)),
                  pl.BlockSpec((len(POOL_WINDOWS), C_GD, C_GD), lambda i: (0, 0, 0)),
                  pl.BlockSpec((1, w), lambda i: (0, 0))],
        out_specs=pl.BlockSpec((rows, w), lambda i: (i, 0)),
        out_shape=jax.ShapeDtypeStruct((n_rows, w), F32),
        compiler_params=_params(("arbitrary",), 24),
        name="pool_mix",
    )(halo_src, x_src, w_bf16, scale.reshape(1, w))


def _hgrn_kernel(q_ref, f_ref, i_ref, g_ref, s0_ref, lb_ref, ng_ref, o_ref, sfin_ref, s_scr,
                 *, rows, valid):
    t = pl.program_id(1)
    c_rows = HGRN_CHUNK

    @pl.when(t == 0)
    def _():
        s_scr[...] = s0_ref[...]

    r_io = lax.broadcasted_iota(jnp.int32, (c_rows, c_rows), 0)
    c_io = lax.broadcasted_iota(jnp.int32, (c_rows, c_rows), 1)
    causal = r_io >= c_io
    ltri = jnp.where(causal, 1.0, 0.0).astype(BF16)
    lgrp = jnp.where(c_io < (r_io // HGRN_SUB) * HGRN_SUB, 1.0, 0.0).astype(BF16)
    eye = r_io == c_io
    row_id = lax.broadcasted_iota(jnp.int32, (c_rows, D_HK), 0)

    def chunk(c, carry):
        r0 = pl.multiple_of(c * c_rows, c_rows)
        for hd in range(D_HEADS):
            sl = slice(hd * D_HK, (hd + 1) * D_HK)
            lb = lb_ref[:, sl]
            f = lb + (1.0 - lb) * _sigmoid(f_ref[pl.ds(r0, c_rows), sl])
            logf = jnp.log(f)
            kk = 1.0 - f
            qr = q_ref[pl.ds(r0, c_rows), sl]
            q = qr * _sigmoid(qr)
            if valid < rows:
                live = (t * rows + r0 + row_id) < valid
                logf = jnp.where(live, logf, 0.0)
                kk = jnp.where(live, kk, 0.0)
            cg = _dot_exact_lhs(ltri, logf)
            ref = _dot_exact_lhs(lgrp, logf)
            vb = i_ref[pl.ds(r0, c_rows), sl].astype(BF16)
            state = s_scr[hd]
            o = _dot((q * jnp.exp(cg)).astype(BF16), state.astype(BF16))
            qd = (q * jnp.exp(cg - ref)).astype(BF16)
            blocks = []
            for i in range(c_rows // HGRN_SUB):
                ref_i = ref[i * HGRN_SUB:i * HGRN_SUB + 1, :]
                e = jnp.where(row_id < (i + 1) * HGRN_SUB, ref_i - cg, 0.0)
                k_i = (kk * jnp.exp(e)).astype(BF16)
                blocks.append(_dot_nt(qd[i * HGRN_SUB:(i + 1) * HGRN_SUB, :], k_i))
            attn = jnp.where(causal, jnp.concatenate(blocks, axis=0), 0.0)
            o = o + _dot(attn.astype(BF16), vb)
            g_last = cg[c_rows - 1:c_rows, :]
            kd = kk * jnp.exp(g_last - cg)
            decay_col = jnp.sum(
                jnp.where(eye, jnp.broadcast_to(jnp.exp(g_last), (c_rows, D_HK)), 0.0),
                axis=1, keepdims=True)
            s_scr[hd] = state * decay_col + _dot(kd.T.astype(BF16), vb)
            o = o * lax.rsqrt(jnp.mean(o * o, axis=-1, keepdims=True) + RMS_EPS) * ng_ref[...]
            gr = g_ref[pl.ds(r0, c_rows), sl]
            o_ref[pl.ds(r0, c_rows), sl] = o * (gr * _sigmoid(gr))
        return carry

    lax.fori_loop(0, rows // c_rows, chunk, 0)

    @pl.when(t == pl.num_programs(1) - 1)
    def _():
        sfin_ref[...] = s_scr[...]


def _hgrn(src, col0, s0, lb, norm_g, *, bsz, length, rows, valid):
    w = D_HEADS * D_HK
    tiles = length // rows
    col = lambda k: pl.BlockSpec((rows, w), lambda b, t: (b * tiles + t, col0 + k))
    state_spec = pl.BlockSpec((None, D_HEADS, D_HK, D_HK), lambda b, t: (b, 0, 0, 0))
    return pl.pallas_call(
        functools.partial(_hgrn_kernel, rows=rows, valid=valid),
        grid=(bsz, tiles),
        in_specs=[col(0), col(1), col(2), col(3), state_spec,
                  pl.BlockSpec((1, w), lambda b, t: (0, 0)),
                  pl.BlockSpec((1, D_HK), lambda b, t: (0, 0))],
        out_specs=[pl.BlockSpec((rows, w), lambda b, t: (b * tiles + t, 0)), state_spec],
        out_shape=[jax.ShapeDtypeStruct((bsz * length, w), F32),
                   jax.ShapeDtypeStruct((bsz, D_HEADS, D_HK, D_HK), F32)],
        scratch_shapes=[pltpu.VMEM((D_HEADS, D_HK, D_HK), F32)],
        compiler_params=_params(("parallel", "arbitrary"), 32),
        name="hgrn2",
    )(src, src, src, src, s0, lb.reshape(1, w), norm_g.reshape(1, D_HK))


def kernel(x_prompt, x_sample, cache_k, cache_v, state_pool, state_hgrn, page_table, w_in_even, w_out_even, gmlp_ws, gmlp_bs, gmlp_ln_g, gmlp_ln_b, w_in_odd, w_out_odd, pool_w, pool_scale, hgrn_lb_param, hgrn_norm_g, ln_mix_g, ln_mix_b, ln_ffn_g, ln_ffn_b, ffn_w1, ffn_w2):
    w = HALF_W
    nb = SEQ // MOBA_BLOCK
    x = jnp.concatenate([x_prompt.reshape(N_PROMPT, D_MODEL), x_sample.reshape(N_SAMPLE, D_MODEL)], 0)

    col_scale = jnp.where(jnp.arange(EVEN_IN) < w, SCORE_SCALE, 1.0)
    h, hb = _proj(x, (w_in_even[0] * col_scale).astype(BF16), 3 * w)
    hp = hb[:N_PROMPT].reshape(BATCH, nb, MOBA_BLOCK, 3 * w)
    qt = hp[..., :w].transpose(0, 1, 3, 2)
    kb = hp[..., w:2 * w]
    vt = hp[..., 2 * w:].transpose(0, 1, 3, 2)
    oa_prompt = _moba_prompt(qt, kb, vt)

    hs = h[N_PROMPT:].reshape(DEC_BATCH, DEC_SEQ, EVEN_IN)
    per_head = lambda t: t.reshape(DEC_BATCH, DEC_SEQ * A_HEADS, A_HD)
    oa_sample = _moba_sample(page_table, per_head(hs[..., :w]), per_head(hs[..., w:2 * w]),
                             per_head(hs[..., 2 * w:3 * w]), cache_k[0], cache_v[0])
    o_a = jnp.concatenate([oa_prompt, oa_sample.reshape(N_SAMPLE, w)], 0)

    tril = jnp.tril(jnp.ones((B_CHUNK, B_CHUNK), bool))
    ws_prompt = jnp.where(tril[None], gmlp_ws[0], 0.0)
    ws_sample = jax.vmap(lambda m: jnp.kron(jnp.eye(DEC_BATCH, dtype=F32), m[:DEC_SEQ, :DEC_SEQ]))(ws_prompt)
    bias_prompt = jnp.repeat(gmlp_bs[0].T, B_GD, axis=1)
    bias_sample = jnp.tile(bias_prompt[:DEC_SEQ], (DEC_BATCH, 1))
    grp = jnp.arange(w) // B_GD
    avg = jnp.where(grp[:, None] == grp[None, :], 1.0 / B_GD, 0.0).astype(BF16)
    o_b, vn_sample = _gmlp(h, jnp.stack([ws_prompt, ws_sample]).astype(BF16),
                           jnp.stack([bias_prompt, bias_sample]), avg, gmlp_ln_g[0], gmlp_ln_b[0])

    x = _mix_out(o_a, o_b, x, w_out_even[0].astype(BF16), ln_mix_g[0], ln_mix_b[0])
    x = _ffn(x, ffn_w1[0].astype(BF16), ffn_w2[0].astype(BF16), ln_ffn_g[0], ln_ffn_b[0])

    new_k_prompt = h[:N_PROMPT, w:2 * w].reshape(1, BATCH, SEQ, A_HEADS, A_HD)
    new_v_prompt = h[:N_PROMPT, 2 * w:3 * w].reshape(1, BATCH, SEQ, A_HEADS, A_HD)
    new_k_sample = hs[..., w:2 * w].reshape(1, DEC_BATCH, DEC_SEQ, A_HEADS, A_HD)
    new_v_sample = hs[..., 2 * w:3 * w].reshape(1, DEC_BATCH, DEC_SEQ, A_HEADS, A_HD)
    new_gmlp_v_sample = vn_sample.reshape(1, DEC_BATCH, DEC_SEQ, w)

    h, _ = _proj(x, w_in_odd[0].astype(BF16), 0)
    lb = jax.nn.softmax(hgrn_lb_param.astype(F32), axis=0)[0]
    pool_wb = pool_w[0].astype(BF16)

    oc_prompt = _pool(h, h, 0, pool_wb, pool_scale[0], n_rows=N_PROMPT, rows=POOL_ROWS, pos0=0,
                      tiles_per_seq=SEQ // POOL_ROWS, fresh=True, halo_rows_per_tile=POOL_HALO)
    hs = h[N_PROMPT:].reshape(DEC_BATCH, DEC_SEQ, ODD_IN)
    xc_sample = hs[..., :w]
    halo_s = jnp.pad(state_pool[0], ((0, 0), (POOL_HALO - POOL_BUF, 0), (0, 0)))
    xs_pad = jnp.pad(xc_sample, ((0, 0), (0, SUBLANES - DEC_SEQ), (0, 0)))
    oc_sample = _pool(halo_s.reshape(DEC_BATCH * POOL_HALO, w), xs_pad.reshape(DEC_BATCH * SUBLANES, w), 0,
                      pool_wb, pool_scale[0], n_rows=DEC_BATCH * SUBLANES, rows=SUBLANES, pos0=PAST_LEN,
                      tiles_per_seq=1, fresh=False, halo_rows_per_tile=POOL_HALO)
    oc_sample = oc_sample.reshape(DEC_BATCH, SUBLANES, w)[:, :DEC_SEQ].reshape(N_SAMPLE, w)

    od_prompt, s_prompt = _hgrn(h, 1, jnp.zeros((BATCH, D_HEADS, D_HK, D_HK), F32), lb, hgrn_norm_g[0],
                                bsz=BATCH, length=SEQ, rows=HGRN_ROWS, valid=SEQ)
    hs_pad = jnp.pad(hs[..., w:], ((0, 0), (0, HGRN_CHUNK - DEC_SEQ), (0, 0)))
    od_sample, s_sample = _hgrn(hs_pad.reshape(DEC_BATCH * HGRN_CHUNK, 4 * w), 0, state_hgrn[0], lb,
                                hgrn_norm_g[0], bsz=DEC_BATCH, length=HGRN_CHUNK, rows=HGRN_CHUNK,
                                valid=DEC_SEQ)
    od_sample = od_sample.reshape(DEC_BATCH, HGRN_CHUNK, w)[:, :DEC_SEQ].reshape(N_SAMPLE, w)

    o_c = jnp.concatenate([oc_prompt, oc_sample], 0)
    o_d = jnp.concatenate([od_prompt, od_sample], 0)
    x = _mix_out(o_c, o_d, x, w_out_odd[0].astype(BF16), ln_mix_g[1], ln_mix_b[1])
    x = _ffn(x, ffn_w1[1].astype(BF16), ffn_w2[1].astype(BF16), ln_ffn_g[1], ln_ffn_b[1])

    new_pool_prompt = h[:N_PROMPT, :w].reshape(BATCH, SEQ, w)[:, SEQ - POOL_BUF:][None]
    new_pool_sample = jnp.concatenate([state_pool[0][:, DEC_SEQ:], xc_sample], axis=1)[None]
    return (x[:N_PROMPT].reshape(BATCH, SEQ, D_MODEL), x[N_PROMPT:].reshape(DEC_BATCH, DEC_SEQ, D_MODEL),
            new_k_prompt, new_v_prompt, new_k_sample, new_v_sample, new_gmlp_v_sample,
            new_pool_prompt, new_pool_sample, s_prompt[None], s_sample[None])
```

```python
import functools
import math

import jax
import jax.numpy as jnp
from jax import lax
from jax.experimental import pallas as pl
from jax.experimental.pallas import tpu as pltpu

F32 = jnp.float32
BF16 = jnp.bfloat16

D_MODEL = 1024
BATCH = 4
SEQ = 4096
DEPTH = 2
DEC_BATCH = 32
DEC_SEQ = 4
PAST_LEN = 8192
PAGE_SIZE = 128
HALF_W = D_MODEL // 2
A_HD = 64
A_HEADS = HALF_W // A_HD
MOBA_BLOCK = 256
MOBA_TOPK = 3
B_GROUPS = 8
B_GD = HALF_W // B_GROUPS
B_CHUNK = 128
POOL_WINDOWS = (2, 4, 8, 16)
C_GD = HALF_W // len(POOL_WINDOWS)
POOL_BUF = max(POOL_WINDOWS) - 1
D_HK = 128
D_HEADS = HALF_W // D_HK
D_FF = 4 * D_MODEL
EVEN_IN = 5 * HALF_W
ODD_IN = 5 * HALF_W
ALPHA = (2 * DEPTH) ** 0.25
LN_EPS = 1e-5
RMS_EPS = 1e-6
NEG = -1e30

N_PROMPT = BATCH * SEQ
N_SAMPLE = DEC_BATCH * DEC_SEQ
N_PAGES = PAST_LEN // PAGE_SIZE
N_PAST_BLOCKS = PAST_LEN // MOBA_BLOCK
PAGES_PER_BLOCK = MOBA_BLOCK // PAGE_SIZE
N_KEY_BLOCKS = SEQ // MOBA_BLOCK

LANES = 128
SUBLANES = 8
PROMPT_TILE = 512
SAMPLE_TILE = N_SAMPLE
FF_CHUNK = 1024
PAGES_PER_STEP = 8
MOBA_LOOKAHEAD = 7
assert A_HEADS % (MOBA_LOOKAHEAD + 1) == 0
SCORE_SCALE = (A_HD ** -0.5) * math.log2(math.e)
HGRN_CHUNK = 128
HGRN_SUB = 16
HGRN_ROWS = 512
POOL_ROWS = 512
POOL_HALO = 16
MIB = 1024 * 1024


def _dot(a, b):
    return jnp.dot(a, b, preferred_element_type=F32)


def _dot_nt(a, b):
    return lax.dot_general(a, b, (((1,), (1,)), ((), ())), preferred_element_type=F32)


def _split3(x):
    p0 = x.astype(BF16)
    r1 = x - p0.astype(F32)
    p1 = r1.astype(BF16)
    p2 = (r1 - p1.astype(F32)).astype(BF16)
    return p0, p1, p2


def _dot_exact_lhs(m_bf16, x):
    p0, p1, p2 = _split3(x)
    return _dot(m_bf16, p0) + _dot(m_bf16, p1) + _dot(m_bf16, p2)


def _dot_exact_rhs(x, m_bf16):
    p0, p1, p2 = _split3(x)
    return _dot(p0, m_bf16) + _dot(p1, m_bf16) + _dot(p2, m_bf16)


def _gelu_tanh(x):
    return 0.5 * x * (1.0 + jnp.tanh(0.7978845608028654 * (x + 0.044715 * (x * x * x))))


def _sigmoid(x):
    return 1.0 / (1.0 + jnp.exp(-x))


def _layer_norm(z, g, b):
    mu = jnp.mean(z, axis=-1, keepdims=True)
    zc = z - mu
    var = jnp.mean(zc * zc, axis=-1, keepdims=True)
    return zc * lax.rsqrt(var + LN_EPS) * g + b


def _params(sem, vmem_mib):
    return pltpu.CompilerParams(dimension_semantics=sem, vmem_limit_bytes=int(vmem_mib * MIB))


def _resident(shape):
    return pl.BlockSpec(shape, lambda *_: (0,) * len(shape), pipeline_mode=pl.Buffered(1))


def _proj_kernel(x_ref, w_ref, h_ref):
    xb = x_ref[...].astype(BF16)
    for c in range(w_ref.shape[1] // HALF_W):
        sl = slice(c * HALF_W, (c + 1) * HALF_W)
        h_ref[:, sl] = _dot(xb, w_ref[:, sl])


def _proj(x, w_bf16, tile):
    n, d = x.shape
    n_out = w_bf16.shape[1]
    vmem = 2 * (tile * d * 4 + tile * n_out * 4) + d * n_out * 2
    return pl.pallas_call(
        _proj_kernel,
        grid=(n // tile,),
        in_specs=[pl.BlockSpec((tile, d), lambda i: (i, 0)), _resident((d, n_out))],
        out_specs=pl.BlockSpec((tile, n_out), lambda i: (i, 0)),
        out_shape=jax.ShapeDtypeStruct((n, n_out), F32),
        compiler_params=_params(("parallel",), vmem / MIB + 8),
        name="in_proj",
    )(x, w_bf16)


def _proj_even_kernel(x_ref, wn_ref, wt_ref, kb_ref, ugv_ref, qt_ref, vt_ref, kt32_ref, vt32_ref):
    w = HALF_W
    xb = x_ref[...].astype(BF16)
    k = _dot(xb, wn_ref[:, :w])
    for half in range(PROMPT_TILE // MOBA_BLOCK):
        kb_ref[half] = k[half * MOBA_BLOCK:(half + 1) * MOBA_BLOCK, :].astype(BF16)
    ugv_ref[:, :w] = _dot(xb, wn_ref[:, w:2 * w])
    ugv_ref[:, w:] = _dot(xb, wn_ref[:, 2 * w:])
    qt = _dot_nt(wt_ref[:w, :], xb)
    kt32_ref[...] = _dot_nt(wt_ref[w:2 * w, :], xb)
    vt = _dot_nt(wt_ref[2 * w:, :], xb)
    vt32_ref[...] = vt
    for half in range(PROMPT_TILE // MOBA_BLOCK):
        cols = slice(half * MOBA_BLOCK, (half + 1) * MOBA_BLOCK)
        qt_ref[half] = qt[:, cols].astype(BF16)
        vt_ref[half] = vt[:, cols].astype(BF16)


def _proj_even_prompt(x_prompt, w_nat, w_tr):
    w = HALF_W
    tiles = SEQ // PROMPT_TILE
    per_tile = PROMPT_TILE // MOBA_BLOCK
    blocked = lambda shape: pl.BlockSpec((None, per_tile) + shape, lambda b, t: (b, t, 0, 0))
    vmem = (2 * (PROMPT_TILE * D_MODEL * 4 + PROMPT_TILE * 2 * w * 4 + 3 * PROMPT_TILE * w * 2
                 + 2 * PROMPT_TILE * w * 4) + 2 * D_MODEL * 3 * w * 2 + 4 * PROMPT_TILE * w * 4)
    return pl.pallas_call(
        _proj_even_kernel,
        grid=(BATCH, tiles),
        in_specs=[pl.BlockSpec((None, PROMPT_TILE, D_MODEL), lambda b, t: (b, t, 0)),
                  _resident((D_MODEL, 3 * w)), _resident((3 * w, D_MODEL))],
        out_specs=[blocked((MOBA_BLOCK, w)),
                   pl.BlockSpec((PROMPT_TILE, 2 * w), lambda b, t: (b * tiles + t, 0)),
                   blocked((w, MOBA_BLOCK)), blocked((w, MOBA_BLOCK)),
                   pl.BlockSpec((None, w, PROMPT_TILE), lambda b, t: (b, 0, t)),
                   pl.BlockSpec((None, w, PROMPT_TILE), lambda b, t: (b, 0, t))],
        out_shape=[jax.ShapeDtypeStruct((BATCH, N_KEY_BLOCKS, MOBA_BLOCK, w), BF16),
                   jax.ShapeDtypeStruct((N_PROMPT, 2 * w), F32),
                   jax.ShapeDtypeStruct((BATCH, N_KEY_BLOCKS, w, MOBA_BLOCK), BF16),
                   jax.ShapeDtypeStruct((BATCH, N_KEY_BLOCKS, w, MOBA_BLOCK), BF16),
                   jax.ShapeDtypeStruct((BATCH, w, SEQ), F32),
                   jax.ShapeDtypeStruct((BATCH, w, SEQ), F32)],
        compiler_params=_params(("parallel", "parallel"), vmem / MIB + 8),
        name="in_proj_even",
    )(x_prompt, w_nat, w_tr)


def _mix_ffn_kernel(oa_ref, ob_ref, x_ref, wo_ref, g1_ref, b1_ref, w1_ref, w2_ref, g2_ref, b2_ref,
                    y_ref):
    mixed = _dot(oa_ref[...].astype(BF16), wo_ref[:HALF_W, :])
    mixed = mixed + _dot(ob_ref[...].astype(BF16), wo_ref[HALF_W:, :])
    x = _layer_norm(ALPHA * x_ref[...] + mixed, g1_ref[...], b1_ref[...])
    xb = x.astype(BF16)
    acc = jnp.zeros(x.shape, F32)
    for c in range(D_FF // FF_CHUNK):
        sl = slice(c * FF_CHUNK, (c + 1) * FF_CHUNK)
        hid = jnp.maximum(_dot(xb, w1_ref[:, sl]), 0.0)
        acc = acc + _dot((hid * hid).astype(BF16), w2_ref[sl, :])
    y_ref[...] = _layer_norm(ALPHA * x + acc, g2_ref[...], b2_ref[...])


def _mix_ffn(oa, ob, x, wo_bf16, g1, b1, w1_bf16, w2_bf16, g2, b2, tile):
    n, d = x.shape
    row = lambda i: (i, 0)
    vec = lambda v: v.reshape(1, d)
    vmem = (2 * (2 * tile * HALF_W * oa.dtype.itemsize + 2 * tile * d * 4)
            + (d * d + 2 * d * D_FF) * 2 + 4 * tile * FF_CHUNK * 4)
    return pl.pallas_call(
        _mix_ffn_kernel,
        grid=(n // tile,),
        in_specs=[pl.BlockSpec((tile, HALF_W), row), pl.BlockSpec((tile, HALF_W), row),
                  pl.BlockSpec((tile, d), row), _resident((d, d)),
                  _resident((1, d)), _resident((1, d)),
                  _resident((d, D_FF)), _resident((D_FF, d)),
                  _resident((1, d)), _resident((1, d))],
        out_specs=pl.BlockSpec((tile, d), row),
        out_shape=jax.ShapeDtypeStruct((n, d), F32),
        compiler_params=_params(("parallel",), vmem / MIB + 8),
        name="mix_ffn_ln",
    )(oa, ob, x, wo_bf16, vec(g1), vec(b1), w1_bf16, w2_bf16, vec(g2), vec(b2))


def _moba_prompt_kernel(qt_ref, k_ref, vt_ref, o_ref,
                        kmt_scr, qtz_scr, selb_scr, m_scr, l_scr, acc_scr, s_scr, ref_scr, a_scr):
    j = pl.program_id(1)
    nb = k_ref.shape[0]
    blk = MOBA_BLOCK
    pair_w = 2 * A_HD

    @pl.when(j == 0)
    def _():
        rows = [jnp.sum(k_ref[n].astype(F32), axis=0, keepdims=True) * (1.0 / blk)
                for n in range(nb)]
        kmean = jnp.concatenate(rows, axis=0)
        tiled = jnp.concatenate([kmean] * A_HEADS, axis=0)
        rh = lax.broadcasted_iota(jnp.int32, tiled.shape, 0) // nb
        ch = lax.broadcasted_iota(jnp.int32, tiled.shape, 1) // A_HD
        kmt_scr[...] = jnp.where(rh == ch, tiled, 0.0)

    qts = qt_ref[...]
    k0, k1, k2 = _split3(kmt_scr[...])
    gate = _dot(k0, qts) + _dot(k1, qts) + _dot(k2, qts)

    n_io = lax.broadcasted_iota(jnp.int32, (nb, blk), 0)
    half = lax.broadcasted_iota(jnp.int32, (pair_w, blk), 0) // A_HD
    for h in range(A_HEADS):
        g = jnp.where(n_io < j, gate[h * nb:(h + 1) * nb, :], NEG)
        bias = jnp.full((nb, blk), NEG, F32)
        for _ in range(MOBA_TOPK):
            mx = jnp.max(g, axis=0, keepdims=True)
            cand = jnp.where((g == mx) & (mx > 0.5 * NEG), n_io, nb)
            pick = n_io == jnp.min(cand, axis=0, keepdims=True)
            bias = jnp.where(pick, 0.0, bias)
            g = jnp.where(pick, NEG, g)
        selb_scr[h * nb:(h + 1) * nb, :] = bias
        pr = h // 2
        qpair = qts[pr * pair_w:(pr + 1) * pair_w, :]
        qtz_scr[h] = jnp.where(half == (h % 2), qpair, jnp.zeros_like(qpair))

    causal = (lax.broadcasted_iota(jnp.int32, (blk, blk), 0)
              <= lax.broadcasted_iota(jnp.int32, (blk, blk), 1))

    def fold_rows(x, op):
        return op(x.reshape(blk // SUBLANES, SUBLANES, blk), axis=0)

    def stage_scores(n, h, slot, own):
        pr = h // 2
        s = _dot(k_ref[n, :, pr * pair_w:(pr + 1) * pair_w], qtz_scr[h])
        if own:
            s = jnp.where(causal, s, NEG)
        s_scr[slot] = s
        col_max = jnp.max(fold_rows(s, jnp.max), axis=0, keepdims=True)
        if own:
            m_scr[h] = col_max
            ref_scr[slot] = col_max
        else:
            bias = selb_scr[pl.ds(h * nb + n, 1), :]
            m_old = m_scr[h]
            m_new = jnp.maximum(m_old, col_max + bias)
            m_scr[h] = m_new
            a_scr[slot] = jnp.exp2(m_old - m_new)
            ref_scr[slot] = m_new - bias

    def accumulate(n, h, slot, own):
        hs = slice(h * A_HD, (h + 1) * A_HD)
        p = jnp.exp2(s_scr[slot] - ref_scr[slot])
        p_sum = jnp.sum(fold_rows(p, jnp.sum), axis=0, keepdims=True)
        pv = _dot(vt_ref[n, hs, :], p.astype(BF16))
        if own:
            l_scr[h] = p_sum
            acc_scr[hs, :] = pv
        else:
            a = a_scr[slot]
            l_scr[h] = a * l_scr[h] + p_sum
            acc_scr[hs, :] = a * acc_scr[hs, :] + pv

    n_slots = MOBA_LOOKAHEAD + 1

    def stage_ahead(n, n_next, h, own):
        ha = h + MOBA_LOOKAHEAD
        if ha < A_HEADS:
            stage_scores(n, ha, ha % n_slots, own=own)
        else:
            stage_scores(n_next, ha - A_HEADS, ha % n_slots, own=False)

    for h in range(MOBA_LOOKAHEAD):
        stage_scores(j, h, h % n_slots, own=True)
    for h in range(A_HEADS):
        stage_ahead(j, 0, h, own=True)
        accumulate(j, h, h % n_slots, own=True)

    def past_block(n, carry):
        for h in range(A_HEADS):
            stage_ahead(n, n + 1, h, own=False)
            accumulate(n, h, h % n_slots, own=False)
        return carry

    lax.fori_loop(0, j, past_block, 0)

    for h in range(A_HEADS):
        hs = slice(h * A_HD, (h + 1) * A_HD)
        acc_scr[hs, :] = acc_scr[hs, :] / l_scr[h]
    o_ref[...] = acc_scr[...].T.astype(o_ref.dtype)


def _moba_prompt(qt, kb, vt):
    bsz, nb = qt.shape[:2]
    w = A_HEADS * A_HD
    n_slots = MOBA_LOOKAHEAD + 1
    vmem = (2 * (2 * nb * MOBA_BLOCK * w * 2 + w * MOBA_BLOCK * 2 + MOBA_BLOCK * w * 2)
            + n_slots * MOBA_BLOCK * MOBA_BLOCK * 4 + 4 * MIB)
    return pl.pallas_call(
        _moba_prompt_kernel,
        grid=(bsz, nb),
        in_specs=[pl.BlockSpec((None, None, w, MOBA_BLOCK), lambda b, j: (b, j, 0, 0)),
                  pl.BlockSpec((None, nb, MOBA_BLOCK, w), lambda b, j: (b, 0, 0, 0)),
                  pl.BlockSpec((None, nb, w, MOBA_BLOCK), lambda b, j: (b, 0, 0, 0))],
        out_specs=pl.BlockSpec((MOBA_BLOCK, w), lambda b, j: (b * nb + j, 0)),
        out_shape=jax.ShapeDtypeStruct((bsz * nb * MOBA_BLOCK, w), BF16),
        scratch_shapes=[pltpu.VMEM((A_HEADS * nb, w), F32),
                        pltpu.VMEM((A_HEADS, 2 * A_HD, MOBA_BLOCK), BF16),
                        pltpu.VMEM((A_HEADS * nb, MOBA_BLOCK), F32),
                        pltpu.VMEM((A_HEADS, 1, MOBA_BLOCK), F32),
                        pltpu.VMEM((A_HEADS, 1, MOBA_BLOCK), F32),
                        pltpu.VMEM((w, MOBA_BLOCK), F32),
                        pltpu.VMEM((n_slots, MOBA_BLOCK, MOBA_BLOCK), F32),
                        pltpu.VMEM((n_slots, 1, MOBA_BLOCK), F32),
                        pltpu.VMEM((n_slots, 1, MOBA_BLOCK), F32)],
        compiler_params=_params(("parallel", "arbitrary"), vmem / MIB + 8),
        name="moba_prompt",
    )(qt, kb, vt)


def _moba_sample_kernel(pt_ref, q_ref, kn_ref, vn_ref, *refs):
    del pt_ref
    k_refs = refs[:PAGES_PER_STEP]
    v_refs = refs[PAGES_PER_STEP:2 * PAGES_PER_STEP]
    o_ref = refs[2 * PAGES_PER_STEP]
    m_scr, l_scr, gs_scr, gate_scr, acc_scr = refs[2 * PAGES_PER_STEP + 1:]
    g = pl.program_id(1)
    w = A_HEADS * A_HD
    nrow = DEC_SEQ * A_HEADS
    wide = lambda c: jnp.broadcast_to(c, (nrow, LANES))

    head_mask = (lax.broadcasted_iota(jnp.int32, (A_HEADS, w), 1) // A_HD
                 == lax.broadcasted_iota(jnp.int32, (A_HEADS, w), 0))
    q = q_ref[...]
    qbd = jnp.concatenate(
        [jnp.where(head_mask, jnp.broadcast_to(q[i:i + 1, :], (A_HEADS, w)), 0.0)
         for i in range(DEC_SEQ)], axis=0).astype(BF16)

    scores = [_dot(qbd, k_refs[i][...].reshape(w, PAGE_SIZE).astype(BF16))
              for i in range(PAGES_PER_STEP)]
    probs = []
    for i in range(PAGES_PER_STEP):
        page = g * PAGES_PER_STEP + i
        s = scores[i]
        m = jnp.max(s, axis=-1, keepdims=True)
        p = jnp.exp2(s - m)
        m_scr[page] = wide(m)
        l_scr[page] = wide(jnp.sum(p, axis=-1, keepdims=True))
        gs_scr[page] = wide(jnp.sum(s, axis=-1, keepdims=True))
        probs.append(p.astype(BF16))
    for i in range(PAGES_PER_STEP):
        page = g * PAGES_PER_STEP + i
        acc_scr[page] = _dot_nt(probs[i], v_refs[i][...].reshape(w, PAGE_SIZE).astype(BF16))

    @pl.when(g == pl.num_programs(1) - 1)
    def _():
        tile4 = lambda c: jnp.concatenate([c] * (w // LANES), axis=1)
        s = _dot_nt(qbd, kn_ref[...].astype(BF16))
        col = lax.broadcasted_iota(jnp.int32, s.shape, 1)
        qi = lax.broadcasted_iota(jnp.int32, s.shape, 0) // A_HEADS
        s = jnp.where(col <= qi, s, NEG)
        m_col = jnp.max(s, axis=-1, keepdims=True)
        p = jnp.exp2(s - m_col)
        m_own = wide(m_col)
        l_own = wide(jnp.sum(p, axis=-1, keepdims=True))
        o_own = _dot(p, vn_ref[...])

        for n in range(N_PAST_BLOCKS):
            tot = gs_scr[PAGES_PER_BLOCK * n]
            for r in range(1, PAGES_PER_BLOCK):
                tot = tot + gs_scr[PAGES_PER_BLOCK * n + r]
            gate_scr[n] = tot * (1.0 / MOBA_BLOCK)
        for _ in range(MOBA_TOPK):
            best = jnp.full((nrow, LANES), NEG, F32)
            bidx = jnp.zeros((nrow, LANES), jnp.int32)
            for n in range(N_PAST_BLOCKS):
                gn = gate_scr[n]
                upd = gn > best
                best = jnp.where(upd, gn, best)
                bidx = jnp.where(upd, n, bidx)
            for n in range(N_PAST_BLOCKS):
                gate_scr[n] = jnp.where(bidx == n, -jnp.inf, gate_scr[n])

        m_all = m_own
        for pg in range(N_PAGES):
            picked = gate_scr[pg // PAGES_PER_BLOCK] == -jnp.inf
            m_all = jnp.maximum(m_all, jnp.where(picked, m_scr[pg], NEG))
        w_own = jnp.exp2(m_own - m_all)
        l_all = w_own * l_own
        o_all = tile4(w_own) * o_own
        for pg in range(N_PAGES):
            picked = gate_scr[pg // PAGES_PER_BLOCK] == -jnp.inf
            wp = jnp.where(picked, jnp.exp2(m_scr[pg] - m_all), 0.0)
            l_all = l_all + wp * l_scr[pg]
            o_all = o_all + tile4(wp) * acc_scr[pg]
        out = o_all / tile4(l_all)
        for i in range(DEC_SEQ):
            rows = out[i * A_HEADS:(i + 1) * A_HEADS, :]
            o_ref[i:i + 1, :] = jnp.sum(jnp.where(head_mask, rows, 0.0), axis=0, keepdims=True)


def _moba_sample(page_table, q, k_new, v_new, cache_kt, cache_vt):
    w = A_HEADS * A_HD
    nrow = DEC_SEQ * A_HEADS
    page_spec = lambda i: pl.BlockSpec(
        (None, A_HEADS, A_HD, PAGE_SIZE),
        lambda b, g, pt: (pt[b, g * PAGES_PER_STEP + i], 0, 0, 0))
    per_sample = lambda rows: pl.BlockSpec((None, rows, w), lambda b, g, pt: (b, 0, 0))
    grid_spec = pltpu.PrefetchScalarGridSpec(
        num_scalar_prefetch=1,
        grid=(DEC_BATCH, N_PAGES // PAGES_PER_STEP),
        in_specs=([per_sample(DEC_SEQ), per_sample(SUBLANES), per_sample(SUBLANES)]
                  + [page_spec(i) for i in range(PAGES_PER_STEP)]
                  + [page_spec(i) for i in range(PAGES_PER_STEP)]),
        out_specs=per_sample(DEC_SEQ),
        scratch_shapes=[pltpu.VMEM((N_PAGES, nrow, LANES), F32),
                        pltpu.VMEM((N_PAGES, nrow, LANES), F32),
                        pltpu.VMEM((N_PAGES, nrow, LANES), F32),
                        pltpu.VMEM((N_PAST_BLOCKS, nrow, LANES), F32),
                        pltpu.VMEM((N_PAGES, nrow, w), F32)],
    )
    vmem = (2 * 2 * PAGES_PER_STEP * PAGE_SIZE * w * 4
            + 4 * N_PAGES * nrow * LANES * 4 + N_PAGES * nrow * w * 4)
    return pl.pallas_call(
        _moba_sample_kernel,
        grid_spec=grid_spec,
        out_shape=jax.ShapeDtypeStruct((DEC_BATCH, DEC_SEQ, w), F32),
        compiler_params=_params(("parallel", "arbitrary"), vmem / MIB + 8),
        name="moba_sample",
    )(page_table, q, k_new, v_new,
      *([cache_kt] * PAGES_PER_STEP), *([cache_vt] * PAGES_PER_STEP))


def _gmlp_kernel(u_ref, gv_ref, w_ref, bias_ref, avg_ref, lng_ref, lnb_ref, ob_ref, *vn_refs):
    u = _gelu_tanh(u_ref[...])
    gv = _gelu_tanh(gv_ref[...])
    avg = avg_ref[...]
    mu = _dot_exact_rhs(gv, avg)
    gc = gv - mu
    var = _dot_exact_rhs(gc * gc, avg)
    vn = gc * lax.rsqrt(var + LN_EPS) * lng_ref[...] + lnb_ref[...]
    if vn_refs:
        vn_refs[0][...] = vn
    vb = vn.astype(BF16)
    pair_w = 2 * B_GD
    lane = lax.broadcasted_iota(jnp.int32, (B_CHUNK, pair_w), 1)
    for pr in range(B_GROUPS // 2):
        sl = slice(pr * pair_w, (pr + 1) * pair_w)
        vp = vb[:, sl]
        zero = jnp.zeros_like(vp)
        mixed = (_dot(w_ref[2 * pr], jnp.where(lane < B_GD, vp, zero))
                 + _dot(w_ref[2 * pr + 1], jnp.where(lane >= B_GD, vp, zero)))
        ob_ref[:, sl] = (u[:, sl] * (mixed + bias_ref[:, sl])).astype(ob_ref.dtype)


def _gmlp(src, u_col, gv_col, w_masked, bias, avg, ln_g, ln_b, *, out_dtype, emit_vn):
    rows = src.shape[0]
    w = B_GROUPS * B_GD
    out_shape = [jax.ShapeDtypeStruct((rows, w), out_dtype)]
    out_specs = [pl.BlockSpec((B_CHUNK, w), lambda c: (c, 0))]
    if emit_vn:
        out_shape.append(jax.ShapeDtypeStruct((rows, w), F32))
        out_specs.append(pl.BlockSpec((B_CHUNK, w), lambda c: (c, 0)))
    return pl.pallas_call(
        _gmlp_kernel,
        grid=(rows // B_CHUNK,),
        in_specs=[pl.BlockSpec((B_CHUNK, w), lambda c: (c, u_col)),
                  pl.BlockSpec((B_CHUNK, w), lambda c: (c, gv_col)),
                  _resident((B_GROUPS, B_CHUNK, B_CHUNK)), _resident((B_CHUNK, w)),
                  _resident((w, w)), _resident((1, w)), _resident((1, w))],
        out_specs=out_specs,
        out_shape=out_shape,
        compiler_params=_params(("parallel",), 24),
        name="gmlp_gate",
    )(src, src, w_masked, bias, avg, ln_g.reshape(1, w), ln_b.reshape(1, w))


def _pool_kernel(halo_ref, x_ref, w_ref, sc_ref, y_ref, *, rows, pos0, tiles_per_seq, fresh):
    t = pl.program_id(0) % tiles_per_seq
    halo = halo_ref[...]
    if fresh:
        halo = jnp.where(t == 0, 0.0, halo)
    x = x_ref[...]
    ext = jnp.concatenate([halo, x], axis=0)
    pos = pos0 + t * rows + lax.broadcasted_iota(jnp.int32, (rows, C_GD), 0)
    for gi, win in enumerate(POOL_WINDOWS):
        sl = slice(gi * C_GD, (gi + 1) * C_GD)
        s = ext[:, sl]
        sh = 1
        while sh < win:
            s = s + pltpu.roll(s, sh, 0)
            sh *= 2
        cnt = jnp.minimum(win, pos + 1).astype(F32)
        pooled = s[POOL_HALO:, :] / cnt - x[:, sl]
        y_ref[:, sl] = (_dot(pooled.astype(BF16), w_ref[gi]) * sc_ref[:, sl]).astype(y_ref.dtype)


def _pool(halo_src, x_src, w_bf16, scale, *, n_rows, rows, pos0, tiles_per_seq, fresh, out_dtype):
    w = len(POOL_WINDOWS) * C_GD
    if fresh:
        step = rows // POOL_HALO
        halo_map = lambda i: (jnp.maximum(i * step - 1, 0), 0)
    else:
        halo_map = lambda i: (i, 0)
    return pl.pallas_call(
        functools.partial(_pool_kernel, rows=rows, pos0=pos0, tiles_per_seq=tiles_per_seq,
                          fresh=fresh),
        grid=(n_rows // rows,),
        in_specs=[pl.BlockSpec((POOL_HALO, w), halo_map),
                  pl.BlockSpec((rows, w), lambda i: (i, 0)),
                  _resident((len(POOL_WINDOWS), C_GD, C_GD)), _resident((1, w))],
        out_specs=pl.BlockSpec((rows, w), lambda i: (i, 0)),
        out_shape=jax.ShapeDtypeStruct((n_rows, w), out_dtype),
        compiler_params=_params(("arbitrary",), 24),
        name="pool_mix",
    )(halo_src, x_src, w_bf16, scale.reshape(1, w))


def _hgrn_kernel(q_ref, f_ref, i_ref, g_ref, s0_ref, lb_ref, ng_ref, o_ref, sfin_ref, s_scr,
                 *, rows, in_rows, valid):
    t = pl.program_id(1)
    c_rows = HGRN_CHUNK

    @pl.when(t == 0)
    def _():
        s_scr[...] = s0_ref[...]

    r_io = lax.broadcasted_iota(jnp.int32, (c_rows, c_rows), 0)
    c_io = lax.broadcasted_iota(jnp.int32, (c_rows, c_rows), 1)
    causal = r_io >= c_io
    ltri = jnp.where(causal, 1.0, 0.0).astype(BF16)
    lgrp = jnp.where(c_io < (r_io // HGRN_SUB) * HGRN_SUB, 1.0, 0.0).astype(BF16)
    eye = r_io == c_io
    row_id = lax.broadcasted_iota(jnp.int32, (c_rows, D_HK), 0)

    def chunk(c, carry):
        r0 = pl.multiple_of(c * c_rows, c_rows)

        def load(ref, sl):
            if in_rows == rows:
                return ref[pl.ds(r0, c_rows), sl]
            return jnp.concatenate([ref[:, sl], jnp.zeros((c_rows - in_rows, D_HK), F32)], axis=0)

        for hd in range(D_HEADS):
            sl = slice(hd * D_HK, (hd + 1) * D_HK)
            lb = lb_ref[:, sl]
            f = lb + (1.0 - lb) * _sigmoid(load(f_ref, sl))
            logf = jnp.log(f)
            kk = 1.0 - f
            qr = load(q_ref, sl)
            q = qr * _sigmoid(qr)
            if valid < rows:
                live = (t * rows + r0 + row_id) < valid
                logf = jnp.where(live, logf, 0.0)
                kk = jnp.where(live, kk, 0.0)
            cg = _dot_exact_lhs(ltri, logf)
            ref = _dot_exact_lhs(lgrp, logf)
            vb = load(i_ref, sl).astype(BF16)
            state = s_scr[hd]
            o = _dot((q * jnp.exp(cg)).astype(BF16), state.astype(BF16))
            qd = (q * jnp.exp(cg - ref)).astype(BF16)
            blocks = []
            for i in range(c_rows // HGRN_SUB):
                ref_i = ref[i * HGRN_SUB:i * HGRN_SUB + 1, :]
                e = jnp.where(row_id < (i + 1) * HGRN_SUB, ref_i - cg, 0.0)
                k_i = (kk * jnp.exp(e)).astype(BF16)
                blocks.append(_dot_nt(qd[i * HGRN_SUB:(i + 1) * HGRN_SUB, :], k_i))
            attn = jnp.where(causal, jnp.concatenate(blocks, axis=0), 0.0)
            o = o + _dot(attn.astype(BF16), vb)
            g_last = cg[c_rows - 1:c_rows, :]
            kd = kk * jnp.exp(g_last - cg)
            decay_col = jnp.sum(
                jnp.where(eye, jnp.broadcast_to(jnp.exp(g_last), (c_rows, D_HK)), 0.0),
                axis=1, keepdims=True)
            s_scr[hd] = state * decay_col + _dot(kd.T.astype(BF16), vb)
            o = o * lax.rsqrt(jnp.mean(o * o, axis=-1, keepdims=True) + RMS_EPS) * ng_ref[...]
            gr = load(g_ref, sl)
            o = (o * (gr * _sigmoid(gr))).astype(o_ref.dtype)
            if in_rows == rows:
                o_ref[pl.ds(r0, c_rows), sl] = o
            else:
                o_ref[:, sl] = o[:in_rows, :]
        return carry

    lax.fori_loop(0, rows // c_rows, chunk, 0)

    @pl.when(t == pl.num_programs(1) - 1)
    def _():
        sfin_ref[...] = s_scr[...]


def _hgrn(src, s0, lb, norm_g, *, bsz, length, rows, in_rows, valid, out_dtype):
    w = D_HEADS * D_HK
    tiles = length // rows
    col = lambda k: pl.BlockSpec((in_rows, w), lambda b, t: (b * tiles + t, 1 + k))
    state_spec = pl.BlockSpec((None, D_HEADS, D_HK, D_HK), lambda b, t: (b, 0, 0, 0))
    return pl.pallas_call(
        functools.partial(_hgrn_kernel, rows=rows, in_rows=in_rows, valid=valid),
        grid=(bsz, tiles),
        in_specs=[col(0), col(1), col(2), col(3), state_spec, _resident((1, w)),
                  _resident((1, D_HK))],
        out_specs=[pl.BlockSpec((in_rows, w), lambda b, t: (b * tiles + t, 0)), state_spec],
        out_shape=[jax.ShapeDtypeStruct((bsz * tiles * in_rows, w), out_dtype),
                   jax.ShapeDtypeStruct((bsz, D_HEADS, D_HK, D_HK), F32)],
        scratch_shapes=[pltpu.VMEM((D_HEADS, D_HK, D_HK), F32)],
        compiler_params=_params(("parallel", "arbitrary"), 32),
        name="hgrn2",
    )(src, src, src, src, s0, lb.reshape(1, w), norm_g.reshape(1, D_HK))


def kernel(x_prompt, x_sample, cache_k, cache_v, state_pool, state_hgrn, page_table, w_in_even, w_out_even, gmlp_ws, gmlp_bs, gmlp_ln_g, gmlp_ln_b, w_in_odd, w_out_odd, pool_w, pool_scale, hgrn_lb_param, hgrn_norm_g, ln_mix_g, ln_mix_b, ln_ffn_g, ln_ffn_b, ffn_w1, ffn_w2):
    w = HALF_W
    xp = x_prompt.reshape(N_PROMPT, D_MODEL)
    xs = x_sample.reshape(N_SAMPLE, D_MODEL)
    pad_tokens = ((0, 0), (0, SUBLANES - DEC_SEQ), (0, 0))

    col_scale = jnp.where(jnp.arange(EVEN_IN) < w, SCORE_SCALE, 1.0)
    w_even = (w_in_even[0] * col_scale).astype(BF16)
    w_k_u_gv = jnp.concatenate([w_even[:, w:2 * w], w_even[:, 3 * w:]], axis=1)
    kb, ugv, qt, vt, kt32, vt32 = _proj_even_prompt(x_prompt, w_k_u_gv, w_even[:, :3 * w].T)
    oa_prompt = _moba_prompt(qt, kb, vt)

    h0s = _proj(xs, w_even, SAMPLE_TILE)
    hs = h0s.reshape(DEC_BATCH, DEC_SEQ, EVEN_IN)
    oa_sample = _moba_sample(page_table, hs[..., :w],
                             jnp.pad(hs[..., w:2 * w], pad_tokens), jnp.pad(hs[..., 2 * w:3 * w], pad_tokens),
                             cache_k[0].transpose(0, 2, 3, 1), cache_v[0].transpose(0, 2, 3, 1))

    tril = jnp.tril(jnp.ones((B_CHUNK, B_CHUNK), bool))
    ws_prompt = jnp.where(tril[None], gmlp_ws[0], 0.0)
    ws_sample = jax.vmap(lambda m: jnp.kron(jnp.eye(DEC_BATCH, dtype=F32), m[:DEC_SEQ, :DEC_SEQ]))(ws_prompt)
    bias_prompt = jnp.repeat(gmlp_bs[0].T, B_GD, axis=1)
    bias_sample = jnp.tile(bias_prompt[:DEC_SEQ], (DEC_BATCH, 1))
    grp = jnp.arange(w) // B_GD
    avg = jnp.where(grp[:, None] == grp[None, :], 1.0 / B_GD, 0.0).astype(BF16)
    ob_prompt, = _gmlp(ugv, 0, 1, ws_prompt.astype(BF16), bias_prompt, avg, gmlp_ln_g[0], gmlp_ln_b[0],
                       out_dtype=BF16, emit_vn=False)
    ob_sample, vn_sample = _gmlp(h0s, 3, 4, ws_sample.astype(BF16), bias_sample, avg, gmlp_ln_g[0],
                                 gmlp_ln_b[0], out_dtype=F32, emit_vn=True)

    layer0 = (w_out_even[0].astype(BF16), ln_mix_g[0], ln_mix_b[0],
              ffn_w1[0].astype(BF16), ffn_w2[0].astype(BF16), ln_ffn_g[0], ln_ffn_b[0])
    xp = _mix_ffn(oa_prompt, ob_prompt, xp, *layer0, PROMPT_TILE)
    xs = _mix_ffn(oa_sample.reshape(N_SAMPLE, w), ob_sample, xs, *layer0, SAMPLE_TILE)

    per_head = lambda t: t.reshape(BATCH, A_HEADS, A_HD, SEQ).transpose(0, 3, 1, 2)[None]
    new_k_prompt = per_head(kt32)
    new_v_prompt = per_head(vt32)
    new_k_sample = hs[..., w:2 * w].reshape(1, DEC_BATCH, DEC_SEQ, A_HEADS, A_HD)
    new_v_sample = hs[..., 2 * w:3 * w].reshape(1, DEC_BATCH, DEC_SEQ, A_HEADS, A_HD)
    new_gmlp_v_sample = vn_sample.reshape(1, DEC_BATCH, DEC_SEQ, w)

    w_odd = w_in_odd[0].astype(BF16)
    lb = jax.nn.softmax(hgrn_lb_param.astype(F32), axis=0)[0]
    pool_wb = pool_w[0].astype(BF16)

    h1p = _proj(xp, w_odd, PROMPT_TILE)
    oc_prompt = _pool(h1p, h1p, pool_wb, pool_scale[0], n_rows=N_PROMPT, rows=POOL_ROWS, pos0=0,
                      tiles_per_seq=SEQ // POOL_ROWS, fresh=True, out_dtype=BF16)
    od_prompt, s_prompt = _hgrn(h1p, jnp.zeros((BATCH, D_HEADS, D_HK, D_HK), F32), lb, hgrn_norm_g[0],
                                bsz=BATCH, length=SEQ, rows=HGRN_ROWS, in_rows=HGRN_ROWS, valid=SEQ,
                                out_dtype=BF16)

    h1s = _proj(xs, w_odd, SAMPLE_TILE).reshape(DEC_BATCH, DEC_SEQ, ODD_IN)
    h1s_pad = jnp.pad(h1s, pad_tokens).reshape(DEC_BATCH * SUBLANES, ODD_IN)
    halo_s = jnp.pad(state_pool[0], ((0, 0), (POOL_HALO - POOL_BUF, 0), (0, 0)))
    oc_sample = _pool(halo_s.reshape(DEC_BATCH * POOL_HALO, w), h1s_pad, pool_wb, pool_scale[0],
                      n_rows=DEC_BATCH * SUBLANES, rows=SUBLANES, pos0=PAST_LEN, tiles_per_seq=1,
                      fresh=False, out_dtype=F32)
    od_sample, s_sample = _hgrn(h1s_pad, state_hgrn[0], lb, hgrn_norm_g[0], bsz=DEC_BATCH,
                                length=HGRN_CHUNK, rows=HGRN_CHUNK, in_rows=SUBLANES, valid=DEC_SEQ,
                                out_dtype=F32)
    real_rows = lambda t: t.reshape(DEC_BATCH, SUBLANES, w)[:, :DEC_SEQ].reshape(N_SAMPLE, w)

    layer1 = (w_out_odd[0].astype(BF16), ln_mix_g[1], ln_mix_b[1],
              ffn_w1[1].astype(BF16), ffn_w2[1].astype(BF16), ln_ffn_g[1], ln_ffn_b[1])
    xp = _mix_ffn(oc_prompt, od_prompt, xp, *layer1, PROMPT_TILE)
    xs = _mix_ffn(real_rows(oc_sample), real_rows(od_sample), xs, *layer1, SAMPLE_TILE)

    new_pool_prompt = h1p.reshape(BATCH, SEQ, ODD_IN)[:, SEQ - POOL_BUF:, :w][None]
    new_pool_sample = jnp.concatenate([state_pool[0][:, DEC_SEQ:], h1s[..., :w]], axis=1)[None]
    return (xp.reshape(BATCH, SEQ, D_MODEL), xs.reshape(DEC_BATCH, DEC_SEQ, D_MODEL),
            new_k_prompt, new_v_prompt, new_k_sample, new_v_sample, new_gmlp_v_sample,
            new_pool_prompt, new_pool_sample, s_prompt[None], s_sample[None])
```

```python
import functools
import math

import jax
import jax.numpy as jnp
from jax import lax
from jax.experimental import pallas as pl
from jax.experimental.pallas import tpu as pltpu

F32 = jnp.float32
BF16 = jnp.bfloat16

D_MODEL = 1024
BATCH = 4
SEQ = 4096
DEPTH = 2
DEC_BATCH = 32
DEC_SEQ = 4
PAST_LEN = 8192
PAGE_SIZE = 128
HALF_W = D_MODEL // 2
A_HD = 64
A_HEADS = HALF_W // A_HD
MOBA_BLOCK = 256
MOBA_TOPK = 3
B_GROUPS = 8
B_GD = HALF_W // B_GROUPS
B_CHUNK = 128
POOL_WINDOWS = (2, 4, 8, 16)
C_GD = HALF_W // len(POOL_WINDOWS)
POOL_BUF = max(POOL_WINDOWS) - 1
D_HK = 128
D_HEADS = HALF_W // D_HK
D_FF = 4 * D_MODEL
EVEN_IN = 5 * HALF_W
ODD_IN = 5 * HALF_W
ALPHA = (2 * DEPTH) ** 0.25
LN_EPS = 1e-5
RMS_EPS = 1e-6
NEG = -1e30

N_PROMPT = BATCH * SEQ
N_SAMPLE = DEC_BATCH * DEC_SEQ
N_PAGES = PAST_LEN // PAGE_SIZE
N_PAST_BLOCKS = PAST_LEN // MOBA_BLOCK
PAGES_PER_BLOCK = MOBA_BLOCK // PAGE_SIZE
N_KEY_BLOCKS = SEQ // MOBA_BLOCK

LANES = 128
SUBLANES = 8
PROMPT_TILE = 512
SAMPLE_TILE = N_SAMPLE
FF_CHUNK = 1024
PAGES_PER_STEP = 16
MOBA_LOOKAHEAD = 7
assert A_HEADS % (MOBA_LOOKAHEAD + 1) == 0
SCORE_SCALE = (A_HD ** -0.5) * math.log2(math.e)
HGRN_CHUNK = 128
HGRN_SUB = 32
HGRN_ROWS = 512
HGRN_GROUP = 2
GMLP_GROUP = 4
POOL_ROWS = 512
POOL_HALO = 16
MIB = 1024 * 1024


def _dot(a, b):
    return jnp.dot(a, b, preferred_element_type=F32)


def _dot_nt(a, b):
    return lax.dot_general(a, b, (((1,), (1,)), ((), ())), preferred_element_type=F32)


def _split3(x):
    p0 = x.astype(BF16)
    r1 = x - p0.astype(F32)
    p1 = r1.astype(BF16)
    p2 = (r1 - p1.astype(F32)).astype(BF16)
    return p0, p1, p2


def _dot_exact_lhs(m_bf16, x):
    p0, p1, p2 = _split3(x)
    return _dot(m_bf16, p0) + _dot(m_bf16, p1) + _dot(m_bf16, p2)


def _dot_exact_rhs(x, m_bf16, pieces=3):
    parts = _split3(x)[:pieces]
    out = _dot(parts[0], m_bf16)
    for part in parts[1:]:
        out = out + _dot(part, m_bf16)
    return out


def _gelu_tanh(x):
    return 0.5 * x * (1.0 + jnp.tanh(0.7978845608028654 * (x + 0.044715 * (x * x * x))))


def _silu(x):
    h = 0.5 * x
    return h + h * jnp.tanh(h)


def _layer_norm(z, g, b):
    mu = jnp.mean(z, axis=-1, keepdims=True)
    zc = z - mu
    var = jnp.mean(zc * zc, axis=-1, keepdims=True)
    return zc * lax.rsqrt(var + LN_EPS) * g + b


def _params(sem, vmem_mib):
    return pltpu.CompilerParams(dimension_semantics=sem, vmem_limit_bytes=int(vmem_mib * MIB))


def _resident(shape):
    return pl.BlockSpec(shape, lambda *_: (0,) * len(shape), pipeline_mode=pl.Buffered(1))


def _proj_kernel(x_ref, w_ref, h_ref):
    xb = x_ref[...].astype(BF16)
    for c in range(w_ref.shape[1] // HALF_W):
        sl = slice(c * HALF_W, (c + 1) * HALF_W)
        h_ref[:, sl] = _dot(xb, w_ref[:, sl])


def _proj(x, w_bf16, tile):
    n, d = x.shape
    n_out = w_bf16.shape[1]
    vmem = 2 * (tile * d * 4 + tile * n_out * 4) + d * n_out * 2
    return pl.pallas_call(
        _proj_kernel,
        grid=(n // tile,),
        in_specs=[pl.BlockSpec((tile, d), lambda i: (i, 0)), _resident((d, n_out))],
        out_specs=pl.BlockSpec((tile, n_out), lambda i: (i, 0)),
        out_shape=jax.ShapeDtypeStruct((n, n_out), F32),
        compiler_params=_params(("parallel",), vmem / MIB + 8),
        name="in_proj",
    )(x, w_bf16)


def _proj_even_kernel(x_ref, wn_ref, wt_ref, kb_ref, ugv_ref, qt_ref, vt_ref, kt32_ref, vt32_ref):
    w = HALF_W
    xb = x_ref[...].astype(BF16)
    k = _dot(xb, wn_ref[:, :w])
    for half in range(PROMPT_TILE // MOBA_BLOCK):
        kb_ref[half] = k[half * MOBA_BLOCK:(half + 1) * MOBA_BLOCK, :].astype(BF16)
    ugv_ref[:, :w] = _dot(xb, wn_ref[:, w:2 * w])
    ugv_ref[:, w:] = _dot(xb, wn_ref[:, 2 * w:])
    qt = _dot_nt(wt_ref[:w, :], xb)
    kt32_ref[...] = _dot_nt(wt_ref[w:2 * w, :], xb)
    vt = _dot_nt(wt_ref[2 * w:, :], xb)
    vt32_ref[...] = vt
    for half in range(PROMPT_TILE // MOBA_BLOCK):
        cols = slice(half * MOBA_BLOCK, (half + 1) * MOBA_BLOCK)
        qt_ref[half] = qt[:, cols].astype(BF16)
        vt_ref[half] = vt[:, cols].astype(BF16)


def _proj_even_prompt(x_prompt, w_nat, w_tr):
    w = HALF_W
    tiles = SEQ // PROMPT_TILE
    per_tile = PROMPT_TILE // MOBA_BLOCK
    blocked = lambda shape: pl.BlockSpec((None, per_tile) + shape, lambda b, t: (b, t, 0, 0))
    vmem = (2 * (PROMPT_TILE * D_MODEL * 4 + PROMPT_TILE * 2 * w * 4 + 3 * PROMPT_TILE * w * 2
                 + 2 * PROMPT_TILE * w * 4) + 2 * D_MODEL * 3 * w * 2 + 4 * PROMPT_TILE * w * 4)
    return pl.pallas_call(
        _proj_even_kernel,
        grid=(BATCH, tiles),
        in_specs=[pl.BlockSpec((None, PROMPT_TILE, D_MODEL), lambda b, t: (b, t, 0)),
                  _resident((D_MODEL, 3 * w)), _resident((3 * w, D_MODEL))],
        out_specs=[blocked((MOBA_BLOCK, w)),
                   pl.BlockSpec((PROMPT_TILE, 2 * w), lambda b, t: (b * tiles + t, 0)),
                   blocked((w, MOBA_BLOCK)), blocked((w, MOBA_BLOCK)),
                   pl.BlockSpec((None, w, PROMPT_TILE), lambda b, t: (b, 0, t)),
                   pl.BlockSpec((None, w, PROMPT_TILE), lambda b, t: (b, 0, t))],
        out_shape=[jax.ShapeDtypeStruct((BATCH, N_KEY_BLOCKS, MOBA_BLOCK, w), BF16),
                   jax.ShapeDtypeStruct((N_PROMPT, 2 * w), F32),
                   jax.ShapeDtypeStruct((BATCH, N_KEY_BLOCKS, w, MOBA_BLOCK), BF16),
                   jax.ShapeDtypeStruct((BATCH, N_KEY_BLOCKS, w, MOBA_BLOCK), BF16),
                   jax.ShapeDtypeStruct((BATCH, w, SEQ), F32),
                   jax.ShapeDtypeStruct((BATCH, w, SEQ), F32)],
        compiler_params=_params(("parallel", "parallel"), vmem / MIB + 8),
        name="in_proj_even",
    )(x_prompt, w_nat, w_tr)


def _mix_ffn_kernel(oa_ref, ob_ref, x_ref, wo_ref, g1_ref, b1_ref, w1_ref, w2_ref, g2_ref, b2_ref,
                    y_ref):
    mixed = _dot(oa_ref[...].astype(BF16), wo_ref[:HALF_W, :])
    mixed = mixed + _dot(ob_ref[...].astype(BF16), wo_ref[HALF_W:, :])
    x = _layer_norm(ALPHA * x_ref[...] + mixed, g1_ref[...], b1_ref[...])
    xb = x.astype(BF16)
    acc = jnp.zeros(x.shape, F32)
    for c in range(D_FF // FF_CHUNK):
        sl = slice(c * FF_CHUNK, (c + 1) * FF_CHUNK)
        hid = jnp.maximum(_dot(xb, w1_ref[:, sl]), 0.0)
        acc = acc + _dot((hid * hid).astype(BF16), w2_ref[sl, :])
    y_ref[...] = _layer_norm(ALPHA * x + acc, g2_ref[...], b2_ref[...])


def _mix_ffn(oa, ob, x, wo_bf16, g1, b1, w1_bf16, w2_bf16, g2, b2, tile):
    n, d = x.shape
    row = lambda i: (i, 0)
    vec = lambda v: v.reshape(1, d)
    vmem = (2 * (2 * tile * HALF_W * oa.dtype.itemsize + 2 * tile * d * 4)
            + (d * d + 2 * d * D_FF) * 2 + 4 * tile * FF_CHUNK * 4)
    return pl.pallas_call(
        _mix_ffn_kernel,
        grid=(n // tile,),
        in_specs=[pl.BlockSpec((tile, HALF_W), row), pl.BlockSpec((tile, HALF_W), row),
                  pl.BlockSpec((tile, d), row), _resident((d, d)),
                  _resident((1, d)), _resident((1, d)),
                  _resident((d, D_FF)), _resident((D_FF, d)),
                  _resident((1, d)), _resident((1, d))],
        out_specs=pl.BlockSpec((tile, d), row),
        out_shape=jax.ShapeDtypeStruct((n, d), F32),
        compiler_params=_params(("parallel",), vmem / MIB + 8),
        name="mix_ffn_ln",
    )(oa, ob, x, wo_bf16, vec(g1), vec(b1), w1_bf16, w2_bf16, vec(g2), vec(b2))


def _moba_prompt_kernel(qt_ref, k_ref, vt_ref, o_ref,
                        kmt_scr, qtz_scr, selb_scr, m_scr, l_scr, acc_scr, s_scr, ref_scr, a_scr):
    j = pl.program_id(1)
    nb = k_ref.shape[0]
    blk = MOBA_BLOCK
    pair_w = 2 * A_HD

    @pl.when(j == 0)
    def _():
        rows = [jnp.sum(k_ref[n].astype(F32), axis=0, keepdims=True) * (1.0 / blk)
                for n in range(nb)]
        kmean = jnp.concatenate(rows, axis=0)
        tiled = jnp.concatenate([kmean] * A_HEADS, axis=0)
        rh = lax.broadcasted_iota(jnp.int32, tiled.shape, 0) // nb
        ch = lax.broadcasted_iota(jnp.int32, tiled.shape, 1) // A_HD
        kmt_scr[...] = jnp.where(rh == ch, tiled, 0.0)

    qts = qt_ref[...]
    k0, k1, k2 = _split3(kmt_scr[...])
    gate = _dot(k0, qts) + _dot(k1, qts) + _dot(k2, qts)

    n_io = lax.broadcasted_iota(jnp.int32, (nb, blk), 0)
    half = lax.broadcasted_iota(jnp.int32, (pair_w, blk), 0) // A_HD
    for h in range(A_HEADS):
        g = jnp.where(n_io < j, gate[h * nb:(h + 1) * nb, :], NEG)
        bias = jnp.full((nb, blk), NEG, F32)
        for _ in range(MOBA_TOPK):
            mx = jnp.max(g, axis=0, keepdims=True)
            cand = jnp.where((g == mx) & (mx > 0.5 * NEG), n_io, nb)
            pick = n_io == jnp.min(cand, axis=0, keepdims=True)
            bias = jnp.where(pick, 0.0, bias)
            g = jnp.where(pick, NEG, g)
        selb_scr[h * nb:(h + 1) * nb, :] = bias
        pr = h // 2
        qpair = qts[pr * pair_w:(pr + 1) * pair_w, :]
        qtz_scr[h] = jnp.where(half == (h % 2), qpair, jnp.zeros_like(qpair))

    causal = (lax.broadcasted_iota(jnp.int32, (blk, blk), 0)
              <= lax.broadcasted_iota(jnp.int32, (blk, blk), 1))

    def fold_rows(x, op):
        return op(x.reshape(blk // SUBLANES, SUBLANES, blk), axis=0)

    def stage_scores(n, h, slot, own):
        pr = h // 2
        s = _dot(k_ref[n, :, pr * pair_w:(pr + 1) * pair_w], qtz_scr[h])
        if own:
            s = jnp.where(causal, s, NEG)
        s_scr[slot] = s
        col_max = jnp.max(fold_rows(s, jnp.max), axis=0, keepdims=True)
        if own:
            m_scr[h] = col_max
            ref_scr[slot] = col_max
        else:
            bias = selb_scr[pl.ds(h * nb + n, 1), :]
            m_old = m_scr[h]
            m_new = jnp.maximum(m_old, col_max + bias)
            m_scr[h] = m_new
            a_scr[slot] = jnp.exp2(m_old - m_new)
            ref_scr[slot] = m_new - bias

    def accumulate(n, h, slot, own):
        hs = slice(h * A_HD, (h + 1) * A_HD)
        p = jnp.exp2(s_scr[slot] - ref_scr[slot])
        p_sum = jnp.sum(fold_rows(p, jnp.sum), axis=0, keepdims=True)
        pv = _dot(vt_ref[n, hs, :], p.astype(BF16))
        if own:
            l_scr[h] = p_sum
            acc_scr[hs, :] = pv
        else:
            a = a_scr[slot]
            l_scr[h] = a * l_scr[h] + p_sum
            acc_scr[hs, :] = a * acc_scr[hs, :] + pv

    n_slots = MOBA_LOOKAHEAD + 1

    def stage_ahead(n, n_next, h, own):
        ha = h + MOBA_LOOKAHEAD
        if ha < A_HEADS:
            stage_scores(n, ha, ha % n_slots, own=own)
        else:
            stage_scores(n_next, ha - A_HEADS, ha % n_slots, own=False)

    for h in range(MOBA_LOOKAHEAD):
        stage_scores(j, h, h % n_slots, own=True)
    for h in range(A_HEADS):
        stage_ahead(j, 0, h, own=True)
        accumulate(j, h, h % n_slots, own=True)

    def past_block(n, carry):
        for h in range(A_HEADS):
            stage_ahead(n, n + 1, h, own=False)
            accumulate(n, h, h % n_slots, own=False)
        return carry

    lax.fori_loop(0, j, past_block, 0)

    for h in range(A_HEADS):
        hs = slice(h * A_HD, (h + 1) * A_HD)
        acc_scr[hs, :] = acc_scr[hs, :] / l_scr[h]
    o_ref[...] = acc_scr[...].T.astype(o_ref.dtype)


def _moba_prompt(qt, kb, vt):
    bsz, nb = qt.shape[:2]
    w = A_HEADS * A_HD
    n_slots = MOBA_LOOKAHEAD + 1
    vmem = (2 * (2 * nb * MOBA_BLOCK * w * 2 + w * MOBA_BLOCK * 2 + MOBA_BLOCK * w * 2)
            + n_slots * MOBA_BLOCK * MOBA_BLOCK * 4 + 4 * MIB)
    return pl.pallas_call(
        _moba_prompt_kernel,
        grid=(bsz, nb),
        in_specs=[pl.BlockSpec((None, None, w, MOBA_BLOCK), lambda b, j: (b, j, 0, 0)),
                  pl.BlockSpec((None, nb, MOBA_BLOCK, w), lambda b, j: (b, 0, 0, 0)),
                  pl.BlockSpec((None, nb, w, MOBA_BLOCK), lambda b, j: (b, 0, 0, 0))],
        out_specs=pl.BlockSpec((MOBA_BLOCK, w), lambda b, j: (b * nb + j, 0)),
        out_shape=jax.ShapeDtypeStruct((bsz * nb * MOBA_BLOCK, w), BF16),
        scratch_shapes=[pltpu.VMEM((A_HEADS * nb, w), F32),
                        pltpu.VMEM((A_HEADS, 2 * A_HD, MOBA_BLOCK), BF16),
                        pltpu.VMEM((A_HEADS * nb, MOBA_BLOCK), F32),
                        pltpu.VMEM((A_HEADS, 1, MOBA_BLOCK), F32),
                        pltpu.VMEM((A_HEADS, 1, MOBA_BLOCK), F32),
                        pltpu.VMEM((w, MOBA_BLOCK), F32),
                        pltpu.VMEM((n_slots, MOBA_BLOCK, MOBA_BLOCK), F32),
                        pltpu.VMEM((n_slots, 1, MOBA_BLOCK), F32),
                        pltpu.VMEM((n_slots, 1, MOBA_BLOCK), F32)],
        compiler_params=_params(("parallel", "arbitrary"), vmem / MIB + 8),
        name="moba_prompt",
    )(qt, kb, vt)


def _moba_sample_kernel(pt_ref, q_ref, kn_ref, vn_ref, *refs):
    del pt_ref
    k_refs = refs[:PAGES_PER_STEP]
    v_refs = refs[PAGES_PER_STEP:2 * PAGES_PER_STEP]
    o_ref = refs[2 * PAGES_PER_STEP]
    m_scr, l_scr, gs_scr, gate_scr, acc_scr = refs[2 * PAGES_PER_STEP + 1:]
    g = pl.program_id(1)
    w = A_HEADS * A_HD
    nrow = DEC_SEQ * A_HEADS
    wide = lambda c: jnp.broadcast_to(c, (nrow, LANES))

    head_mask = (lax.broadcasted_iota(jnp.int32, (A_HEADS, w), 1) // A_HD
                 == lax.broadcasted_iota(jnp.int32, (A_HEADS, w), 0))
    q = q_ref[...]
    qbd = jnp.concatenate(
        [jnp.where(head_mask, jnp.broadcast_to(q[i:i + 1, :], (A_HEADS, w)), 0.0)
         for i in range(DEC_SEQ)], axis=0).astype(BF16)

    scores = [_dot(qbd, k_refs[i][...].reshape(w, PAGE_SIZE).astype(BF16))
              for i in range(PAGES_PER_STEP)]
    probs = []
    for i in range(PAGES_PER_STEP):
        page = g * PAGES_PER_STEP + i
        s = scores[i]
        m = jnp.max(s, axis=-1, keepdims=True)
        p = jnp.exp2(s - m)
        m_scr[page] = wide(m)
        l_scr[page] = wide(jnp.sum(p, axis=-1, keepdims=True))
        gs_scr[page] = wide(jnp.sum(s, axis=-1, keepdims=True))
        probs.append(p.astype(BF16))
    for i in range(PAGES_PER_STEP):
        page = g * PAGES_PER_STEP + i
        acc_scr[page] = _dot_nt(probs[i], v_refs[i][...].reshape(w, PAGE_SIZE).astype(BF16))

    @pl.when(g == pl.num_programs(1) - 1)
    def _():
        tile4 = lambda c: jnp.concatenate([c] * (w // LANES), axis=1)
        s = _dot_nt(qbd, kn_ref[...].astype(BF16))
        col = lax.broadcasted_iota(jnp.int32, s.shape, 1)
        qi = lax.broadcasted_iota(jnp.int32, s.shape, 0) // A_HEADS
        s = jnp.where(col <= qi, s, NEG)
        m_col = jnp.max(s, axis=-1, keepdims=True)
        p = jnp.exp2(s - m_col)
        m_own = wide(m_col)
        l_own = wide(jnp.sum(p, axis=-1, keepdims=True))
        o_own = _dot(p, vn_ref[...])

        for n in range(N_PAST_BLOCKS):
            tot = gs_scr[PAGES_PER_BLOCK * n]
            for r in range(1, PAGES_PER_BLOCK):
                tot = tot + gs_scr[PAGES_PER_BLOCK * n + r]
            gate_scr[n] = tot * (1.0 / MOBA_BLOCK)
        for _ in range(MOBA_TOPK):
            best = jnp.full((nrow, LANES), NEG, F32)
            bidx = jnp.zeros((nrow, LANES), jnp.int32)
            for n in range(N_PAST_BLOCKS):
                gn = gate_scr[n]
                upd = gn > best
                best = jnp.where(upd, gn, best)
                bidx = jnp.where(upd, n, bidx)
            for n in range(N_PAST_BLOCKS):
                gate_scr[n] = jnp.where(bidx == n, -jnp.inf, gate_scr[n])

        m_all = m_own
        for pg in range(N_PAGES):
            picked = gate_scr[pg // PAGES_PER_BLOCK] == -jnp.inf
            m_all = jnp.maximum(m_all, jnp.where(picked, m_scr[pg], NEG))
        w_own = jnp.exp2(m_own - m_all)
        l_all = w_own * l_own
        o_all = tile4(w_own) * o_own
        for pg in range(N_PAGES):
            picked = gate_scr[pg // PAGES_PER_BLOCK] == -jnp.inf
            wp = jnp.where(picked, jnp.exp2(m_scr[pg] - m_all), 0.0)
            l_all = l_all + wp * l_scr[pg]
            o_all = o_all + tile4(wp) * acc_scr[pg]
        out = o_all / tile4(l_all)
        for i in range(DEC_SEQ):
            rows = out[i * A_HEADS:(i + 1) * A_HEADS, :]
            o_ref[i:i + 1, :] = jnp.sum(jnp.where(head_mask, rows, 0.0), axis=0, keepdims=True)


def _moba_sample(page_table, q, k_new, v_new, cache_kt, cache_vt):
    w = A_HEADS * A_HD
    nrow = DEC_SEQ * A_HEADS
    page_spec = lambda i: pl.BlockSpec(
        (None, A_HEADS, A_HD, PAGE_SIZE),
        lambda b, g, pt: (pt[b, g * PAGES_PER_STEP + i], 0, 0, 0))
    per_sample = lambda rows: pl.BlockSpec((None, rows, w), lambda b, g, pt: (b, 0, 0))
    grid_spec = pltpu.PrefetchScalarGridSpec(
        num_scalar_prefetch=1,
        grid=(DEC_BATCH, N_PAGES // PAGES_PER_STEP),
        in_specs=([per_sample(DEC_SEQ), per_sample(SUBLANES), per_sample(SUBLANES)]
                  + [page_spec(i) for i in range(PAGES_PER_STEP)]
                  + [page_spec(i) for i in range(PAGES_PER_STEP)]),
        out_specs=per_sample(DEC_SEQ),
        scratch_shapes=[pltpu.VMEM((N_PAGES, nrow, LANES), F32),
                        pltpu.VMEM((N_PAGES, nrow, LANES), F32),
                        pltpu.VMEM((N_PAGES, nrow, LANES), F32),
                        pltpu.VMEM((N_PAST_BLOCKS, nrow, LANES), F32),
                        pltpu.VMEM((N_PAGES, nrow, w), F32)],
    )
    vmem = (2 * 2 * PAGES_PER_STEP * PAGE_SIZE * w * 4
            + 4 * N_PAGES * nrow * LANES * 4 + N_PAGES * nrow * w * 4)
    return pl.pallas_call(
        _moba_sample_kernel,
        grid_spec=grid_spec,
        out_shape=jax.ShapeDtypeStruct((DEC_BATCH, DEC_SEQ, w), F32),
        compiler_params=_params(("parallel", "arbitrary"), vmem / MIB + 8),
        name="moba_sample",
    )(page_table, q, k_new, v_new,
      *([cache_kt] * PAGES_PER_STEP), *([cache_vt] * PAGES_PER_STEP))


def _gmlp_kernel(u_ref, gv_ref, w_ref, bias_ref, avg_ref, lng_ref, lnb_ref, ob_ref, *vn_refs):
    chunks = [slice(c * B_CHUNK, (c + 1) * B_CHUNK) for c in range(u_ref.shape[0] // B_CHUNK)]
    avg = avg_ref[...]
    gvs = [_gelu_tanh(gv_ref[rows, :]) for rows in chunks]
    centred = [gv - _dot_exact_rhs(gv, avg) for gv in gvs]
    variances = [_dot_exact_rhs(gc * gc, avg, pieces=2) for gc in centred]
    pair_w = 2 * B_GD
    lane = lax.broadcasted_iota(jnp.int32, (B_CHUNK, pair_w), 1)
    for rows, gc, var in zip(chunks, centred, variances):
        vn = gc * lax.rsqrt(var + LN_EPS) * lng_ref[...] + lnb_ref[...]
        if vn_refs:
            vn_refs[0][rows, :] = vn
        vb = vn.astype(BF16)
        u = _gelu_tanh(u_ref[rows, :])
        for pr in range(B_GROUPS // 2):
            sl = slice(pr * pair_w, (pr + 1) * pair_w)
            vp = vb[:, sl]
            zero = jnp.zeros_like(vp)
            mixed = (_dot(w_ref[2 * pr], jnp.where(lane < B_GD, vp, zero))
                     + _dot(w_ref[2 * pr + 1], jnp.where(lane >= B_GD, vp, zero)))
            ob_ref[rows, sl] = (u[:, sl] * (mixed + bias_ref[:, sl])).astype(ob_ref.dtype)


def _gmlp(src, u_col, gv_col, w_masked, bias, avg, ln_g, ln_b, *, out_dtype, emit_vn):
    rows = src.shape[0]
    w = B_GROUPS * B_GD
    tile = B_CHUNK * min(GMLP_GROUP, rows // B_CHUNK)
    out_shape = [jax.ShapeDtypeStruct((rows, w), out_dtype)]
    out_specs = [pl.BlockSpec((tile, w), lambda c: (c, 0))]
    if emit_vn:
        out_shape.append(jax.ShapeDtypeStruct((rows, w), F32))
        out_specs.append(pl.BlockSpec((tile, w), lambda c: (c, 0)))
    return pl.pallas_call(
        _gmlp_kernel,
        grid=(rows // tile,),
        in_specs=[pl.BlockSpec((tile, w), lambda c: (c, u_col)),
                  pl.BlockSpec((tile, w), lambda c: (c, gv_col)),
                  _resident((B_GROUPS, B_CHUNK, B_CHUNK)), _resident((B_CHUNK, w)),
                  _resident((w, w)), _resident((1, w)), _resident((1, w))],
        out_specs=out_specs,
        out_shape=out_shape,
        compiler_params=_params(("parallel",), 24),
        name="gmlp_gate",
    )(src, src, w_masked, bias, avg, ln_g.reshape(1, w), ln_b.reshape(1, w))


def _pool_kernel(halo_ref, x_ref, w_ref, sc_ref, y_ref, *, rows, pos0, tiles_per_seq, fresh):
    t = pl.program_id(0) % tiles_per_seq
    halo = halo_ref[...]
    if fresh:
        halo = jnp.where(t == 0, 0.0, halo)
    x = x_ref[...]
    ext = jnp.concatenate([halo, x], axis=0)
    pos = pos0 + t * rows + lax.broadcasted_iota(jnp.int32, (rows, C_GD), 0)
    for gi, win in enumerate(POOL_WINDOWS):
        sl = slice(gi * C_GD, (gi + 1) * C_GD)
        s = ext[:, sl]
        sh = 1
        while sh < win:
            s = s + pltpu.roll(s, sh, 0)
            sh *= 2
        cnt = jnp.minimum(win, pos + 1).astype(F32)
        pooled = s[POOL_HALO:, :] / cnt - x[:, sl]
        y_ref[:, sl] = (_dot(pooled.astype(BF16), w_ref[gi]) * sc_ref[:, sl]).astype(y_ref.dtype)


def _pool(halo_src, x_src, w_bf16, scale, *, n_rows, rows, pos0, tiles_per_seq, fresh, out_dtype):
    w = len(POOL_WINDOWS) * C_GD
    if fresh:
        step = rows // POOL_HALO
        halo_map = lambda i: (jnp.maximum(i * step - 1, 0), 0)
    else:
        halo_map = lambda i: (i, 0)
    return pl.pallas_call(
        functools.partial(_pool_kernel, rows=rows, pos0=pos0, tiles_per_seq=tiles_per_seq,
                          fresh=fresh),
        grid=(n_rows // rows,),
        in_specs=[pl.BlockSpec((POOL_HALO, w), halo_map),
                  pl.BlockSpec((rows, w), lambda i: (i, 0)),
                  _resident((len(POOL_WINDOWS), C_GD, C_GD)), _resident((1, w))],
        out_specs=pl.BlockSpec((rows, w), lambda i: (i, 0)),
        out_shape=jax.ShapeDtypeStruct((n_rows, w), out_dtype),
        compiler_params=_params(("arbitrary",), 24),
        name="pool_mix",
    )(halo_src, x_src, w_bf16, scale.reshape(1, w))


def _hgrn_kernel(q_ref, f_ref, i_ref, g_ref, s0_ref, lb_ref, ng_ref, o_ref, sfin_ref, s_scr,
                 *, rows, in_rows, valid):
    t = pl.program_id(1)
    c_rows = HGRN_CHUNK

    @pl.when(t == 0)
    def _():
        s_scr[...] = s0_ref[...]

    r_io = lax.broadcasted_iota(jnp.int32, (c_rows, c_rows), 0)
    c_io = lax.broadcasted_iota(jnp.int32, (c_rows, c_rows), 1)
    causal = r_io >= c_io
    ltri = jnp.where(causal, 1.0, 0.0).astype(BF16)
    lgrp = jnp.where(c_io < (r_io // HGRN_SUB) * HGRN_SUB + HGRN_SUB // 2, 1.0, 0.0).astype(BF16)
    lsum = jnp.concatenate([ltri, lgrp], axis=0)
    eye = r_io == c_io
    row_id = lax.broadcasted_iota(jnp.int32, (c_rows, D_HK), 0)

    heads = [slice(hd * D_HK, (hd + 1) * D_HK) for hd in range(D_HEADS)]

    def load(ref, r0, sl):
        if in_rows == rows:
            return ref[pl.ds(r0, c_rows), sl]
        return jnp.concatenate([ref[:, sl], jnp.zeros((c_rows - in_rows, D_HK), F32)], axis=0)

    def decay_sums(r0):
        gates = []
        for sl in heads:
            lb = lb_ref[:, sl]
            half_span = 0.5 * (1.0 - lb)
            f = (lb + half_span) + half_span * jnp.tanh(0.5 * load(f_ref, r0, sl))
            logf = jnp.log2(f)
            kk = 1.0 - f
            if valid < rows:
                live = (t * rows + r0 + row_id) < valid
                logf = jnp.where(live, logf, 0.0)
                kk = jnp.where(live, kk, 0.0)
            sums = _dot_exact_lhs(lsum, logf)
            gates.append((kk, sums[:c_rows, :], sums[c_rows:, :]))
        return gates

    def chunk_scores(r0, gates):
        attns, queries = [], []
        for sl, (kk, cg, ref) in zip(heads, gates):
            q = _silu(load(q_ref, r0, sl))
            qd = (q * jnp.exp2(cg - ref)).astype(BF16)
            blocks = []
            for i in range(c_rows // HGRN_SUB):
                ref_i = ref[i * HGRN_SUB:i * HGRN_SUB + 1, :]
                e = jnp.where(row_id < (i + 1) * HGRN_SUB, ref_i - cg, 0.0)
                k_i = (kk * jnp.exp2(e)).astype(BF16)
                blocks.append(_dot_nt(qd[i * HGRN_SUB:(i + 1) * HGRN_SUB, :], k_i))
            attns.append(jnp.where(causal, jnp.concatenate(blocks, axis=0), 0.0).astype(BF16))
            queries.append((q * jnp.exp2(cg)).astype(BF16))
        return attns, queries

    def advance_state(r0, gates, attns, queries):
        outs = []
        for hd, (sl, (kk, cg, _)) in enumerate(zip(heads, gates)):
            vb = load(i_ref, r0, sl).astype(BF16)
            state = s_scr[hd]
            outs.append(_dot(queries[hd], state.astype(BF16)) + _dot(attns[hd], vb))
            g_last = cg[c_rows - 1:c_rows, :]
            kd = kk * jnp.exp2(g_last - cg)
            decay_col = jnp.sum(
                jnp.where(eye, jnp.broadcast_to(jnp.exp2(g_last), (c_rows, D_HK)), 0.0),
                axis=1, keepdims=True)
            s_scr[hd] = state * decay_col + _dot(kd.T.astype(BF16), vb)
        return outs

    def finish(r0, outs):
        for sl, o in zip(heads, outs):
            o = o * lax.rsqrt(jnp.mean(o * o, axis=-1, keepdims=True) + RMS_EPS) * ng_ref[...]
            o = (o * _silu(load(g_ref, r0, sl))).astype(o_ref.dtype)
            if in_rows == rows:
                o_ref[pl.ds(r0, c_rows), sl] = o
            else:
                o_ref[:, sl] = o[:in_rows, :]

    group = min(HGRN_GROUP, rows // c_rows)

    def chunk_group(c, carry):
        starts = [pl.multiple_of((c * group + k) * c_rows, c_rows) for k in range(group)]
        gates = [decay_sums(r0) for r0 in starts]
        outs = []
        for r0, gt in zip(starts, gates):
            attns, queries = chunk_scores(r0, gt)
            outs.append(advance_state(r0, gt, attns, queries))
        for r0, o in zip(starts, outs):
            finish(r0, o)
        return carry

    lax.fori_loop(0, rows // (c_rows * group), chunk_group, 0)

    @pl.when(t == pl.num_programs(1) - 1)
    def _():
        sfin_ref[...] = s_scr[...]


def _hgrn(src, s0, lb, norm_g, *, bsz, length, rows, in_rows, valid, out_dtype):
    w = D_HEADS * D_HK
    tiles = length // rows
    col = lambda k: pl.BlockSpec((in_rows, w), lambda b, t: (b * tiles + t, 1 + k))
    state_spec = pl.BlockSpec((None, D_HEADS, D_HK, D_HK), lambda b, t: (b, 0, 0, 0))
    return pl.pallas_call(
        functools.partial(_hgrn_kernel, rows=rows, in_rows=in_rows, valid=valid),
        grid=(bsz, tiles),
        in_specs=[col(0), col(1), col(2), col(3), state_spec, _resident((1, w)),
                  _resident((1, D_HK))],
        out_specs=[pl.BlockSpec((in_rows, w), lambda b, t: (b * tiles + t, 0)), state_spec],
        out_shape=[jax.ShapeDtypeStruct((bsz * tiles * in_rows, w), out_dtype),
                   jax.ShapeDtypeStruct((bsz, D_HEADS, D_HK, D_HK), F32)],
        scratch_shapes=[pltpu.VMEM((D_HEADS, D_HK, D_HK), F32)],
        compiler_params=_params(("parallel", "arbitrary"), 32),
        name="hgrn2",
    )(src, src, src, src, s0, lb.reshape(1, w), norm_g.reshape(1, D_HK))


def kernel(x_prompt, x_sample, cache_k, cache_v, state_pool, state_hgrn, page_table, w_in_even, w_out_even, gmlp_ws, gmlp_bs, gmlp_ln_g, gmlp_ln_b, w_in_odd, w_out_odd, pool_w, pool_scale, hgrn_lb_param, hgrn_norm_g, ln_mix_g, ln_mix_b, ln_ffn_g, ln_ffn_b, ffn_w1, ffn_w2):
    w = HALF_W
    xp = x_prompt.reshape(N_PROMPT, D_MODEL)
    xs = x_sample.reshape(N_SAMPLE, D_MODEL)
    pad_tokens = ((0, 0), (0, SUBLANES - DEC_SEQ), (0, 0))

    col_scale = jnp.where(jnp.arange(EVEN_IN) < w, SCORE_SCALE, 1.0)
    w_even = (w_in_even[0] * col_scale).astype(BF16)
    w_k_u_gv = jnp.concatenate([w_even[:, w:2 * w], w_even[:, 3 * w:]], axis=1)
    kb, ugv, qt, vt, kt32, vt32 = _proj_even_prompt(x_prompt, w_k_u_gv, w_even[:, :3 * w].T)
    oa_prompt = _moba_prompt(qt, kb, vt)

    h0s = _proj(xs, w_even, SAMPLE_TILE)
    hs = h0s.reshape(DEC_BATCH, DEC_SEQ, EVEN_IN)
    oa_sample = _moba_sample(page_table, hs[..., :w],
                             jnp.pad(hs[..., w:2 * w], pad_tokens), jnp.pad(hs[..., 2 * w:3 * w], pad_tokens),
                             cache_k[0].transpose(0, 2, 3, 1), cache_v[0].transpose(0, 2, 3, 1))

    tril = jnp.tril(jnp.ones((B_CHUNK, B_CHUNK), bool))
    ws_prompt = jnp.where(tril[None], gmlp_ws[0], 0.0)
    ws_sample = jax.vmap(lambda m: jnp.kron(jnp.eye(DEC_BATCH, dtype=F32), m[:DEC_SEQ, :DEC_SEQ]))(ws_prompt)
    bias_prompt = jnp.repeat(gmlp_bs[0].T, B_GD, axis=1)
    bias_sample = jnp.tile(bias_prompt[:DEC_SEQ], (DEC_BATCH, 1))
    grp = jnp.arange(w) // B_GD
    avg = jnp.where(grp[:, None] == grp[None, :], 1.0 / B_GD, 0.0).astype(BF16)
    ob_prompt, = _gmlp(ugv, 0, 1, ws_prompt.astype(BF16), bias_prompt, avg, gmlp_ln_g[0], gmlp_ln_b[0],
                       out_dtype=BF16, emit_vn=False)
    ob_sample, vn_sample = _gmlp(h0s, 3, 4, ws_sample.astype(BF16), bias_sample, avg, gmlp_ln_g[0],
                                 gmlp_ln_b[0], out_dtype=F32, emit_vn=True)

    layer0 = (w_out_even[0].astype(BF16), ln_mix_g[0], ln_mix_b[0],
              ffn_w1[0].astype(BF16), ffn_w2[0].astype(BF16), ln_ffn_g[0], ln_ffn_b[0])
    xp = _mix_ffn(oa_prompt, ob_prompt, xp, *layer0, PROMPT_TILE)
    xs = _mix_ffn(oa_sample.reshape(N_SAMPLE, w), ob_sample, xs, *layer0, SAMPLE_TILE)

    per_head = lambda t: t.reshape(BATCH, A_HEADS, A_HD, SEQ).transpose(0, 3, 1, 2)[None]
    new_k_prompt = per_head(kt32)
    new_v_prompt = per_head(vt32)
    new_k_sample = hs[..., w:2 * w].reshape(1, DEC_BATCH, DEC_SEQ, A_HEADS, A_HD)
    new_v_sample = hs[..., 2 * w:3 * w].reshape(1, DEC_BATCH, DEC_SEQ, A_HEADS, A_HD)
    new_gmlp_v_sample = vn_sample.reshape(1, DEC_BATCH, DEC_SEQ, w)

    w_odd = w_in_odd[0].astype(BF16)
    lb = jax.nn.softmax(hgrn_lb_param.astype(F32), axis=0)[0]
    pool_wb = pool_w[0].astype(BF16)

    h1p = _proj(xp, w_odd, PROMPT_TILE)
    oc_prompt = _pool(h1p, h1p, pool_wb, pool_scale[0], n_rows=N_PROMPT, rows=POOL_ROWS, pos0=0,
                      tiles_per_seq=SEQ // POOL_ROWS, fresh=True, out_dtype=BF16)
    od_prompt, s_prompt = _hgrn(h1p, jnp.zeros((BATCH, D_HEADS, D_HK, D_HK), F32), lb, hgrn_norm_g[0],
                                bsz=BATCH, length=SEQ, rows=HGRN_ROWS, in_rows=HGRN_ROWS, valid=SEQ,
                                out_dtype=BF16)

    h1s = _proj(xs, w_odd, SAMPLE_TILE).reshape(DEC_BATCH, DEC_SEQ, ODD_IN)
    h1s_pad = jnp.pad(h1s, pad_tokens).reshape(DEC_BATCH * SUBLANES, ODD_IN)
    halo_s = jnp.pad(state_pool[0], ((0, 0), (POOL_HALO - POOL_BUF, 0), (0, 0)))
    oc_sample = _pool(halo_s.reshape(DEC_BATCH * POOL_HALO, w), h1s_pad, pool_wb, pool_scale[0],
                      n_rows=DEC_BATCH * SUBLANES, rows=SUBLANES, pos0=PAST_LEN, tiles_per_seq=1,
                      fresh=False, out_dtype=F32)
    od_sample, s_sample = _hgrn(h1s_pad, state_hgrn[0], lb, hgrn_norm_g[0], bsz=DEC_BATCH,
                                length=HGRN_CHUNK, rows=HGRN_CHUNK, in_rows=SUBLANES, valid=DEC_SEQ,
                                out_dtype=F32)
    real_rows = lambda t: t.reshape(DEC_BATCH, SUBLANES, w)[:, :DEC_SEQ].reshape(N_SAMPLE, w)

    layer1 = (w_out_odd[0].astype(BF16), ln_mix_g[1], ln_mix_b[1],
              ffn_w1[1].astype(BF16), ffn_w2[1].astype(BF16), ln_ffn_g[1], ln_ffn_b[1])
    xp = _mix_ffn(oc_prompt, od_prompt, xp, *layer1, PROMPT_TILE)
    xs = _mix_ffn(real_rows(oc_sample), real_rows(od_sample), xs, *layer1, SAMPLE_TILE)

    new_pool_prompt = h1p.reshape(BATCH, SEQ, ODD_IN)[:, SEQ - POOL_BUF:, :w][None]
    new_pool_sample = jnp.concatenate([state_pool[0][:, DEC_SEQ:], h1s[..., :w]], axis=1)[None]
    return (xp.reshape(BATCH, SEQ, D_MODEL), xs.reshape(DEC_BATCH, DEC_SEQ, D_MODEL),
            new_k_prompt, new_v_prompt, new_k_sample, new_v_sample, new_gmlp_v_sample,
            new_pool_prompt, new_pool_sample, s_prompt[None], s_sample[None])
```

```python
import functools
import math

import jax
import jax.numpy as jnp
from jax import lax
from jax.experimental import pallas as pl
from jax.experimental.pallas import tpu as pltpu

F32 = jnp.float32
BF16 = jnp.bfloat16

D_MODEL = 1024
BATCH = 4
SEQ = 4096
DEPTH = 2
DEC_BATCH = 32
DEC_SEQ = 4
PAST_LEN = 8192
PAGE_SIZE = 128
HALF_W = D_MODEL // 2
A_HD = 64
A_HEADS = HALF_W // A_HD
MOBA_BLOCK = 256
MOBA_TOPK = 3
B_GROUPS = 8
B_GD = HALF_W // B_GROUPS
B_CHUNK = 128
POOL_WINDOWS = (2, 4, 8, 16)
C_GD = HALF_W // len(POOL_WINDOWS)
POOL_BUF = max(POOL_WINDOWS) - 1
D_HK = 128
D_HEADS = HALF_W // D_HK
D_FF = 4 * D_MODEL
EVEN_IN = 5 * HALF_W
ODD_IN = 5 * HALF_W
ALPHA = (2 * DEPTH) ** 0.25
LN_EPS = 1e-5
RMS_EPS = 1e-6
NEG = -1e30

N_PROMPT = BATCH * SEQ
N_SAMPLE = DEC_BATCH * DEC_SEQ
N_PAGES = PAST_LEN // PAGE_SIZE
N_PAST_BLOCKS = PAST_LEN // MOBA_BLOCK
PAGES_PER_BLOCK = MOBA_BLOCK // PAGE_SIZE
N_KEY_BLOCKS = SEQ // MOBA_BLOCK

LANES = 128
SUBLANES = 8
PROMPT_TILE = 512
SAMPLE_TILE = N_SAMPLE
FF_CHUNK = 1024
MOBA_LOOKAHEAD = 7
SAMPLE_PAGES_PER_TILE = DEC_BATCH * N_PAGES // (BATCH * N_KEY_BLOCKS)
TILES_PER_SAMPLE = N_PAGES // SAMPLE_PAGES_PER_TILE
assert SAMPLE_PAGES_PER_TILE * BATCH * N_KEY_BLOCKS == DEC_BATCH * N_PAGES
assert TILES_PER_SAMPLE * SAMPLE_PAGES_PER_TILE == N_PAGES and N_KEY_BLOCKS % TILES_PER_SAMPLE == 0
assert SAMPLE_PAGES_PER_TILE % PAGES_PER_BLOCK == 0
V_TAIL = 16
V_SLAB = A_HD + V_TAIL
V_ROWS = A_HEADS * V_SLAB
assert A_HEADS % (MOBA_LOOKAHEAD + 1) == 0
SCORE_SCALE = (A_HD ** -0.5) * math.log2(math.e)
HGRN_CHUNK = 128
HGRN_SUB = 32
HGRN_ROWS = 512
HGRN_GROUP = 2
GMLP_GROUP = 4
POOL_ROWS = 512
POOL_HALO = 16
MIB = 1024 * 1024


def _dot(a, b):
    return jnp.dot(a, b, preferred_element_type=F32)


def _dot_nt(a, b):
    return lax.dot_general(a, b, (((1,), (1,)), ((), ())), preferred_element_type=F32)


def _split3(x):
    p0 = x.astype(BF16)
    r1 = x - p0.astype(F32)
    p1 = r1.astype(BF16)
    p2 = (r1 - p1.astype(F32)).astype(BF16)
    return p0, p1, p2


def _dot_exact_lhs(m_bf16, x):
    p0, p1, p2 = _split3(x)
    return _dot(m_bf16, p0) + _dot(m_bf16, p1) + _dot(m_bf16, p2)


def _dot_exact_rhs(x, m_bf16, pieces=3):
    parts = _split3(x)[:pieces]
    out = _dot(parts[0], m_bf16)
    for part in parts[1:]:
        out = out + _dot(part, m_bf16)
    return out


def _gelu_tanh(x):
    return 0.5 * x * (1.0 + jnp.tanh(0.7978845608028654 * (x + 0.044715 * (x * x * x))))


def _silu(x):
    h = 0.5 * x
    return h + h * jnp.tanh(h)


def _layer_norm(z, g, b):
    mu = jnp.mean(z, axis=-1, keepdims=True)
    zc = z - mu
    var = jnp.mean(zc * zc, axis=-1, keepdims=True)
    return zc * lax.rsqrt(var + LN_EPS) * g + b


def _params(sem, vmem_mib):
    return pltpu.CompilerParams(dimension_semantics=sem, vmem_limit_bytes=int(vmem_mib * MIB))


def _resident(shape):
    return pl.BlockSpec(shape, lambda *_: (0,) * len(shape), pipeline_mode=pl.Buffered(1))


def _proj_kernel(x_ref, w_ref, h_ref):
    xb = x_ref[...].astype(BF16)
    for c in range(w_ref.shape[1] // HALF_W):
        sl = slice(c * HALF_W, (c + 1) * HALF_W)
        h_ref[:, sl] = _dot(xb, w_ref[:, sl])


def _proj(x, w_bf16, tile):
    n, d = x.shape
    n_out = w_bf16.shape[1]
    vmem = 2 * (tile * d * 4 + tile * n_out * 4) + d * n_out * 2
    return pl.pallas_call(
        _proj_kernel,
        grid=(n // tile,),
        in_specs=[pl.BlockSpec((tile, d), lambda i: (i, 0)), _resident((d, n_out))],
        out_specs=pl.BlockSpec((tile, n_out), lambda i: (i, 0)),
        out_shape=jax.ShapeDtypeStruct((n, n_out), F32),
        compiler_params=_params(("parallel",), vmem / MIB + 8),
        name="in_proj",
    )(x, w_bf16)


def _proj_even_kernel(x_ref, wn_ref, wt_ref, kb_ref, ugv_ref, qt_ref, vt_ref, kt32_ref, vt32_ref):
    w = HALF_W
    xb = x_ref[...].astype(BF16)
    k = _dot(xb, wn_ref[:, :w])
    for half in range(PROMPT_TILE // MOBA_BLOCK):
        kb_ref[half] = k[half * MOBA_BLOCK:(half + 1) * MOBA_BLOCK, :].astype(BF16)
    ugv_ref[:, :w] = _dot(xb, wn_ref[:, w:2 * w])
    ugv_ref[:, w:] = _dot(xb, wn_ref[:, 2 * w:])
    qt = _dot_nt(wt_ref[:w, :], xb)
    kt32_ref[...] = _dot_nt(wt_ref[w:2 * w, :], xb)
    vt = _dot_nt(wt_ref[2 * w:, :], xb)
    vt32_ref[...] = vt
    ones_row = jnp.where(lax.broadcasted_iota(jnp.int32, (V_TAIL, MOBA_BLOCK), 0) == 0,
                         1.0, 0.0).astype(BF16)
    for half in range(PROMPT_TILE // MOBA_BLOCK):
        cols = slice(half * MOBA_BLOCK, (half + 1) * MOBA_BLOCK)
        qt_ref[half] = qt[:, cols].astype(BF16)
        for h in range(A_HEADS):
            vt_ref[half, h * V_SLAB:h * V_SLAB + A_HD, :] = vt[h * A_HD:(h + 1) * A_HD, cols].astype(BF16)
            vt_ref[half, h * V_SLAB + A_HD:(h + 1) * V_SLAB, :] = ones_row


def _proj_even_prompt(x_prompt, w_nat, w_tr):
    w = HALF_W
    tiles = SEQ // PROMPT_TILE
    per_tile = PROMPT_TILE // MOBA_BLOCK
    blocked = lambda shape: pl.BlockSpec((None, per_tile) + shape, lambda b, t: (b, t, 0, 0))
    vmem = (2 * (PROMPT_TILE * D_MODEL * 4 + PROMPT_TILE * 2 * w * 4 + 3 * PROMPT_TILE * w * 2
                 + 2 * PROMPT_TILE * w * 4) + 2 * D_MODEL * 3 * w * 2 + 4 * PROMPT_TILE * w * 4)
    return pl.pallas_call(
        _proj_even_kernel,
        grid=(BATCH, tiles),
        in_specs=[pl.BlockSpec((None, PROMPT_TILE, D_MODEL), lambda b, t: (b, t, 0)),
                  _resident((D_MODEL, 3 * w)), _resident((3 * w, D_MODEL))],
        out_specs=[blocked((MOBA_BLOCK, w)),
                   pl.BlockSpec((PROMPT_TILE, 2 * w), lambda b, t: (b * tiles + t, 0)),
                   blocked((w, MOBA_BLOCK)), blocked((V_ROWS, MOBA_BLOCK)),
                   pl.BlockSpec((None, w, PROMPT_TILE), lambda b, t: (b, 0, t)),
                   pl.BlockSpec((None, w, PROMPT_TILE), lambda b, t: (b, 0, t))],
        out_shape=[jax.ShapeDtypeStruct((BATCH, N_KEY_BLOCKS, MOBA_BLOCK, w), BF16),
                   jax.ShapeDtypeStruct((N_PROMPT, 2 * w), F32),
                   jax.ShapeDtypeStruct((BATCH, N_KEY_BLOCKS, w, MOBA_BLOCK), BF16),
                   jax.ShapeDtypeStruct((BATCH, N_KEY_BLOCKS, V_ROWS, MOBA_BLOCK), BF16),
                   jax.ShapeDtypeStruct((BATCH, w, SEQ), F32),
                   jax.ShapeDtypeStruct((BATCH, w, SEQ), F32)],
        compiler_params=_params(("parallel", "parallel"), vmem / MIB + 8),
        name="in_proj_even",
    )(x_prompt, w_nat, w_tr)


def _mix_ffn_kernel(oa_ref, ob_ref, x_ref, wo_ref, g1_ref, b1_ref, w1_ref, w2_ref, g2_ref, b2_ref,
                    y_ref):
    mixed = _dot(oa_ref[...].astype(BF16), wo_ref[:HALF_W, :])
    mixed = mixed + _dot(ob_ref[...].astype(BF16), wo_ref[HALF_W:, :])
    x = _layer_norm(ALPHA * x_ref[...] + mixed, g1_ref[...], b1_ref[...])
    xb = x.astype(BF16)
    acc = jnp.zeros(x.shape, F32)
    for c in range(D_FF // FF_CHUNK):
        sl = slice(c * FF_CHUNK, (c + 1) * FF_CHUNK)
        hid = jnp.maximum(_dot(xb, w1_ref[:, sl]), 0.0)
        acc = acc + _dot((hid * hid).astype(BF16), w2_ref[sl, :])
    y_ref[...] = _layer_norm(ALPHA * x + acc, g2_ref[...], b2_ref[...])


def _mix_ffn(oa, ob, x, wo_bf16, g1, b1, w1_bf16, w2_bf16, g2, b2, tile):
    n, d = x.shape
    row = lambda i: (i, 0)
    vec = lambda v: v.reshape(1, d)
    vmem = (2 * (2 * tile * HALF_W * oa.dtype.itemsize + 2 * tile * d * 4)
            + (d * d + 2 * d * D_FF) * 2 + 4 * tile * FF_CHUNK * 4)
    return pl.pallas_call(
        _mix_ffn_kernel,
        grid=(n // tile,),
        in_specs=[pl.BlockSpec((tile, HALF_W), row), pl.BlockSpec((tile, HALF_W), row),
                  pl.BlockSpec((tile, d), row), _resident((d, d)),
                  _resident((1, d)), _resident((1, d)),
                  _resident((d, D_FF)), _resident((D_FF, d)),
                  _resident((1, d)), _resident((1, d))],
        out_specs=pl.BlockSpec((tile, d), row),
        out_shape=jax.ShapeDtypeStruct((n, d), F32),
        compiler_params=_params(("parallel",), vmem / MIB + 8),
        name="mix_ffn_ln",
    )(oa, ob, x, wo_bf16, vec(g1), vec(b1), w1_bf16, w2_bf16, vec(g2), vec(b2))


def _moba_kernel(pt_ref, qt_ref, k_ref, vt_ref, qs_ref, kn_ref, vn_ref, *refs):
    del pt_ref
    kp_refs = refs[:SAMPLE_PAGES_PER_TILE]
    vp_refs = refs[SAMPLE_PAGES_PER_TILE:2 * SAMPLE_PAGES_PER_TILE]
    (o_ref, os_ref, kmt_scr, qtz_scr, selb_scr, m_scr, acc_scr, s_scr, ref_scr, a_scr,
     sm_scr, sl_scr, sgate_scr, sacc_scr) = refs[2 * SAMPLE_PAGES_PER_TILE:]
    j = pl.program_id(1)
    nb = k_ref.shape[0]
    blk = MOBA_BLOCK
    pair_w = 2 * A_HD
    w = A_HEADS * A_HD
    nrow = DEC_SEQ * A_HEADS
    part = (pl.program_id(0) * nb + j) % TILES_PER_SAMPLE
    wide = lambda c: jnp.broadcast_to(c, (nrow, LANES))

    @pl.when(j == 0)
    def _():
        rows = [jnp.sum(k_ref[n].astype(F32), axis=0, keepdims=True) * (1.0 / blk)
                for n in range(nb)]
        kmean = jnp.concatenate(rows, axis=0)
        tiled = jnp.concatenate([kmean] * A_HEADS, axis=0)
        rh = lax.broadcasted_iota(jnp.int32, tiled.shape, 0) // nb
        ch = lax.broadcasted_iota(jnp.int32, tiled.shape, 1) // A_HD
        kmt_scr[...] = jnp.where(rh == ch, tiled, 0.0)

    head_mask = (lax.broadcasted_iota(jnp.int32, (A_HEADS, w), 1) // A_HD
                 == lax.broadcasted_iota(jnp.int32, (A_HEADS, w), 0))
    qs = qs_ref[...]
    qbd = jnp.concatenate(
        [jnp.where(head_mask, jnp.broadcast_to(qs[i:i + 1, :], (A_HEADS, w)), 0.0)
         for i in range(DEC_SEQ)], axis=0).astype(BF16)
    page_scores = [_dot(qbd, kp_refs[i][...].reshape(w, PAGE_SIZE).astype(BF16))
                   for i in range(SAMPLE_PAGES_PER_TILE)]

    def sample_blocks():
        probs = []
        for bl in range(SAMPLE_PAGES_PER_TILE // PAGES_PER_BLOCK):
            s = jnp.concatenate(page_scores[bl * PAGES_PER_BLOCK:(bl + 1) * PAGES_PER_BLOCK], axis=1)
            blk_id = part * (SAMPLE_PAGES_PER_TILE // PAGES_PER_BLOCK) + bl
            m = jnp.max(s, axis=-1, keepdims=True)
            p = jnp.exp2(s - m)
            sm_scr[blk_id] = wide(m)
            sl_scr[blk_id] = wide(jnp.sum(p, axis=-1, keepdims=True))
            sgate_scr[blk_id] = wide(jnp.sum(s, axis=-1, keepdims=True) * (1.0 / MOBA_BLOCK))
            probs.append(p.astype(BF16))
        for bl, p in enumerate(probs):
            blk_id = part * (SAMPLE_PAGES_PER_TILE // PAGES_PER_BLOCK) + bl
            acc = None
            for r in range(PAGES_PER_BLOCK):
                vpage = vp_refs[bl * PAGES_PER_BLOCK + r][...].reshape(w, PAGE_SIZE).astype(BF16)
                pv = _dot_nt(p[:, r * PAGE_SIZE:(r + 1) * PAGE_SIZE], vpage)
                acc = pv if acc is None else acc + pv
            sacc_scr[blk_id] = acc

    qts = qt_ref[...]
    k0, k1, k2 = _split3(kmt_scr[...])
    gate = _dot(k0, qts) + _dot(k1, qts) + _dot(k2, qts)

    n_io = lax.broadcasted_iota(jnp.int32, (nb, blk), 0)
    half = lax.broadcasted_iota(jnp.int32, (pair_w, blk), 0) // A_HD
    for h in range(A_HEADS):
        g = jnp.where(n_io < j, gate[h * nb:(h + 1) * nb, :], NEG)
        bias = jnp.full((nb, blk), NEG, F32)
        for _ in range(MOBA_TOPK):
            mx = jnp.max(g, axis=0, keepdims=True)
            cand = jnp.where((g == mx) & (mx > 0.5 * NEG), n_io, nb)
            pick = n_io == jnp.min(cand, axis=0, keepdims=True)
            bias = jnp.where(pick, 0.0, bias)
            g = jnp.where(pick, NEG, g)
        selb_scr[h * nb:(h + 1) * nb, :] = bias
        pr = h // 2
        qpair = qts[pr * pair_w:(pr + 1) * pair_w, :]
        qtz_scr[h] = jnp.where(half == (h % 2), qpair, jnp.zeros_like(qpair))

    causal = (lax.broadcasted_iota(jnp.int32, (blk, blk), 0)
              <= lax.broadcasted_iota(jnp.int32, (blk, blk), 1))

    def fold_rows(x, op):
        return op(x.reshape(blk // SUBLANES, SUBLANES, blk), axis=0)

    def stage_scores(n, h, slot, own):
        pr = h // 2
        s = _dot(k_ref[n, :, pr * pair_w:(pr + 1) * pair_w], qtz_scr[h])
        if own:
            s = jnp.where(causal, s, NEG)
        s_scr[slot] = s
        col_max = jnp.max(fold_rows(s, jnp.max), axis=0, keepdims=True)
        if own:
            m_scr[h] = col_max
            ref_scr[slot] = col_max
        else:
            bias = selb_scr[pl.ds(h * nb + n, 1), :]
            m_old = m_scr[h]
            m_new = jnp.maximum(m_old, col_max + bias)
            m_scr[h] = m_new
            a_scr[slot] = jnp.exp2(m_old - m_new)
            ref_scr[slot] = m_new - bias

    def accumulate(n, h, slot, own):
        vs = slice(h * V_SLAB, (h + 1) * V_SLAB)
        p = jnp.exp2(s_scr[slot] - ref_scr[slot])
        pv = _dot(vt_ref[n, vs, :], p.astype(BF16))
        if own:
            acc_scr[vs, :] = pv
        else:
            acc_scr[vs, :] = a_scr[slot] * acc_scr[vs, :] + pv

    n_slots = MOBA_LOOKAHEAD + 1

    def stage_ahead(n, n_next, h, own):
        ha = h + MOBA_LOOKAHEAD
        if ha < A_HEADS:
            stage_scores(n, ha, ha % n_slots, own=own)
        else:
            stage_scores(n_next, ha - A_HEADS, ha % n_slots, own=False)

    for h in range(MOBA_LOOKAHEAD):
        stage_scores(j, h, h % n_slots, own=True)
    sample_blocks()
    for h in range(A_HEADS):
        stage_ahead(j, 0, h, own=True)
        accumulate(j, h, h % n_slots, own=True)

    def past_block(n, carry):
        for h in range(A_HEADS):
            stage_ahead(n, n + 1, h, own=False)
            accumulate(n, h, h % n_slots, own=False)
        return carry

    lax.fori_loop(0, j, past_block, 0)

    heads_out = []
    for h in range(A_HEADS):
        slab = acc_scr[h * V_SLAB:(h + 1) * V_SLAB, :]
        heads_out.append(slab[:A_HD, :] / slab[A_HD:A_HD + 1, :])
    o_ref[...] = jnp.concatenate(heads_out, axis=0).T.astype(o_ref.dtype)

    @pl.when(part == TILES_PER_SAMPLE - 1)
    def _():
        tile4 = lambda c: jnp.concatenate([c] * (w // LANES), axis=1)
        s = _dot_nt(qbd, kn_ref[...].astype(BF16))
        col = lax.broadcasted_iota(jnp.int32, s.shape, 1)
        qi = lax.broadcasted_iota(jnp.int32, s.shape, 0) // A_HEADS
        s = jnp.where(col <= qi, s, NEG)
        m_col = jnp.max(s, axis=-1, keepdims=True)
        p = jnp.exp2(s - m_col)
        m_own = wide(m_col)
        l_own = wide(jnp.sum(p, axis=-1, keepdims=True))
        o_own = _dot(p, vn_ref[...])

        for _ in range(MOBA_TOPK):
            best = jnp.full((nrow, LANES), NEG, F32)
            bidx = jnp.zeros((nrow, LANES), jnp.int32)
            for n in range(N_PAST_BLOCKS):
                gn = sgate_scr[n]
                upd = gn > best
                best = jnp.where(upd, gn, best)
                bidx = jnp.where(upd, n, bidx)
            for n in range(N_PAST_BLOCKS):
                sgate_scr[n] = jnp.where(bidx == n, -jnp.inf, sgate_scr[n])

        m_all = m_own
        for n in range(N_PAST_BLOCKS):
            m_all = jnp.maximum(m_all, jnp.where(sgate_scr[n] == -jnp.inf, sm_scr[n], NEG))
        w_own = jnp.exp2(m_own - m_all)
        l_all = w_own * l_own
        o_all = tile4(w_own) * o_own
        for n in range(N_PAST_BLOCKS):
            wn = jnp.where(sgate_scr[n] == -jnp.inf, jnp.exp2(sm_scr[n] - m_all), 0.0)
            l_all = l_all + wn * sl_scr[n]
            o_all = o_all + tile4(wn) * sacc_scr[n]
        out = o_all / tile4(l_all)
        for i in range(DEC_SEQ):
            rows = out[i * A_HEADS:(i + 1) * A_HEADS, :]
            os_ref[i:i + 1, :] = jnp.sum(jnp.where(head_mask, rows, 0.0), axis=0, keepdims=True)


def _moba(page_table, qt, kb, vt, q_s, k_new, v_new, cache_kt, cache_vt):
    bsz, nb = qt.shape[:2]
    w = A_HEADS * A_HD
    nrow = DEC_SEQ * A_HEADS
    n_slots = MOBA_LOOKAHEAD + 1
    step = lambda b, j: b * nb + j
    sample_of = lambda b, j: step(b, j) // TILES_PER_SAMPLE
    first_page = lambda b, j: (step(b, j) % TILES_PER_SAMPLE) * SAMPLE_PAGES_PER_TILE
    page_spec = lambda i: pl.BlockSpec(
        (None, A_HEADS, A_HD, PAGE_SIZE),
        lambda b, j, pt: (pt[sample_of(b, j), first_page(b, j) + i], 0, 0, 0))
    per_sample = lambda rows: pl.BlockSpec((None, rows, w), lambda b, j, pt: (sample_of(b, j), 0, 0))
    whole_row = lambda shape: pl.BlockSpec((None,) + shape, lambda b, j, pt: (b, 0, 0, 0),
                                           pipeline_mode=pl.Buffered(1))
    grid_spec = pltpu.PrefetchScalarGridSpec(
        num_scalar_prefetch=1,
        grid=(bsz, nb),
        in_specs=([pl.BlockSpec((None, None, w, MOBA_BLOCK), lambda b, j, pt: (b, j, 0, 0)),
                   whole_row((nb, MOBA_BLOCK, w)), whole_row((nb, V_ROWS, MOBA_BLOCK)),
                   per_sample(DEC_SEQ), per_sample(SUBLANES), per_sample(SUBLANES)]
                  + [page_spec(i) for i in range(SAMPLE_PAGES_PER_TILE)]
                  + [page_spec(i) for i in range(SAMPLE_PAGES_PER_TILE)]),
        out_specs=[pl.BlockSpec((MOBA_BLOCK, w), lambda b, j, pt: (step(b, j), 0)),
                   per_sample(DEC_SEQ)],
        scratch_shapes=[pltpu.VMEM((A_HEADS * nb, w), F32),
                        pltpu.VMEM((A_HEADS, 2 * A_HD, MOBA_BLOCK), BF16),
                        pltpu.VMEM((A_HEADS * nb, MOBA_BLOCK), F32),
                        pltpu.VMEM((A_HEADS, 1, MOBA_BLOCK), F32),
                        pltpu.VMEM((V_ROWS, MOBA_BLOCK), F32),
                        pltpu.VMEM((n_slots, MOBA_BLOCK, MOBA_BLOCK), F32),
                        pltpu.VMEM((n_slots, 1, MOBA_BLOCK), F32),
                        pltpu.VMEM((n_slots, 1, MOBA_BLOCK), F32),
                        pltpu.VMEM((N_PAST_BLOCKS, nrow, LANES), F32),
                        pltpu.VMEM((N_PAST_BLOCKS, nrow, LANES), F32),
                        pltpu.VMEM((N_PAST_BLOCKS, nrow, LANES), F32),
                        pltpu.VMEM((N_PAST_BLOCKS, nrow, w), F32)],
    )
    vmem = (nb * MOBA_BLOCK * (w + V_ROWS) * 2 + 2 * (w * MOBA_BLOCK * 2 + MOBA_BLOCK * w * 2)
            + n_slots * MOBA_BLOCK * MOBA_BLOCK * 4
            + 2 * 2 * SAMPLE_PAGES_PER_TILE * PAGE_SIZE * w * 4
            + N_PAST_BLOCKS * nrow * (3 * LANES + w) * 4 + 4 * MIB)
    return pl.pallas_call(
        _moba_kernel,
        grid_spec=grid_spec,
        out_shape=[jax.ShapeDtypeStruct((bsz * nb * MOBA_BLOCK, w), BF16),
                   jax.ShapeDtypeStruct((DEC_BATCH, DEC_SEQ, w), F32)],
        compiler_params=_params(("arbitrary", "arbitrary"), vmem / MIB + 6),
        name="moba",
    )(page_table, qt, kb, vt, q_s, k_new, v_new,
      *([cache_kt] * SAMPLE_PAGES_PER_TILE), *([cache_vt] * SAMPLE_PAGES_PER_TILE))


def _gmlp_kernel(u_ref, gv_ref, w_ref, bias_ref, avg_ref, lng_ref, lnb_ref, ob_ref, *vn_refs):
    chunks = [slice(c * B_CHUNK, (c + 1) * B_CHUNK) for c in range(u_ref.shape[0] // B_CHUNK)]
    avg = avg_ref[...]
    gvs = [_gelu_tanh(gv_ref[rows, :]) for rows in chunks]
    centred = [gv - _dot_exact_rhs(gv, avg) for gv in gvs]
    variances = [_dot_exact_rhs(gc * gc, avg, pieces=2) for gc in centred]
    pair_w = 2 * B_GD
    lane = lax.broadcasted_iota(jnp.int32, (B_CHUNK, pair_w), 1)
    for rows, gc, var in zip(chunks, centred, variances):
        vn = gc * lax.rsqrt(var + LN_EPS) * lng_ref[...] + lnb_ref[...]
        if vn_refs:
            vn_refs[0][rows, :] = vn
        vb = vn.astype(BF16)
        u = _gelu_tanh(u_ref[rows, :])
        for pr in range(B_GROUPS // 2):
            sl = slice(pr * pair_w, (pr + 1) * pair_w)
            vp = vb[:, sl]
            zero = jnp.zeros_like(vp)
            mixed = (_dot(w_ref[2 * pr], jnp.where(lane < B_GD, vp, zero))
                     + _dot(w_ref[2 * pr + 1], jnp.where(lane >= B_GD, vp, zero)))
            ob_ref[rows, sl] = (u[:, sl] * (mixed + bias_ref[:, sl])).astype(ob_ref.dtype)


def _gmlp(src, u_col, gv_col, w_masked, bias, avg, ln_g, ln_b, *, out_dtype, emit_vn):
    rows = src.shape[0]
    w = B_GROUPS * B_GD
    tile = B_CHUNK * min(GMLP_GROUP, rows // B_CHUNK)
    out_shape = [jax.ShapeDtypeStruct((rows, w), out_dtype)]
    out_specs = [pl.BlockSpec((tile, w), lambda c: (c, 0))]
    if emit_vn:
        out_shape.append(jax.ShapeDtypeStruct((rows, w), F32))
        out_specs.append(pl.BlockSpec((tile, w), lambda c: (c, 0)))
    return pl.pallas_call(
        _gmlp_kernel,
        grid=(rows // tile,),
        in_specs=[pl.BlockSpec((tile, w), lambda c: (c, u_col)),
                  pl.BlockSpec((tile, w), lambda c: (c, gv_col)),
                  _resident((B_GROUPS, B_CHUNK, B_CHUNK)), _resident((B_CHUNK, w)),
                  _resident((w, w)), _resident((1, w)), _resident((1, w))],
        out_specs=out_specs,
        out_shape=out_shape,
        compiler_params=_params(("parallel",), 24),
        name="gmlp_gate",
    )(src, src, w_masked, bias, avg, ln_g.reshape(1, w), ln_b.reshape(1, w))


def _pool_kernel(halo_ref, x_ref, w_ref, sc_ref, y_ref, *, rows, pos0, tiles_per_seq, fresh):
    t = pl.program_id(0) % tiles_per_seq
    halo = halo_ref[...]
    if fresh:
        halo = jnp.where(t == 0, 0.0, halo)
    x = x_ref[...]
    ext = jnp.concatenate([halo, x], axis=0)
    pos = pos0 + t * rows + lax.broadcasted_iota(jnp.int32, (rows, C_GD), 0)
    for gi, win in enumerate(POOL_WINDOWS):
        sl = slice(gi * C_GD, (gi + 1) * C_GD)
        s = ext[:, sl]
        sh = 1
        while sh < win:
            s = s + pltpu.roll(s, sh, 0)
            sh *= 2
        cnt = jnp.minimum(win, pos + 1).astype(F32)
        pooled = s[POOL_HALO:, :] / cnt - x[:, sl]
        y_ref[:, sl] = (_dot(pooled.astype(BF16), w_ref[gi]) * sc_ref[:, sl]).astype(y_ref.dtype)


def _pool(halo_src, x_src, w_bf16, scale, *, n_rows, rows, pos0, tiles_per_seq, fresh, out_dtype):
    w = len(POOL_WINDOWS) * C_GD
    if fresh:
        step = rows // POOL_HALO
        halo_map = lambda i: (jnp.maximum(i * step - 1, 0), 0)
    else:
        halo_map = lambda i: (i, 0)
    return pl.pallas_call(
        functools.partial(_pool_kernel, rows=rows, pos0=pos0, tiles_per_seq=tiles_per_seq,
                          fresh=fresh),
        grid=(n_rows // rows,),
        in_specs=[pl.BlockSpec((POOL_HALO, w), halo_map),
                  pl.BlockSpec((rows, w), lambda i: (i, 0)),
                  _resident((len(POOL_WINDOWS), C_GD, C_GD)), _resident((1, w))],
        out_specs=pl.BlockSpec((rows, w), lambda i: (i, 0)),
        out_shape=jax.ShapeDtypeStruct((n_rows, w), out_dtype),
        compiler_params=_params(("arbitrary",), 24),
        name="pool_mix",
    )(halo_src, x_src, w_bf16, scale.reshape(1, w))


def _hgrn_kernel(q_ref, f_ref, i_ref, g_ref, s0_ref, lb_ref, ng_ref, o_ref, sfin_ref, s_scr,
                 *, rows, in_rows, valid):
    t = pl.program_id(1)
    c_rows = HGRN_CHUNK

    @pl.when(t == 0)
    def _():
        s_scr[...] = s0_ref[...]

    r_io = lax.broadcasted_iota(jnp.int32, (c_rows, c_rows), 0)
    c_io = lax.broadcasted_iota(jnp.int32, (c_rows, c_rows), 1)
    causal = r_io >= c_io
    ltri = jnp.where(causal, 1.0, 0.0).astype(BF16)
    lgrp = jnp.where(c_io < (r_io // HGRN_SUB) * HGRN_SUB + HGRN_SUB // 2, 1.0, 0.0).astype(BF16)
    lsum = jnp.concatenate([ltri, lgrp], axis=0)
    eye = r_io == c_io
    row_id = lax.broadcasted_iota(jnp.int32, (c_rows, D_HK), 0)

    heads = [slice(hd * D_HK, (hd + 1) * D_HK) for hd in range(D_HEADS)]

    def load(ref, r0, sl):
        if in_rows == rows:
            return ref[pl.ds(r0, c_rows), sl]
        return jnp.concatenate([ref[:, sl], jnp.zeros((c_rows - in_rows, D_HK), F32)], axis=0)

    def decay_sums(r0):
        gates = []
        for sl in heads:
            lb = lb_ref[:, sl]
            half_span = 0.5 * (1.0 - lb)
            f = (lb + half_span) + half_span * jnp.tanh(0.5 * load(f_ref, r0, sl))
            logf = jnp.log2(f)
            kk = 1.0 - f
            if valid < rows:
                live = (t * rows + r0 + row_id) < valid
                logf = jnp.where(live, logf, 0.0)
                kk = jnp.where(live, kk, 0.0)
            sums = _dot_exact_lhs(lsum, logf)
            gates.append((kk, sums[:c_rows, :], sums[c_rows:, :]))
        return gates

    def chunk_scores(r0, gates):
        attns, queries = [], []
        for sl, (kk, cg, ref) in zip(heads, gates):
            q = _silu(load(q_ref, r0, sl))
            qd = (q * jnp.exp2(cg - ref)).astype(BF16)
            blocks = []
            for i in range(c_rows // HGRN_SUB):
                ref_i = ref[i * HGRN_SUB:i * HGRN_SUB + 1, :]
                e = jnp.where(row_id < (i + 1) * HGRN_SUB, ref_i - cg, 0.0)
                k_i = (kk * jnp.exp2(e)).astype(BF16)
                blocks.append(_dot_nt(qd[i * HGRN_SUB:(i + 1) * HGRN_SUB, :], k_i))
            attns.append(jnp.where(causal, jnp.concatenate(blocks, axis=0), 0.0).astype(BF16))
            queries.append((q * jnp.exp2(cg)).astype(BF16))
        return attns, queries

    def advance_state(r0, gates, attns, queries):
        outs = []
        for hd, (sl, (kk, cg, _)) in enumerate(zip(heads, gates)):
            vb = load(i_ref, r0, sl).astype(BF16)
            state = s_scr[hd]
            outs.append(_dot(queries[hd], state.astype(BF16)) + _dot(attns[hd], vb))
            g_last = cg[c_rows - 1:c_rows, :]
            kd = kk * jnp.exp2(g_last - cg)
            decay_col = jnp.sum(
                jnp.where(eye, jnp.broadcast_to(jnp.exp2(g_last), (c_rows, D_HK)), 0.0),
                axis=1, keepdims=True)
            s_scr[hd] = state * decay_col + _dot(kd.T.astype(BF16), vb)
        return outs

    def finish(r0, outs):
        for sl, o in zip(heads, outs):
            o = o * lax.rsqrt(jnp.mean(o * o, axis=-1, keepdims=True) + RMS_EPS) * ng_ref[...]
            o = (o * _silu(load(g_ref, r0, sl))).astype(o_ref.dtype)
            if in_rows == rows:
                o_ref[pl.ds(r0, c_rows), sl] = o
            else:
                o_ref[:, sl] = o[:in_rows, :]

    group = min(HGRN_GROUP, rows // c_rows)

    def chunk_group(c, carry):
        starts = [pl.multiple_of((c * group + k) * c_rows, c_rows) for k in range(group)]
        gates = [decay_sums(r0) for r0 in starts]
        outs = []
        for r0, gt in zip(starts, gates):
            attns, queries = chunk_scores(r0, gt)
            outs.append(advance_state(r0, gt, attns, queries))
        for r0, o in zip(starts, outs):
            finish(r0, o)
        return carry

    lax.fori_loop(0, rows // (c_rows * group), chunk_group, 0)

    @pl.when(t == pl.num_programs(1) - 1)
    def _():
        sfin_ref[...] = s_scr[...]


def _hgrn(src, s0, lb, norm_g, *, bsz, length, rows, in_rows, valid, out_dtype):
    w = D_HEADS * D_HK
    tiles = length // rows
    col = lambda k: pl.BlockSpec((in_rows, w), lambda b, t: (b * tiles + t, 1 + k))
    state_spec = pl.BlockSpec((None, D_HEADS, D_HK, D_HK), lambda b, t: (b, 0, 0, 0))
    return pl.pallas_call(
        functools.partial(_hgrn_kernel, rows=rows, in_rows=in_rows, valid=valid),
        grid=(bsz, tiles),
        in_specs=[col(0), col(1), col(2), col(3), state_spec, _resident((1, w)),
                  _resident((1, D_HK))],
        out_specs=[pl.BlockSpec((in_rows, w), lambda b, t: (b * tiles + t, 0)), state_spec],
        out_shape=[jax.ShapeDtypeStruct((bsz * tiles * in_rows, w), out_dtype),
                   jax.ShapeDtypeStruct((bsz, D_HEADS, D_HK, D_HK), F32)],
        scratch_shapes=[pltpu.VMEM((D_HEADS, D_HK, D_HK), F32)],
        compiler_params=_params(("parallel", "arbitrary"), 32),
        name="hgrn2",
    )(src, src, src, src, s0, lb.reshape(1, w), norm_g.reshape(1, D_HK))


def kernel(x_prompt, x_sample, cache_k, cache_v, state_pool, state_hgrn, page_table, w_in_even, w_out_even, gmlp_ws, gmlp_bs, gmlp_ln_g, gmlp_ln_b, w_in_odd, w_out_odd, pool_w, pool_scale, hgrn_lb_param, hgrn_norm_g, ln_mix_g, ln_mix_b, ln_ffn_g, ln_ffn_b, ffn_w1, ffn_w2):
    w = HALF_W
    xp = x_prompt.reshape(N_PROMPT, D_MODEL)
    xs = x_sample.reshape(N_SAMPLE, D_MODEL)
    pad_tokens = ((0, 0), (0, SUBLANES - DEC_SEQ), (0, 0))

    col_scale = jnp.where(jnp.arange(EVEN_IN) < w, SCORE_SCALE, 1.0)
    w_even = (w_in_even[0] * col_scale).astype(BF16)
    w_k_u_gv = jnp.concatenate([w_even[:, w:2 * w], w_even[:, 3 * w:]], axis=1)
    kb, ugv, qt, vt, kt32, vt32 = _proj_even_prompt(x_prompt, w_k_u_gv, w_even[:, :3 * w].T)
    h0s = _proj(xs, w_even, SAMPLE_TILE)
    hs = h0s.reshape(DEC_BATCH, DEC_SEQ, EVEN_IN)
    oa_prompt, oa_sample = _moba(page_table, qt, kb, vt, hs[..., :w],
                                 jnp.pad(hs[..., w:2 * w], pad_tokens), jnp.pad(hs[..., 2 * w:3 * w], pad_tokens),
                                 cache_k[0].transpose(0, 2, 3, 1), cache_v[0].transpose(0, 2, 3, 1))

    tril = jnp.tril(jnp.ones((B_CHUNK, B_CHUNK), bool))
    ws_prompt = jnp.where(tril[None], gmlp_ws[0], 0.0)
    ws_sample = jax.vmap(lambda m: jnp.kron(jnp.eye(DEC_BATCH, dtype=F32), m[:DEC_SEQ, :DEC_SEQ]))(ws_prompt)
    bias_prompt = jnp.repeat(gmlp_bs[0].T, B_GD, axis=1)
    bias_sample = jnp.tile(bias_prompt[:DEC_SEQ], (DEC_BATCH, 1))
    grp = jnp.arange(w) // B_GD
    avg = jnp.where(grp[:, None] == grp[None, :], 1.0 / B_GD, 0.0).astype(BF16)
    ob_prompt, = _gmlp(ugv, 0, 1, ws_prompt.astype(BF16), bias_prompt, avg, gmlp_ln_g[0], gmlp_ln_b[0],
                       out_dtype=BF16, emit_vn=False)
    ob_sample, vn_sample = _gmlp(h0s, 3, 4, ws_sample.astype(BF16), bias_sample, avg, gmlp_ln_g[0],
                                 gmlp_ln_b[0], out_dtype=F32, emit_vn=True)

    layer0 = (w_out_even[0].astype(BF16), ln_mix_g[0], ln_mix_b[0],
              ffn_w1[0].astype(BF16), ffn_w2[0].astype(BF16), ln_ffn_g[0], ln_ffn_b[0])
    xp = _mix_ffn(oa_prompt, ob_prompt, xp, *layer0, PROMPT_TILE)
    xs = _mix_ffn(oa_sample.reshape(N_SAMPLE, w), ob_sample, xs, *layer0, SAMPLE_TILE)

    per_head = lambda t: t.reshape(BATCH, A_HEADS, A_HD, SEQ).transpose(0, 3, 1, 2)[None]
    new_k_prompt = per_head(kt32)
    new_v_prompt = per_head(vt32)
    new_k_sample = hs[..., w:2 * w].reshape(1, DEC_BATCH, DEC_SEQ, A_HEADS, A_HD)
    new_v_sample = hs[..., 2 * w:3 * w].reshape(1, DEC_BATCH, DEC_SEQ, A_HEADS, A_HD)
    new_gmlp_v_sample = vn_sample.reshape(1, DEC_BATCH, DEC_SEQ, w)

    w_odd = w_in_odd[0].astype(BF16)
    lb = jax.nn.softmax(hgrn_lb_param.astype(F32), axis=0)[0]
    pool_wb = pool_w[0].astype(BF16)

    h1p = _proj(xp, w_odd, PROMPT_TILE)
    oc_prompt = _pool(h1p, h1p, pool_wb, pool_scale[0], n_rows=N_PROMPT, rows=POOL_ROWS, pos0=0,
                      tiles_per_seq=SEQ // POOL_ROWS, fresh=True, out_dtype=BF16)
    od_prompt, s_prompt = _hgrn(h1p, jnp.zeros((BATCH, D_HEADS, D_HK, D_HK), F32), lb, hgrn_norm_g[0],
                                bsz=BATCH, length=SEQ, rows=HGRN_ROWS, in_rows=HGRN_ROWS, valid=SEQ,
                                out_dtype=BF16)

    h1s = _proj(xs, w_odd, SAMPLE_TILE).reshape(DEC_BATCH, DEC_SEQ, ODD_IN)
    h1s_pad = jnp.pad(h1s, pad_tokens).reshape(DEC_BATCH * SUBLANES, ODD_IN)
    halo_s = jnp.pad(state_pool[0], ((0, 0), (POOL_HALO - POOL_BUF, 0), (0, 0)))
    oc_sample = _pool(halo_s.reshape(DEC_BATCH * POOL_HALO, w), h1s_pad, pool_wb, pool_scale[0],
                      n_rows=DEC_BATCH * SUBLANES, rows=SUBLANES, pos0=PAST_LEN, tiles_per_seq=1,
                      fresh=False, out_dtype=F32)
    od_sample, s_sample = _hgrn(h1s_pad, state_hgrn[0], lb, hgrn_norm_g[0], bsz=DEC_BATCH,
                                length=HGRN_CHUNK, rows=HGRN_CHUNK, in_rows=SUBLANES, valid=DEC_SEQ,
                                out_dtype=F32)
    real_rows = lambda t: t.reshape(DEC_BATCH, SUBLANES, w)[:, :DEC_SEQ].reshape(N_SAMPLE, w)

    layer1 = (w_out_odd[0].astype(BF16), ln_mix_g[1], ln_mix_b[1],
              ffn_w1[1].astype(BF16), ffn_w2[1].astype(BF16), ln_ffn_g[1], ln_ffn_b[1])
    xp = _mix_ffn(oc_prompt, od_prompt, xp, *layer1, PROMPT_TILE)
    xs = _mix_ffn(real_rows(oc_sample), real_rows(od_sample), xs, *layer1, SAMPLE_TILE)

    new_pool_prompt = h1p.reshape(BATCH, SEQ, ODD_IN)[:, SEQ - POOL_BUF:, :w][None]
    new_pool_sample = jnp.concatenate([state_pool[0][:, DEC_SEQ:], h1s[..., :w]], axis=1)[None]
    return (xp.reshape(BATCH, SEQ, D_MODEL), xs.reshape(DEC_BATCH, DEC_SEQ, D_MODEL),
            new_k_prompt, new_v_prompt, new_k_sample, new_v_sample, new_gmlp_v_sample,
            new_pool_prompt, new_pool_sample, s_prompt[None], s_sample[None])
```

```python
import functools
import math

import jax
import jax.numpy as jnp
from jax import lax
from jax.experimental import pallas as pl
from jax.experimental.pallas import tpu as pltpu

F32 = jnp.float32
BF16 = jnp.bfloat16

D_MODEL = 1024
BATCH = 4
SEQ = 4096
DEPTH = 2
DEC_BATCH = 32
DEC_SEQ = 4
PAST_LEN = 8192
PAGE_SIZE = 128
HALF_W = D_MODEL // 2
A_HD = 64
A_HEADS = HALF_W // A_HD
MOBA_BLOCK = 256
MOBA_TOPK = 3
B_GROUPS = 8
B_GD = HALF_W // B_GROUPS
B_CHUNK = 128
POOL_WINDOWS = (2, 4, 8, 16)
C_GD = HALF_W // len(POOL_WINDOWS)
POOL_BUF = max(POOL_WINDOWS) - 1
D_HK = 128
D_HEADS = HALF_W // D_HK
D_FF = 4 * D_MODEL
EVEN_IN = 5 * HALF_W
ODD_IN = 5 * HALF_W
ALPHA = (2 * DEPTH) ** 0.25
LN_EPS = 1e-5
RMS_EPS = 1e-6
NEG = -1e30

N_PROMPT = BATCH * SEQ
N_SAMPLE = DEC_BATCH * DEC_SEQ
N_PAGES = PAST_LEN // PAGE_SIZE
N_PAST_BLOCKS = PAST_LEN // MOBA_BLOCK
PAGES_PER_BLOCK = MOBA_BLOCK // PAGE_SIZE
N_KEY_BLOCKS = SEQ // MOBA_BLOCK

LANES = 128
SUBLANES = 8
PROMPT_TILE = 512
SAMPLE_TILE = N_SAMPLE
FF_CHUNK = 1024
FFN_ROW_GROUPS = 2
MOBA_LOOKAHEAD = 7
SAMPLE_PAGES_PER_TILE = DEC_BATCH * N_PAGES // (BATCH * N_KEY_BLOCKS)
TILES_PER_SAMPLE = N_PAGES // SAMPLE_PAGES_PER_TILE
assert SAMPLE_PAGES_PER_TILE * BATCH * N_KEY_BLOCKS == DEC_BATCH * N_PAGES
assert TILES_PER_SAMPLE * SAMPLE_PAGES_PER_TILE == N_PAGES and N_KEY_BLOCKS % TILES_PER_SAMPLE == 0
assert SAMPLE_PAGES_PER_TILE % PAGES_PER_BLOCK == 0
V_TAIL = 16
V_SLAB = A_HD + V_TAIL
V_ROWS = A_HEADS * V_SLAB
assert A_HEADS % (MOBA_LOOKAHEAD + 1) == 0
SCORE_SCALE = (A_HD ** -0.5) * math.log2(math.e)
HGRN_CHUNK = 128
HGRN_SUB = 32
HGRN_ROWS = 512
HGRN_GROUP = 2
GMLP_GROUP = 4
POOL_ROWS = 512
POOL_HALO = 16
MIB = 1024 * 1024


def _dot(a, b):
    return jnp.dot(a, b, preferred_element_type=F32)


def _dot_nt(a, b):
    return lax.dot_general(a, b, (((1,), (1,)), ((), ())), preferred_element_type=F32)


def _split3(x):
    p0 = x.astype(BF16)
    r1 = x - p0.astype(F32)
    p1 = r1.astype(BF16)
    p2 = (r1 - p1.astype(F32)).astype(BF16)
    return p0, p1, p2


def _dot_exact_lhs(m_bf16, x):
    p0, p1, p2 = _split3(x)
    return _dot(m_bf16, p0) + _dot(m_bf16, p1) + _dot(m_bf16, p2)


def _dot_exact_rhs(x, m_bf16, pieces=3):
    parts = _split3(x)[:pieces]
    out = _dot(parts[0], m_bf16)
    for part in parts[1:]:
        out = out + _dot(part, m_bf16)
    return out


def _gelu_tanh(x):
    return 0.5 * x * (1.0 + jnp.tanh(0.7978845608028654 * (x + 0.044715 * (x * x * x))))


def _silu(x):
    h = 0.5 * x
    return h + h * jnp.tanh(h)


def _layer_norm(z, g, b):
    mu = jnp.mean(z, axis=-1, keepdims=True)
    zc = z - mu
    var = jnp.mean(zc * zc, axis=-1, keepdims=True)
    return zc * lax.rsqrt(var + LN_EPS) * g + b


def _params(sem, vmem_mib):
    return pltpu.CompilerParams(dimension_semantics=sem, vmem_limit_bytes=int(vmem_mib * MIB))


def _resident(shape):
    return pl.BlockSpec(shape, lambda *_: (0,) * len(shape), pipeline_mode=pl.Buffered(1))


def _proj_kernel(x_ref, w_ref, h_ref):
    xb = x_ref[...].astype(BF16)
    for c in range(w_ref.shape[1] // HALF_W):
        sl = slice(c * HALF_W, (c + 1) * HALF_W)
        h_ref[:, sl] = _dot(xb, w_ref[:, sl])


def _proj(x, w_bf16, tile):
    n, d = x.shape
    n_out = w_bf16.shape[1]
    vmem = 2 * (tile * d * 4 + tile * n_out * 4) + d * n_out * 2
    return pl.pallas_call(
        _proj_kernel,
        grid=(n // tile,),
        in_specs=[pl.BlockSpec((tile, d), lambda i: (i, 0)), _resident((d, n_out))],
        out_specs=pl.BlockSpec((tile, n_out), lambda i: (i, 0)),
        out_shape=jax.ShapeDtypeStruct((n, n_out), F32),
        compiler_params=_params(("parallel",), vmem / MIB + 8),
        name="in_proj",
    )(x, w_bf16)


def _proj_even_kernel(x_ref, wn_ref, wt_ref, kb_ref, ugv_ref, qt_ref, vt_ref, kt32_ref, vt32_ref):
    w = HALF_W
    xb = x_ref[...].astype(BF16)
    k = _dot(xb, wn_ref[:, :w])
    for half in range(PROMPT_TILE // MOBA_BLOCK):
        kb_ref[half] = k[half * MOBA_BLOCK:(half + 1) * MOBA_BLOCK, :].astype(BF16)
    ugv_ref[:, :w] = _dot(xb, wn_ref[:, w:2 * w])
    ugv_ref[:, w:] = _dot(xb, wn_ref[:, 2 * w:])
    qt = _dot_nt(wt_ref[:w, :], xb)
    kt32_ref[...] = _dot_nt(wt_ref[w:2 * w, :], xb)
    vt = _dot_nt(wt_ref[2 * w:, :], xb)
    vt32_ref[...] = vt
    ones_row = jnp.where(lax.broadcasted_iota(jnp.int32, (V_TAIL, MOBA_BLOCK), 0) == 0,
                         1.0, 0.0).astype(BF16)
    for half in range(PROMPT_TILE // MOBA_BLOCK):
        cols = slice(half * MOBA_BLOCK, (half + 1) * MOBA_BLOCK)
        qt_ref[half] = qt[:, cols].astype(BF16)
        for h in range(A_HEADS):
            vt_ref[half, h * V_SLAB:h * V_SLAB + A_HD, :] = vt[h * A_HD:(h + 1) * A_HD, cols].astype(BF16)
            vt_ref[half, h * V_SLAB + A_HD:(h + 1) * V_SLAB, :] = ones_row


def _proj_even_prompt(x_prompt, w_nat, w_tr):
    w = HALF_W
    tiles = SEQ // PROMPT_TILE
    per_tile = PROMPT_TILE // MOBA_BLOCK
    blocked = lambda shape: pl.BlockSpec((None, per_tile) + shape, lambda b, t: (b, t, 0, 0))
    vmem = (2 * (PROMPT_TILE * D_MODEL * 4 + PROMPT_TILE * 2 * w * 4 + 3 * PROMPT_TILE * w * 2
                 + 2 * PROMPT_TILE * w * 4) + 2 * D_MODEL * 3 * w * 2 + 4 * PROMPT_TILE * w * 4)
    return pl.pallas_call(
        _proj_even_kernel,
        grid=(BATCH, tiles),
        in_specs=[pl.BlockSpec((None, PROMPT_TILE, D_MODEL), lambda b, t: (b, t, 0)),
                  _resident((D_MODEL, 3 * w)), _resident((3 * w, D_MODEL))],
        out_specs=[blocked((MOBA_BLOCK, w)),
                   pl.BlockSpec((PROMPT_TILE, 2 * w), lambda b, t: (b * tiles + t, 0)),
                   blocked((w, MOBA_BLOCK)), blocked((V_ROWS, MOBA_BLOCK)),
                   pl.BlockSpec((None, w, PROMPT_TILE), lambda b, t: (b, 0, t)),
                   pl.BlockSpec((None, w, PROMPT_TILE), lambda b, t: (b, 0, t))],
        out_shape=[jax.ShapeDtypeStruct((BATCH, N_KEY_BLOCKS, MOBA_BLOCK, w), BF16),
                   jax.ShapeDtypeStruct((N_PROMPT, 2 * w), F32),
                   jax.ShapeDtypeStruct((BATCH, N_KEY_BLOCKS, w, MOBA_BLOCK), BF16),
                   jax.ShapeDtypeStruct((BATCH, N_KEY_BLOCKS, V_ROWS, MOBA_BLOCK), BF16),
                   jax.ShapeDtypeStruct((BATCH, w, SEQ), F32),
                   jax.ShapeDtypeStruct((BATCH, w, SEQ), F32)],
        compiler_params=_params(("parallel", "parallel"), vmem / MIB + 8),
        name="in_proj_even",
    )(x_prompt, w_nat, w_tr)


def _mix_ffn_kernel(oa_ref, ob_ref, x_ref, wo_ref, g1_ref, b1_ref, w1_ref, w2_ref, g2_ref, b2_ref,
                    y_ref):
    tile = x_ref.shape[0]
    groups = [slice(r, r + tile // FFN_ROW_GROUPS) for r in range(0, tile, tile // FFN_ROW_GROUPS)]
    mixed = [_dot(oa_ref[rows, :].astype(BF16), wo_ref[:HALF_W, :])
             + _dot(ob_ref[rows, :].astype(BF16), wo_ref[HALF_W:, :]) for rows in groups]
    outs = []
    for rows, mix in zip(groups, mixed):
        x = _layer_norm(ALPHA * x_ref[rows, :] + mix, g1_ref[...], b1_ref[...])
        xb = x.astype(BF16)
        acc = jnp.zeros(x.shape, F32)
        for c in range(D_FF // FF_CHUNK):
            sl = slice(c * FF_CHUNK, (c + 1) * FF_CHUNK)
            hid = jnp.maximum(_dot(xb, w1_ref[:, sl]), 0.0)
            acc = acc + _dot((hid * hid).astype(BF16), w2_ref[sl, :])
        outs.append(ALPHA * x + acc)
    for rows, z in zip(groups, outs):
        y_ref[rows, :] = _layer_norm(z, g2_ref[...], b2_ref[...])


def _mix_ffn(oa, ob, x, wo_bf16, g1, b1, w1_bf16, w2_bf16, g2, b2, tile):
    n, d = x.shape
    row = lambda i: (i, 0)
    vec = lambda v: v.reshape(1, d)
    vmem = (2 * (2 * tile * HALF_W * oa.dtype.itemsize + 2 * tile * d * 4)
            + (d * d + 2 * d * D_FF) * 2 + 8 * tile * FF_CHUNK * 4)
    return pl.pallas_call(
        _mix_ffn_kernel,
        grid=(n // tile,),
        in_specs=[pl.BlockSpec((tile, HALF_W), row), pl.BlockSpec((tile, HALF_W), row),
                  pl.BlockSpec((tile, d), row), _resident((d, d)),
                  _resident((1, d)), _resident((1, d)),
                  _resident((d, D_FF)), _resident((D_FF, d)),
                  _resident((1, d)), _resident((1, d))],
        out_specs=pl.BlockSpec((tile, d), row),
        out_shape=jax.ShapeDtypeStruct((n, d), F32),
        compiler_params=_params(("parallel",), vmem / MIB + 8),
        name="mix_ffn_ln",
    )(oa, ob, x, wo_bf16, vec(g1), vec(b1), w1_bf16, w2_bf16, vec(g2), vec(b2))


def _moba_kernel(pt_ref, qt_ref, k_ref, vt_ref, qs_ref, kn_ref, vn_ref, *refs):
    del pt_ref
    kp_refs = refs[:SAMPLE_PAGES_PER_TILE]
    vp_refs = refs[SAMPLE_PAGES_PER_TILE:2 * SAMPLE_PAGES_PER_TILE]
    (o_ref, os_ref, kmt_scr, qtz_scr, selb_scr, m_scr, acc_scr, s_scr, ref_scr, a_scr,
     sm_scr, sl_scr, sgate_scr, sacc_scr) = refs[2 * SAMPLE_PAGES_PER_TILE:]
    j = pl.program_id(1)
    nb = k_ref.shape[0]
    blk = MOBA_BLOCK
    pair_w = 2 * A_HD
    w = A_HEADS * A_HD
    nrow = DEC_SEQ * A_HEADS
    part = (pl.program_id(0) * nb + j) % TILES_PER_SAMPLE
    wide = lambda c: jnp.broadcast_to(c, (nrow, LANES))

    @pl.when(j == 0)
    def _():
        rows = [jnp.sum(k_ref[n].astype(F32), axis=0, keepdims=True) * (1.0 / blk)
                for n in range(nb)]
        kmean = jnp.concatenate(rows, axis=0)
        tiled = jnp.concatenate([kmean] * A_HEADS, axis=0)
        rh = lax.broadcasted_iota(jnp.int32, tiled.shape, 0) // nb
        ch = lax.broadcasted_iota(jnp.int32, tiled.shape, 1) // A_HD
        kmt_scr[...] = jnp.where(rh == ch, tiled, 0.0)

    head_mask = (lax.broadcasted_iota(jnp.int32, (A_HEADS, w), 1) // A_HD
                 == lax.broadcasted_iota(jnp.int32, (A_HEADS, w), 0))
    qs = qs_ref[...]
    qbd = jnp.concatenate(
        [jnp.where(head_mask, jnp.broadcast_to(qs[i:i + 1, :], (A_HEADS, w)), 0.0)
         for i in range(DEC_SEQ)], axis=0).astype(BF16)
    page_scores = [_dot(qbd, kp_refs[i][...].reshape(w, PAGE_SIZE).astype(BF16))
                   for i in range(SAMPLE_PAGES_PER_TILE)]

    def sample_blocks():
        probs = []
        for bl in range(SAMPLE_PAGES_PER_TILE // PAGES_PER_BLOCK):
            s = jnp.concatenate(page_scores[bl * PAGES_PER_BLOCK:(bl + 1) * PAGES_PER_BLOCK], axis=1)
            blk_id = part * (SAMPLE_PAGES_PER_TILE // PAGES_PER_BLOCK) + bl
            m = jnp.max(s, axis=-1, keepdims=True)
            p = jnp.exp2(s - m)
            sm_scr[blk_id] = wide(m)
            sl_scr[blk_id] = wide(jnp.sum(p, axis=-1, keepdims=True))
            sgate_scr[blk_id] = wide(jnp.sum(s, axis=-1, keepdims=True) * (1.0 / MOBA_BLOCK))
            probs.append(p.astype(BF16))
        for bl, p in enumerate(probs):
            blk_id = part * (SAMPLE_PAGES_PER_TILE // PAGES_PER_BLOCK) + bl
            acc = None
            for r in range(PAGES_PER_BLOCK):
                vpage = vp_refs[bl * PAGES_PER_BLOCK + r][...].reshape(w, PAGE_SIZE).astype(BF16)
                pv = _dot_nt(p[:, r * PAGE_SIZE:(r + 1) * PAGE_SIZE], vpage)
                acc = pv if acc is None else acc + pv
            sacc_scr[blk_id] = acc

    qts = qt_ref[...]
    k0, k1, k2 = _split3(kmt_scr[...])
    gate = _dot(k0, qts) + _dot(k1, qts) + _dot(k2, qts)

    n_io = lax.broadcasted_iota(jnp.int32, (nb, blk), 0)
    half = lax.broadcasted_iota(jnp.int32, (pair_w, blk), 0) // A_HD
    for h in range(A_HEADS):
        g = jnp.where(n_io < j, gate[h * nb:(h + 1) * nb, :], NEG)
        bias = jnp.full((nb, blk), NEG, F32)
        for _ in range(MOBA_TOPK):
            mx = jnp.max(g, axis=0, keepdims=True)
            cand = jnp.where((g == mx) & (mx > 0.5 * NEG), n_io, nb)
            pick = n_io == jnp.min(cand, axis=0, keepdims=True)
            bias = jnp.where(pick, 0.0, bias)
            g = jnp.where(pick, NEG, g)
        selb_scr[h * nb:(h + 1) * nb, :] = bias
        pr = h // 2
        qpair = qts[pr * pair_w:(pr + 1) * pair_w, :]
        qtz_scr[h] = jnp.where(half == (h % 2), qpair, jnp.zeros_like(qpair))

    causal = (lax.broadcasted_iota(jnp.int32, (blk, blk), 0)
              <= lax.broadcasted_iota(jnp.int32, (blk, blk), 1))

    def fold_rows(x, op):
        return op(x.reshape(blk // SUBLANES, SUBLANES, blk), axis=0)

    def stage_scores(n, h, slot, own):
        pr = h // 2
        s = _dot(k_ref[n, :, pr * pair_w:(pr + 1) * pair_w], qtz_scr[h])
        if own:
            s = jnp.where(causal, s, NEG)
        s_scr[slot] = s
        col_max = jnp.max(fold_rows(s, jnp.max), axis=0, keepdims=True)
        if own:
            m_scr[h] = col_max
            ref_scr[slot] = col_max
        else:
            bias = selb_scr[pl.ds(h * nb + n, 1), :]
            m_old = m_scr[h]
            m_new = jnp.maximum(m_old, col_max + bias)
            m_scr[h] = m_new
            a_scr[slot] = jnp.exp2(m_old - m_new)
            ref_scr[slot] = m_new - bias

    def accumulate(n, h, slot, own):
        vs = slice(h * V_SLAB, (h + 1) * V_SLAB)
        p = jnp.exp2(s_scr[slot] - ref_scr[slot])
        pv = _dot(vt_ref[n, vs, :], p.astype(BF16))
        if own:
            acc_scr[vs, :] = pv
        else:
            acc_scr[vs, :] = a_scr[slot] * acc_scr[vs, :] + pv

    n_slots = MOBA_LOOKAHEAD + 1

    def stage_ahead(n, n_next, h, own):
        ha = h + MOBA_LOOKAHEAD
        if ha < A_HEADS:
            stage_scores(n, ha, ha % n_slots, own=own)
        else:
            stage_scores(n_next, ha - A_HEADS, ha % n_slots, own=False)

    for h in range(MOBA_LOOKAHEAD):
        stage_scores(j, h, h % n_slots, own=True)
    for h in range(A_HEADS):
        stage_ahead(j, 0, h, own=True)
        accumulate(j, h, h % n_slots, own=True)
    sample_blocks()

    def past_block(n, carry):
        for h in range(A_HEADS):
            stage_ahead(n, n + 1, h, own=False)
            accumulate(n, h, h % n_slots, own=False)
        return carry

    lax.fori_loop(0, j, past_block, 0)

    heads_out = []
    for h in range(A_HEADS):
        slab = acc_scr[h * V_SLAB:(h + 1) * V_SLAB, :]
        heads_out.append(slab[:A_HD, :] / slab[A_HD:A_HD + 1, :])
    o_ref[...] = jnp.concatenate(heads_out, axis=0).T.astype(o_ref.dtype)

    @pl.when(part == TILES_PER_SAMPLE - 1)
    def _():
        tile4 = lambda c: jnp.concatenate([c] * (w // LANES), axis=1)
        s = _dot_nt(qbd, kn_ref[...].astype(BF16))
        col = lax.broadcasted_iota(jnp.int32, s.shape, 1)
        qi = lax.broadcasted_iota(jnp.int32, s.shape, 0) // A_HEADS
        s = jnp.where(col <= qi, s, NEG)
        m_col = jnp.max(s, axis=-1, keepdims=True)
        p = jnp.exp2(s - m_col)
        m_own = wide(m_col)
        l_own = wide(jnp.sum(p, axis=-1, keepdims=True))
        o_own = _dot(p, vn_ref[...])

        for _ in range(MOBA_TOPK):
            best = jnp.full((nrow, LANES), NEG, F32)
            bidx = jnp.zeros((nrow, LANES), jnp.int32)
            for n in range(N_PAST_BLOCKS):
                gn = sgate_scr[n]
                upd = gn > best
                best = jnp.where(upd, gn, best)
                bidx = jnp.where(upd, n, bidx)
            for n in range(N_PAST_BLOCKS):
                sgate_scr[n] = jnp.where(bidx == n, -jnp.inf, sgate_scr[n])

        m_all = m_own
        for n in range(N_PAST_BLOCKS):
            m_all = jnp.maximum(m_all, jnp.where(sgate_scr[n] == -jnp.inf, sm_scr[n], NEG))
        w_own = jnp.exp2(m_own - m_all)
        l_all = w_own * l_own
        o_all = tile4(w_own) * o_own
        for n in range(N_PAST_BLOCKS):
            wn = jnp.where(sgate_scr[n] == -jnp.inf, jnp.exp2(sm_scr[n] - m_all), 0.0)
            l_all = l_all + wn * sl_scr[n]
            o_all = o_all + tile4(wn) * sacc_scr[n]
        out = o_all / tile4(l_all)
        for i in range(DEC_SEQ):
            rows = out[i * A_HEADS:(i + 1) * A_HEADS, :]
            os_ref[i:i + 1, :] = jnp.sum(jnp.where(head_mask, rows, 0.0), axis=0, keepdims=True)


def _moba(page_table, qt, kb, vt, q_s, k_new, v_new, cache_kt, cache_vt):
    bsz, nb = qt.shape[:2]
    w = A_HEADS * A_HD
    nrow = DEC_SEQ * A_HEADS
    n_slots = MOBA_LOOKAHEAD + 1
    step = lambda b, j: b * nb + j
    sample_of = lambda b, j: step(b, j) // TILES_PER_SAMPLE
    first_page = lambda b, j: (step(b, j) % TILES_PER_SAMPLE) * SAMPLE_PAGES_PER_TILE
    page_spec = lambda i: pl.BlockSpec(
        (None, A_HEADS, A_HD, PAGE_SIZE),
        lambda b, j, pt: (pt[sample_of(b, j), first_page(b, j) + i], 0, 0, 0))
    per_sample = lambda rows: pl.BlockSpec((None, rows, w), lambda b, j, pt: (sample_of(b, j), 0, 0))
    whole_row = lambda shape: pl.BlockSpec((None,) + shape, lambda b, j, pt: (b, 0, 0, 0),
                                           pipeline_mode=pl.Buffered(1))
    grid_spec = pltpu.PrefetchScalarGridSpec(
        num_scalar_prefetch=1,
        grid=(bsz, nb),
        in_specs=([pl.BlockSpec((None, None, w, MOBA_BLOCK), lambda b, j, pt: (b, j, 0, 0)),
                   whole_row((nb, MOBA_BLOCK, w)), whole_row((nb, V_ROWS, MOBA_BLOCK)),
                   per_sample(DEC_SEQ), per_sample(SUBLANES), per_sample(SUBLANES)]
                  + [page_spec(i) for i in range(SAMPLE_PAGES_PER_TILE)]
                  + [page_spec(i) for i in range(SAMPLE_PAGES_PER_TILE)]),
        out_specs=[pl.BlockSpec((MOBA_BLOCK, w), lambda b, j, pt: (step(b, j), 0)),
                   per_sample(DEC_SEQ)],
        scratch_shapes=[pltpu.VMEM((A_HEADS * nb, w), F32),
                        pltpu.VMEM((A_HEADS, 2 * A_HD, MOBA_BLOCK), BF16),
                        pltpu.VMEM((A_HEADS * nb, MOBA_BLOCK), F32),
                        pltpu.VMEM((A_HEADS, 1, MOBA_BLOCK), F32),
                        pltpu.VMEM((V_ROWS, MOBA_BLOCK), F32),
                        pltpu.VMEM((n_slots, MOBA_BLOCK, MOBA_BLOCK), F32),
                        pltpu.VMEM((n_slots, 1, MOBA_BLOCK), F32),
                        pltpu.VMEM((n_slots, 1, MOBA_BLOCK), F32),
                        pltpu.VMEM((N_PAST_BLOCKS, nrow, LANES), F32),
                        pltpu.VMEM((N_PAST_BLOCKS, nrow, LANES), F32),
                        pltpu.VMEM((N_PAST_BLOCKS, nrow, LANES), F32),
                        pltpu.VMEM((N_PAST_BLOCKS, nrow, w), F32)],
    )
    vmem = (nb * MOBA_BLOCK * (w + V_ROWS) * 2 + 2 * (w * MOBA_BLOCK * 2 + MOBA_BLOCK * w * 2)
            + n_slots * MOBA_BLOCK * MOBA_BLOCK * 4
            + 2 * 2 * SAMPLE_PAGES_PER_TILE * PAGE_SIZE * w * 4
            + N_PAST_BLOCKS * nrow * (3 * LANES + w) * 4 + 4 * MIB)
    return pl.pallas_call(
        _moba_kernel,
        grid_spec=grid_spec,
        out_shape=[jax.ShapeDtypeStruct((bsz * nb * MOBA_BLOCK, w), BF16),
                   jax.ShapeDtypeStruct((DEC_BATCH, DEC_SEQ, w), F32)],
        compiler_params=_params(("arbitrary", "arbitrary"), vmem / MIB + 6),
        name="moba",
    )(page_table, qt, kb, vt, q_s, k_new, v_new,
      *([cache_kt] * SAMPLE_PAGES_PER_TILE), *([cache_vt] * SAMPLE_PAGES_PER_TILE))


def _gmlp_kernel(u_ref, gv_ref, w_ref, bias_ref, avg_ref, lng_ref, lnb_ref, ob_ref, *vn_refs):
    chunks = [slice(c * B_CHUNK, (c + 1) * B_CHUNK) for c in range(u_ref.shape[0] // B_CHUNK)]
    avg = avg_ref[...]
    gvs = [_gelu_tanh(gv_ref[rows, :]) for rows in chunks]
    centred = [gv - _dot_exact_rhs(gv, avg) for gv in gvs]
    variances = [_dot_exact_rhs(gc * gc, avg, pieces=2) for gc in centred]
    pair_w = 2 * B_GD
    lane = lax.broadcasted_iota(jnp.int32, (B_CHUNK, pair_w), 1)
    for rows, gc, var in zip(chunks, centred, variances):
        vn = gc * lax.rsqrt(var + LN_EPS) * lng_ref[...] + lnb_ref[...]
        if vn_refs:
            vn_refs[0][rows, :] = vn
        vb = vn.astype(BF16)
        u = _gelu_tanh(u_ref[rows, :])
        for pr in range(B_GROUPS // 2):
            sl = slice(pr * pair_w, (pr + 1) * pair_w)
            vp = vb[:, sl]
            zero = jnp.zeros_like(vp)
            mixed = (_dot(w_ref[2 * pr], jnp.where(lane < B_GD, vp, zero))
                     + _dot(w_ref[2 * pr + 1], jnp.where(lane >= B_GD, vp, zero)))
            ob_ref[rows, sl] = (u[:, sl] * (mixed + bias_ref[:, sl])).astype(ob_ref.dtype)


def _gmlp(src, u_col, gv_col, w_masked, bias, avg, ln_g, ln_b, *, out_dtype, emit_vn):
    rows = src.shape[0]
    w = B_GROUPS * B_GD
    tile = B_CHUNK * min(GMLP_GROUP, rows // B_CHUNK)
    out_shape = [jax.ShapeDtypeStruct((rows, w), out_dtype)]
    out_specs = [pl.BlockSpec((tile, w), lambda c: (c, 0))]
    if emit_vn:
        out_shape.append(jax.ShapeDtypeStruct((rows, w), F32))
        out_specs.append(pl.BlockSpec((tile, w), lambda c: (c, 0)))
    return pl.pallas_call(
        _gmlp_kernel,
        grid=(rows // tile,),
        in_specs=[pl.BlockSpec((tile, w), lambda c: (c, u_col)),
                  pl.BlockSpec((tile, w), lambda c: (c, gv_col)),
                  _resident((B_GROUPS, B_CHUNK, B_CHUNK)), _resident((B_CHUNK, w)),
                  _resident((w, w)), _resident((1, w)), _resident((1, w))],
        out_specs=out_specs,
        out_shape=out_shape,
        compiler_params=_params(("parallel",), 24),
        name="gmlp_gate",
    )(src, src, w_masked, bias, avg, ln_g.reshape(1, w), ln_b.reshape(1, w))


def _pool_kernel(halo_ref, x_ref, w_ref, sc_ref, y_ref, *, rows, pos0, tiles_per_seq, fresh):
    t = pl.program_id(0) % tiles_per_seq
    halo = halo_ref[...]
    if fresh:
        halo = jnp.where(t == 0, 0.0, halo)
    x = x_ref[...]
    ext = jnp.concatenate([halo, x], axis=0)
    pos = pos0 + t * rows + lax.broadcasted_iota(jnp.int32, (rows, C_GD), 0)
    for gi, win in enumerate(POOL_WINDOWS):
        sl = slice(gi * C_GD, (gi + 1) * C_GD)
        s = ext[:, sl]
        sh = 1
        while sh < win:
            s = s + pltpu.roll(s, sh, 0)
            sh *= 2
        cnt = jnp.minimum(win, pos + 1).astype(F32)
        pooled = s[POOL_HALO:, :] / cnt - x[:, sl]
        y_ref[:, sl] = (_dot(pooled.astype(BF16), w_ref[gi]) * sc_ref[:, sl]).astype(y_ref.dtype)


def _pool(halo_src, x_src, w_bf16, scale, *, n_rows, rows, pos0, tiles_per_seq, fresh, out_dtype):
    w = len(POOL_WINDOWS) * C_GD
    if fresh:
        step = rows // POOL_HALO
        halo_map = lambda i: (jnp.maximum(i * step - 1, 0), 0)
    else:
        halo_map = lambda i: (i, 0)
    return pl.pallas_call(
        functools.partial(_pool_kernel, rows=rows, pos0=pos0, tiles_per_seq=tiles_per_seq,
                          fresh=fresh),
        grid=(n_rows // rows,),
        in_specs=[pl.BlockSpec((POOL_HALO, w), halo_map),
                  pl.BlockSpec((rows, w), lambda i: (i, 0)),
                  _resident((len(POOL_WINDOWS), C_GD, C_GD)), _resident((1, w))],
        out_specs=pl.BlockSpec((rows, w), lambda i: (i, 0)),
        out_shape=jax.ShapeDtypeStruct((n_rows, w), out_dtype),
        compiler_params=_params(("arbitrary",), 24),
        name="pool_mix",
    )(halo_src, x_src, w_bf16, scale.reshape(1, w))


def _hgrn_kernel(q_ref, f_ref, i_ref, g_ref, s0_ref, lb_ref, ng_ref, o_ref, sfin_ref, s_scr,
                 *, rows, in_rows, valid):
    t = pl.program_id(1)
    c_rows = HGRN_CHUNK

    @pl.when(t == 0)
    def _():
        s_scr[...] = s0_ref[...]

    r_io = lax.broadcasted_iota(jnp.int32, (c_rows, c_rows), 0)
    c_io = lax.broadcasted_iota(jnp.int32, (c_rows, c_rows), 1)
    causal = r_io >= c_io
    ltri = jnp.where(causal, 1.0, 0.0).astype(BF16)
    lgrp = jnp.where(c_io < (r_io // HGRN_SUB) * HGRN_SUB + HGRN_SUB // 2, 1.0, 0.0).astype(BF16)
    lsum = jnp.concatenate([ltri, lgrp], axis=0)
    eye = r_io == c_io
    row_id = lax.broadcasted_iota(jnp.int32, (c_rows, D_HK), 0)

    heads = [slice(hd * D_HK, (hd + 1) * D_HK) for hd in range(D_HEADS)]

    def load(ref, r0, sl):
        if in_rows == rows:
            return ref[pl.ds(r0, c_rows), sl]
        return jnp.concatenate([ref[:, sl], jnp.zeros((c_rows - in_rows, D_HK), F32)], axis=0)

    def decay_sums(r0):
        gates = []
        for sl in heads:
            lb = lb_ref[:, sl]
            half_span = 0.5 * (1.0 - lb)
            f = (lb + half_span) + half_span * jnp.tanh(0.5 * load(f_ref, r0, sl))
            logf = jnp.log2(f)
            kk = 1.0 - f
            if valid < rows:
                live = (t * rows + r0 + row_id) < valid
                logf = jnp.where(live, logf, 0.0)
                kk = jnp.where(live, kk, 0.0)
            sums = _dot_exact_lhs(lsum, logf)
            gates.append((kk, sums[:c_rows, :], sums[c_rows:, :]))
        return gates

    def chunk_scores(r0, gates):
        attns, queries = [], []
        for sl, (kk, cg, ref) in zip(heads, gates):
            q = _silu(load(q_ref, r0, sl))
            qd = (q * jnp.exp2(cg - ref)).astype(BF16)
            blocks = []
            for i in range(c_rows // HGRN_SUB):
                ref_i = ref[i * HGRN_SUB:i * HGRN_SUB + 1, :]
                e = jnp.where(row_id < (i + 1) * HGRN_SUB, ref_i - cg, 0.0)
                k_i = (kk * jnp.exp2(e)).astype(BF16)
                blocks.append(_dot_nt(qd[i * HGRN_SUB:(i + 1) * HGRN_SUB, :], k_i))
            attns.append(jnp.where(causal, jnp.concatenate(blocks, axis=0), 0.0).astype(BF16))
            queries.append((q * jnp.exp2(cg)).astype(BF16))
        return attns, queries

    def advance_state(r0, gates, attns, queries):
        outs = []
        for hd, (sl, (kk, cg, _)) in enumerate(zip(heads, gates)):
            vb = load(i_ref, r0, sl).astype(BF16)
            state = s_scr[hd]
            outs.append(_dot(queries[hd], state.astype(BF16)) + _dot(attns[hd], vb))
            g_last = cg[c_rows - 1:c_rows, :]
            kd = kk * jnp.exp2(g_last - cg)
            decay_col = jnp.sum(
                jnp.where(eye, jnp.broadcast_to(jnp.exp2(g_last), (c_rows, D_HK)), 0.0),
                axis=1, keepdims=True)
            s_scr[hd] = state * decay_col + _dot(kd.T.astype(BF16), vb)
        return outs

    def finish(r0, outs):
        for sl, o in zip(heads, outs):
            o = o * lax.rsqrt(jnp.mean(o * o, axis=-1, keepdims=True) + RMS_EPS) * ng_ref[...]
            o = (o * _silu(load(g_ref, r0, sl))).astype(o_ref.dtype)
            if in_rows == rows:
                o_ref[pl.ds(r0, c_rows), sl] = o
            else:
                o_ref[:, sl] = o[:in_rows, :]

    group = min(HGRN_GROUP, rows // c_rows)

    def chunk_group(c, carry):
        starts = [pl.multiple_of((c * group + k) * c_rows, c_rows) for k in range(group)]
        gates = [decay_sums(r0) for r0 in starts]
        outs = []
        for r0, gt in zip(starts, gates):
            attns, queries = chunk_scores(r0, gt)
            outs.append(advance_state(r0, gt, attns, queries))
        for r0, o in zip(starts, outs):
            finish(r0, o)
        return carry

    lax.fori_loop(0, rows // (c_rows * group), chunk_group, 0)

    @pl.when(t == pl.num_programs(1) - 1)
    def _():
        sfin_ref[...] = s_scr[...]


def _hgrn(src, s0, lb, norm_g, *, bsz, length, rows, in_rows, valid, out_dtype):
    w = D_HEADS * D_HK
    tiles = length // rows
    col = lambda k: pl.BlockSpec((in_rows, w), lambda b, t: (b * tiles + t, 1 + k))
    state_spec = pl.BlockSpec((None, D_HEADS, D_HK, D_HK), lambda b, t: (b, 0, 0, 0))
    return pl.pallas_call(
        functools.partial(_hgrn_kernel, rows=rows, in_rows=in_rows, valid=valid),
        grid=(bsz, tiles),
        in_specs=[col(0), col(1), col(2), col(3), state_spec, _resident((1, w)),
                  _resident((1, D_HK))],
        out_specs=[pl.BlockSpec((in_rows, w), lambda b, t: (b * tiles + t, 0)), state_spec],
        out_shape=[jax.ShapeDtypeStruct((bsz * tiles * in_rows, w), out_dtype),
                   jax.ShapeDtypeStruct((bsz, D_HEADS, D_HK, D_HK), F32)],
        scratch_shapes=[pltpu.VMEM((D_HEADS, D_HK, D_HK), F32)],
        compiler_params=_params(("parallel", "arbitrary"), 32),
        name="hgrn2",
    )(src, src, src, src, s0, lb.reshape(1, w), norm_g.reshape(1, D_HK))


def kernel(x_prompt, x_sample, cache_k, cache_v, state_pool, state_hgrn, page_table, w_in_even, w_out_even, gmlp_ws, gmlp_bs, gmlp_ln_g, gmlp_ln_b, w_in_odd, w_out_odd, pool_w, pool_scale, hgrn_lb_param, hgrn_norm_g, ln_mix_g, ln_mix_b, ln_ffn_g, ln_ffn_b, ffn_w1, ffn_w2):
    w = HALF_W
    xp = x_prompt.reshape(N_PROMPT, D_MODEL)
    xs = x_sample.reshape(N_SAMPLE, D_MODEL)
    pad_tokens = ((0, 0), (0, SUBLANES - DEC_SEQ), (0, 0))

    col_scale = jnp.where(jnp.arange(EVEN_IN) < w, SCORE_SCALE, 1.0)
    w_even = (w_in_even[0] * col_scale).astype(BF16)
    w_k_u_gv = jnp.concatenate([w_even[:, w:2 * w], w_even[:, 3 * w:]], axis=1)
    kb, ugv, qt, vt, kt32, vt32 = _proj_even_prompt(x_prompt, w_k_u_gv, w_even[:, :3 * w].T)
    h0s = _proj(xs, w_even, SAMPLE_TILE)
    hs = h0s.reshape(DEC_BATCH, DEC_SEQ, EVEN_IN)
    oa_prompt, oa_sample = _moba(page_table, qt, kb, vt, hs[..., :w],
                                 jnp.pad(hs[..., w:2 * w], pad_tokens), jnp.pad(hs[..., 2 * w:3 * w], pad_tokens),
                                 cache_k[0].transpose(0, 2, 3, 1), cache_v[0].transpose(0, 2, 3, 1))

    tril = jnp.tril(jnp.ones((B_CHUNK, B_CHUNK), bool))
    ws_prompt = jnp.where(tril[None], gmlp_ws[0], 0.0)
    ws_sample = jax.vmap(lambda m: jnp.kron(jnp.eye(DEC_BATCH, dtype=F32), m[:DEC_SEQ, :DEC_SEQ]))(ws_prompt)
    bias_prompt = jnp.repeat(gmlp_bs[0].T, B_GD, axis=1)
    bias_sample = jnp.tile(bias_prompt[:DEC_SEQ], (DEC_BATCH, 1))
    grp = jnp.arange(w) // B_GD
    avg = jnp.where(grp[:, None] == grp[None, :], 1.0 / B_GD, 0.0).astype(BF16)
    ob_prompt, = _gmlp(ugv, 0, 1, ws_prompt.astype(BF16), bias_prompt, avg, gmlp_ln_g[0], gmlp_ln_b[0],
                       out_dtype=BF16, emit_vn=False)
    ob_sample, vn_sample = _gmlp(h0s, 3, 4, ws_sample.astype(BF16), bias_sample, avg, gmlp_ln_g[0],
                                 gmlp_ln_b[0], out_dtype=F32, emit_vn=True)

    layer0 = (w_out_even[0].astype(BF16), ln_mix_g[0], ln_mix_b[0],
              ffn_w1[0].astype(BF16), ffn_w2[0].astype(BF16), ln_ffn_g[0], ln_ffn_b[0])
    xp = _mix_ffn(oa_prompt, ob_prompt, xp, *layer0, PROMPT_TILE)
    xs = _mix_ffn(oa_sample.reshape(N_SAMPLE, w), ob_sample, xs, *layer0, SAMPLE_TILE)

    per_head = lambda t: t.reshape(BATCH, A_HEADS, A_HD, SEQ).transpose(0, 3, 1, 2)[None]
    new_k_prompt = per_head(kt32)
    new_v_prompt = per_head(vt32)
    new_k_sample = hs[..., w:2 * w].reshape(1, DEC_BATCH, DEC_SEQ, A_HEADS, A_HD)
    new_v_sample = hs[..., 2 * w:3 * w].reshape(1, DEC_BATCH, DEC_SEQ, A_HEADS, A_HD)
    new_gmlp_v_sample = vn_sample.reshape(1, DEC_BATCH, DEC_SEQ, w)

    w_odd = w_in_odd[0].astype(BF16)
    lb = jax.nn.softmax(hgrn_lb_param.astype(F32), axis=0)[0]
    pool_wb = pool_w[0].astype(BF16)

    h1p = _proj(xp, w_odd, PROMPT_TILE)
    oc_prompt = _pool(h1p, h1p, pool_wb, pool_scale[0], n_rows=N_PROMPT, rows=POOL_ROWS, pos0=0,
                      tiles_per_seq=SEQ // POOL_ROWS, fresh=True, out_dtype=BF16)
    od_prompt, s_prompt = _hgrn(h1p, jnp.zeros((BATCH, D_HEADS, D_HK, D_HK), F32), lb, hgrn_norm_g[0],
                                bsz=BATCH, length=SEQ, rows=HGRN_ROWS, in_rows=HGRN_ROWS, valid=SEQ,
                                out_dtype=BF16)

    h1s = _proj(xs, w_odd, SAMPLE_TILE).reshape(DEC_BATCH, DEC_SEQ, ODD_IN)
    h1s_pad = jnp.pad(h1s, pad_tokens).reshape(DEC_BATCH * SUBLANES, ODD_IN)
    halo_s = jnp.pad(state_pool[0], ((0, 0), (POOL_HALO - POOL_BUF, 0), (0, 0)))
    oc_sample = _pool(halo_s.reshape(DEC_BATCH * POOL_HALO, w), h1s_pad, pool_wb, pool_scale[0],
                      n_rows=DEC_BATCH * SUBLANES, rows=SUBLANES, pos0=PAST_LEN, tiles_per_seq=1,
                      fresh=False, out_dtype=F32)
    od_sample, s_sample = _hgrn(h1s_pad, state_hgrn[0], lb, hgrn_norm_g[0], bsz=DEC_BATCH,
                                length=HGRN_CHUNK, rows=HGRN_CHUNK, in_rows=SUBLANES, valid=DEC_SEQ,
                                out_dtype=F32)
    real_rows = lambda t: t.reshape(DEC_BATCH, SUBLANES, w)[:, :DEC_SEQ].reshape(N_SAMPLE, w)

    layer1 = (w_out_odd[0].astype(BF16), ln_mix_g[1], ln_mix_b[1],
              ffn_w1[1].astype(BF16), ffn_w2[1].astype(BF16), ln_ffn_g[1], ln_ffn_b[1])
    xp = _mix_ffn(oc_prompt, od_prompt, xp, *layer1, PROMPT_TILE)
    xs = _mix_ffn(real_rows(oc_sample), real_rows(od_sample), xs, *layer1, SAMPLE_TILE)

    new_pool_prompt = h1p.reshape(BATCH, SEQ, ODD_IN)[:, SEQ - POOL_BUF:, :w][None]
    new_pool_sample = jnp.concatenate([state_pool[0][:, DEC_SEQ:], h1s[..., :w]], axis=1)[None]
    return (xp.reshape(BATCH, SEQ, D_MODEL), xs.reshape(DEC_BATCH, DEC_SEQ, D_MODEL),
            new_k_prompt, new_v_prompt, new_k_sample, new_v_sample, new_gmlp_v_sample,
            new_pool_prompt, new_pool_sample, s_prompt[None], s_sample[None])
```

```python
import functools
import math

import jax
import jax.numpy as jnp
from jax import lax
from jax.experimental import pallas as pl
from jax.experimental.pallas import tpu as pltpu

F32 = jnp.float32
BF16 = jnp.bfloat16

D_MODEL = 1024
BATCH = 4
SEQ = 4096
DEPTH = 2
DEC_BATCH = 32
DEC_SEQ = 4
PAST_LEN = 8192
PAGE_SIZE = 128
HALF_W = D_MODEL // 2
A_HD = 64
A_HEADS = HALF_W // A_HD
MOBA_BLOCK = 256
MOBA_TOPK = 3
B_GROUPS = 8
B_GD = HALF_W // B_GROUPS
B_CHUNK = 128
POOL_WINDOWS = (2, 4, 8, 16)
C_GD = HALF_W // len(POOL_WINDOWS)
POOL_BUF = max(POOL_WINDOWS) - 1
D_HK = 128
D_HEADS = HALF_W // D_HK
D_FF = 4 * D_MODEL
EVEN_IN = 5 * HALF_W
ODD_IN = 5 * HALF_W
ALPHA = (2 * DEPTH) ** 0.25
LN_EPS = 1e-5
RMS_EPS = 1e-6
NEG = -1e30

N_PROMPT = BATCH * SEQ
N_SAMPLE = DEC_BATCH * DEC_SEQ
N_PAGES = PAST_LEN // PAGE_SIZE
N_PAST_BLOCKS = PAST_LEN // MOBA_BLOCK
PAGES_PER_BLOCK = MOBA_BLOCK // PAGE_SIZE
N_KEY_BLOCKS = SEQ // MOBA_BLOCK

LANES = 128
SUBLANES = 8
PROMPT_TILE = 512
SAMPLE_TILE = N_SAMPLE
FF_CHUNK = 1024
FFN_ROW_GROUPS = 2
MOBA_LOOKAHEAD = 7
SAMPLE_PAGES_PER_TILE = DEC_BATCH * N_PAGES // (BATCH * N_KEY_BLOCKS)
TILES_PER_SAMPLE = N_PAGES // SAMPLE_PAGES_PER_TILE
assert SAMPLE_PAGES_PER_TILE * BATCH * N_KEY_BLOCKS == DEC_BATCH * N_PAGES
assert TILES_PER_SAMPLE * SAMPLE_PAGES_PER_TILE == N_PAGES and N_KEY_BLOCKS % TILES_PER_SAMPLE == 0
assert SAMPLE_PAGES_PER_TILE % PAGES_PER_BLOCK == 0
V_TAIL = 16
V_SLAB = A_HD + V_TAIL
V_ROWS = A_HEADS * V_SLAB
assert A_HEADS % (MOBA_LOOKAHEAD + 1) == 0
SCORE_SCALE = (A_HD ** -0.5) * math.log2(math.e)
HGRN_CHUNK = 128
HGRN_SUB = 32
HGRN_ROWS = 512
HGRN_GROUP = 2
GMLP_GROUP = 4
POOL_ROWS = 512
POOL_HALO = 16
MIB = 1024 * 1024


def _dot(a, b):
    return jnp.dot(a, b, preferred_element_type=F32)


def _dot_nt(a, b):
    return lax.dot_general(a, b, (((1,), (1,)), ((), ())), preferred_element_type=F32)


def _split3(x):
    p0 = x.astype(BF16)
    r1 = x - p0.astype(F32)
    p1 = r1.astype(BF16)
    p2 = (r1 - p1.astype(F32)).astype(BF16)
    return p0, p1, p2


def _dot_exact_lhs(m_bf16, x):
    p0, p1, p2 = _split3(x)
    return _dot(m_bf16, p0) + _dot(m_bf16, p1) + _dot(m_bf16, p2)


def _dot_exact_rhs(x, m_bf16, pieces=3):
    parts = _split3(x)[:pieces]
    out = _dot(parts[0], m_bf16)
    for part in parts[1:]:
        out = out + _dot(part, m_bf16)
    return out


def _gelu_tanh(x):
    return 0.5 * x * (1.0 + jnp.tanh(0.7978845608028654 * (x + 0.044715 * (x * x * x))))


def _silu(x):
    h = 0.5 * x
    return h + h * jnp.tanh(h)


def _layer_norm(z, g, b):
    mu = jnp.mean(z, axis=-1, keepdims=True)
    zc = z - mu
    var = jnp.mean(zc * zc, axis=-1, keepdims=True)
    return zc * lax.rsqrt(var + LN_EPS) * g + b


def _params(sem, vmem_mib):
    return pltpu.CompilerParams(dimension_semantics=sem, vmem_limit_bytes=int(vmem_mib * MIB))


def _resident(shape):
    return pl.BlockSpec(shape, lambda *_: (0,) * len(shape), pipeline_mode=pl.Buffered(1))


def _proj_kernel(x_ref, w_ref, h_ref):
    xb = x_ref[...].astype(BF16)
    for c in range(w_ref.shape[1] // HALF_W):
        sl = slice(c * HALF_W, (c + 1) * HALF_W)
        h_ref[:, sl] = _dot(xb, w_ref[:, sl])


def _proj(x, w_bf16, tile):
    n, d = x.shape
    n_out = w_bf16.shape[1]
    vmem = 2 * (tile * d * 4 + tile * n_out * 4) + d * n_out * 2
    return pl.pallas_call(
        _proj_kernel,
        grid=(n // tile,),
        in_specs=[pl.BlockSpec((tile, d), lambda i: (i, 0)), _resident((d, n_out))],
        out_specs=pl.BlockSpec((tile, n_out), lambda i: (i, 0)),
        out_shape=jax.ShapeDtypeStruct((n, n_out), F32),
        compiler_params=_params(("parallel",), vmem / MIB + 8),
        name="in_proj",
    )(x, w_bf16)


def _proj_even_kernel(x_ref, wn_ref, wt_ref, kb_ref, ugv_ref, qt_ref, vt_ref, kt32_ref, vt32_ref):
    w = HALF_W
    xb = x_ref[...].astype(BF16)
    k = _dot(xb, wn_ref[:, w:2 * w])
    for half in range(PROMPT_TILE // MOBA_BLOCK):
        kb_ref[half] = k[half * MOBA_BLOCK:(half + 1) * MOBA_BLOCK, :].astype(BF16)
    ugv_ref[:, :w] = _dot(xb, wn_ref[:, 3 * w:4 * w])
    ugv_ref[:, w:] = _dot(xb, wn_ref[:, 4 * w:])
    qt = _dot_nt(wt_ref[:w, :], xb)
    kt32_ref[...] = _dot_nt(wt_ref[w:2 * w, :], xb)
    vt = _dot_nt(wt_ref[2 * w:, :], xb)
    vt32_ref[...] = vt
    ones_row = jnp.where(lax.broadcasted_iota(jnp.int32, (V_TAIL, MOBA_BLOCK), 0) == 0,
                         1.0, 0.0).astype(BF16)
    for half in range(PROMPT_TILE // MOBA_BLOCK):
        cols = slice(half * MOBA_BLOCK, (half + 1) * MOBA_BLOCK)
        qt_ref[half] = qt[:, cols].astype(BF16)
        for h in range(A_HEADS):
            vt_ref[half, h * V_SLAB:h * V_SLAB + A_HD, :] = vt[h * A_HD:(h + 1) * A_HD, cols].astype(BF16)
            vt_ref[half, h * V_SLAB + A_HD:(h + 1) * V_SLAB, :] = ones_row


def _proj_even_prompt(x_prompt, w_nat, w_tr):
    w = HALF_W
    tiles = SEQ // PROMPT_TILE
    per_tile = PROMPT_TILE // MOBA_BLOCK
    blocked = lambda shape: pl.BlockSpec((None, per_tile) + shape, lambda b, t: (b, t, 0, 0))
    vmem = (2 * (PROMPT_TILE * D_MODEL * 4 + PROMPT_TILE * 2 * w * 4 + 3 * PROMPT_TILE * w * 2
                 + 2 * PROMPT_TILE * w * 4) + 2 * D_MODEL * 3 * w * 2 + 4 * PROMPT_TILE * w * 4)
    return pl.pallas_call(
        _proj_even_kernel,
        grid=(BATCH, tiles),
        in_specs=[pl.BlockSpec((None, PROMPT_TILE, D_MODEL), lambda b, t: (b, t, 0)),
                  _resident((D_MODEL, EVEN_IN)), _resident((3 * w, D_MODEL))],
        out_specs=[blocked((MOBA_BLOCK, w)),
                   pl.BlockSpec((PROMPT_TILE, 2 * w), lambda b, t: (b * tiles + t, 0)),
                   blocked((w, MOBA_BLOCK)), blocked((V_ROWS, MOBA_BLOCK)),
                   pl.BlockSpec((None, w, PROMPT_TILE), lambda b, t: (b, 0, t)),
                   pl.BlockSpec((None, w, PROMPT_TILE), lambda b, t: (b, 0, t))],
        out_shape=[jax.ShapeDtypeStruct((BATCH, N_KEY_BLOCKS, MOBA_BLOCK, w), BF16),
                   jax.ShapeDtypeStruct((N_PROMPT, 2 * w), F32),
                   jax.ShapeDtypeStruct((BATCH, N_KEY_BLOCKS, w, MOBA_BLOCK), BF16),
                   jax.ShapeDtypeStruct((BATCH, N_KEY_BLOCKS, V_ROWS, MOBA_BLOCK), BF16),
                   jax.ShapeDtypeStruct((BATCH, w, SEQ), F32),
                   jax.ShapeDtypeStruct((BATCH, w, SEQ), F32)],
        compiler_params=_params(("parallel", "parallel"), vmem / MIB + 8),
        name="in_proj_even",
    )(x_prompt, w_nat, w_tr)


def _mix_ffn_kernel(oa_ref, ob_ref, x_ref, wo_ref, g1_ref, b1_ref, w1_ref, w2_ref, g2_ref, b2_ref,
                    y_ref):
    tile = x_ref.shape[0]
    groups = [slice(r, r + tile // FFN_ROW_GROUPS) for r in range(0, tile, tile // FFN_ROW_GROUPS)]
    mixed = [_dot(oa_ref[rows, :].astype(BF16), wo_ref[:HALF_W, :])
             + _dot(ob_ref[rows, :].astype(BF16), wo_ref[HALF_W:, :]) for rows in groups]
    outs = []
    for rows, mix in zip(groups, mixed):
        x = _layer_norm(ALPHA * x_ref[rows, :] + mix, g1_ref[...], b1_ref[...])
        xb = x.astype(BF16)
        acc = jnp.zeros(x.shape, F32)
        for c in range(D_FF // FF_CHUNK):
            sl = slice(c * FF_CHUNK, (c + 1) * FF_CHUNK)
            hid = jnp.maximum(_dot(xb, w1_ref[:, sl]), 0.0)
            acc = acc + _dot((hid * hid).astype(BF16), w2_ref[sl, :])
        outs.append(ALPHA * x + acc)
    for rows, z in zip(groups, outs):
        y_ref[rows, :] = _layer_norm(z, g2_ref[...], b2_ref[...])


def _mix_ffn(oa, ob, x, wo_bf16, g1, b1, w1_bf16, w2_bf16, g2, b2, tile):
    n, d = x.shape
    row = lambda i: (i, 0)
    vec = lambda v: v.reshape(1, d)
    vmem = (2 * (2 * tile * HALF_W * oa.dtype.itemsize + 2 * tile * d * 4)
            + (d * d + 2 * d * D_FF) * 2 + 8 * tile * FF_CHUNK * 4)
    return pl.pallas_call(
        _mix_ffn_kernel,
        grid=(n // tile,),
        in_specs=[pl.BlockSpec((tile, HALF_W), row), pl.BlockSpec((tile, HALF_W), row),
                  pl.BlockSpec((tile, d), row), _resident((d, d)),
                  _resident((1, d)), _resident((1, d)),
                  _resident((d, D_FF)), _resident((D_FF, d)),
                  _resident((1, d)), _resident((1, d))],
        out_specs=pl.BlockSpec((tile, d), row),
        out_shape=jax.ShapeDtypeStruct((n, d), F32),
        compiler_params=_params(("parallel",), vmem / MIB + 8),
        name="mix_ffn_ln",
    )(oa, ob, x, wo_bf16, vec(g1), vec(b1), w1_bf16, w2_bf16, vec(g2), vec(b2))


def _moba_kernel(pt_ref, qt_ref, k_ref, vt_ref, qs_ref, kn_ref, vn_ref, *refs):
    del pt_ref
    kp_refs = refs[:SAMPLE_PAGES_PER_TILE]
    vp_refs = refs[SAMPLE_PAGES_PER_TILE:2 * SAMPLE_PAGES_PER_TILE]
    (o_ref, os_ref, kmt_scr, qtz_scr, selb_scr, m_scr, acc_scr, s_scr, ref_scr, a_scr,
     sm_scr, sl_scr, sgate_scr, sacc_scr) = refs[2 * SAMPLE_PAGES_PER_TILE:]
    j = pl.program_id(1)
    nb = k_ref.shape[0]
    blk = MOBA_BLOCK
    pair_w = 2 * A_HD
    w = A_HEADS * A_HD
    nrow = DEC_SEQ * A_HEADS
    part = (pl.program_id(0) * nb + j) % TILES_PER_SAMPLE
    wide = lambda c: jnp.broadcast_to(c, (nrow, LANES))

    @pl.when(j == 0)
    def _():
        rows = [jnp.sum(k_ref[n].astype(F32), axis=0, keepdims=True) * (1.0 / blk)
                for n in range(nb)]
        kmean = jnp.concatenate(rows, axis=0)
        tiled = jnp.concatenate([kmean] * A_HEADS, axis=0)
        rh = lax.broadcasted_iota(jnp.int32, tiled.shape, 0) // nb
        ch = lax.broadcasted_iota(jnp.int32, tiled.shape, 1) // A_HD
        kmt_scr[...] = jnp.where(rh == ch, tiled, 0.0)

    head_mask = (lax.broadcasted_iota(jnp.int32, (A_HEADS, w), 1) // A_HD
                 == lax.broadcasted_iota(jnp.int32, (A_HEADS, w), 0))
    qs = qs_ref[...]
    qbd = jnp.concatenate(
        [jnp.where(head_mask, jnp.broadcast_to(qs[i:i + 1, :], (A_HEADS, w)), 0.0)
         for i in range(DEC_SEQ)], axis=0).astype(BF16)
    page_scores = [_dot(qbd, kp_refs[i][...].reshape(w, PAGE_SIZE).astype(BF16))
                   for i in range(SAMPLE_PAGES_PER_TILE)]

    def sample_blocks():
        probs = []
        for bl in range(SAMPLE_PAGES_PER_TILE // PAGES_PER_BLOCK):
            s = jnp.concatenate(page_scores[bl * PAGES_PER_BLOCK:(bl + 1) * PAGES_PER_BLOCK], axis=1)
            blk_id = part * (SAMPLE_PAGES_PER_TILE // PAGES_PER_BLOCK) + bl
            m = jnp.max(s, axis=-1, keepdims=True)
            p = jnp.exp2(s - m)
            sm_scr[blk_id] = wide(m)
            sl_scr[blk_id] = wide(jnp.sum(p, axis=-1, keepdims=True))
            sgate_scr[blk_id] = wide(jnp.sum(s, axis=-1, keepdims=True) * (1.0 / MOBA_BLOCK))
            probs.append(p.astype(BF16))
        for bl, p in enumerate(probs):
            blk_id = part * (SAMPLE_PAGES_PER_TILE // PAGES_PER_BLOCK) + bl
            acc = None
            for r in range(PAGES_PER_BLOCK):
                vpage = vp_refs[bl * PAGES_PER_BLOCK + r][...].reshape(w, PAGE_SIZE).astype(BF16)
                pv = _dot_nt(p[:, r * PAGE_SIZE:(r + 1) * PAGE_SIZE], vpage)
                acc = pv if acc is None else acc + pv
            sacc_scr[blk_id] = acc

    qts = qt_ref[...]
    k0, k1, k2 = _split3(kmt_scr[...])
    gate = _dot(k0, qts) + _dot(k1, qts) + _dot(k2, qts)

    n_io = lax.broadcasted_iota(jnp.int32, (nb, blk), 0)
    half = lax.broadcasted_iota(jnp.int32, (pair_w, blk), 0) // A_HD
    for h in range(A_HEADS):
        g = jnp.where(n_io < j, gate[h * nb:(h + 1) * nb, :], NEG)
        bias = jnp.full((nb, blk), NEG, F32)
        for _ in range(MOBA_TOPK):
            mx = jnp.max(g, axis=0, keepdims=True)
            cand = jnp.where((g == mx) & (mx > 0.5 * NEG), n_io, nb)
            pick = n_io == jnp.min(cand, axis=0, keepdims=True)
            bias = jnp.where(pick, 0.0, bias)
            g = jnp.where(pick, NEG, g)
        selb_scr[h * nb:(h + 1) * nb, :] = bias
        pr = h // 2
        qpair = qts[pr * pair_w:(pr + 1) * pair_w, :]
        qtz_scr[h] = jnp.where(half == (h % 2), qpair, jnp.zeros_like(qpair))

    causal = (lax.broadcasted_iota(jnp.int32, (blk, blk), 0)
              <= lax.broadcasted_iota(jnp.int32, (blk, blk), 1))

    def fold_rows(x, op):
        return op(x.reshape(blk // SUBLANES, SUBLANES, blk), axis=0)

    def stage_scores(n, h, slot, own):
        pr = h // 2
        s = _dot(k_ref[n, :, pr * pair_w:(pr + 1) * pair_w], qtz_scr[h])
        if own:
            s = jnp.where(causal, s, NEG)
        s_scr[slot] = s
        col_max = jnp.max(fold_rows(s, jnp.max), axis=0, keepdims=True)
        if own:
            m_scr[h] = col_max
            ref_scr[slot] = col_max
        else:
            bias = selb_scr[pl.ds(h * nb + n, 1), :]
            m_old = m_scr[h]
            m_new = jnp.maximum(m_old, col_max + bias)
            m_scr[h] = m_new
            a_scr[slot] = jnp.exp2(m_old - m_new)
            ref_scr[slot] = m_new - bias

    def accumulate(n, h, slot, own):
        vs = slice(h * V_SLAB, (h + 1) * V_SLAB)
        p = jnp.exp2(s_scr[slot] - ref_scr[slot])
        pv = _dot(vt_ref[n, vs, :], p.astype(BF16))
        if own:
            acc_scr[vs, :] = pv
        else:
            acc_scr[vs, :] = a_scr[slot] * acc_scr[vs, :] + pv

    n_slots = MOBA_LOOKAHEAD + 1

    def stage_ahead(n, n_next, h, own):
        ha = h + MOBA_LOOKAHEAD
        if ha < A_HEADS:
            stage_scores(n, ha, ha % n_slots, own=own)
        else:
            stage_scores(n_next, ha - A_HEADS, ha % n_slots, own=False)

    for h in range(MOBA_LOOKAHEAD):
        stage_scores(j, h, h % n_slots, own=True)
    sample_blocks()
    for h in range(A_HEADS):
        stage_ahead(j, 0, h, own=True)
        accumulate(j, h, h % n_slots, own=True)

    def past_block(n, carry):
        for h in range(A_HEADS):
            stage_ahead(n, n + 1, h, own=False)
            accumulate(n, h, h % n_slots, own=False)
        return carry

    lax.fori_loop(0, j, past_block, 0)

    heads_out = []
    for h in range(A_HEADS):
        slab = acc_scr[h * V_SLAB:(h + 1) * V_SLAB, :]
        heads_out.append(slab[:A_HD, :] / slab[A_HD:A_HD + 1, :])
    o_ref[...] = jnp.concatenate(heads_out, axis=0).T.astype(o_ref.dtype)

    @pl.when(part == TILES_PER_SAMPLE - 1)
    def _():
        tile4 = lambda c: jnp.concatenate([c] * (w // LANES), axis=1)
        s = _dot_nt(qbd, kn_ref[...].astype(BF16))
        col = lax.broadcasted_iota(jnp.int32, s.shape, 1)
        qi = lax.broadcasted_iota(jnp.int32, s.shape, 0) // A_HEADS
        s = jnp.where(col <= qi, s, NEG)
        m_col = jnp.max(s, axis=-1, keepdims=True)
        p = jnp.exp2(s - m_col)
        m_own = wide(m_col)
        l_own = wide(jnp.sum(p, axis=-1, keepdims=True))
        o_own = _dot(p, vn_ref[...])

        for _ in range(MOBA_TOPK):
            best = jnp.full((nrow, LANES), NEG, F32)
            bidx = jnp.zeros((nrow, LANES), jnp.int32)
            for n in range(N_PAST_BLOCKS):
                gn = sgate_scr[n]
                upd = gn > best
                best = jnp.where(upd, gn, best)
                bidx = jnp.where(upd, n, bidx)
            for n in range(N_PAST_BLOCKS):
                sgate_scr[n] = jnp.where(bidx == n, -jnp.inf, sgate_scr[n])

        m_all = m_own
        for n in range(N_PAST_BLOCKS):
            m_all = jnp.maximum(m_all, jnp.where(sgate_scr[n] == -jnp.inf, sm_scr[n], NEG))
        w_own = jnp.exp2(m_own - m_all)
        l_all = w_own * l_own
        o_all = tile4(w_own) * o_own
        for n in range(N_PAST_BLOCKS):
            wn = jnp.where(sgate_scr[n] == -jnp.inf, jnp.exp2(sm_scr[n] - m_all), 0.0)
            l_all = l_all + wn * sl_scr[n]
            o_all = o_all + tile4(wn) * sacc_scr[n]
        out = o_all / tile4(l_all)
        for i in range(DEC_SEQ):
            rows = out[i * A_HEADS:(i + 1) * A_HEADS, :]
            os_ref[i:i + 1, :] = jnp.sum(jnp.where(head_mask, rows, 0.0), axis=0, keepdims=True)


def _moba(page_table, qt, kb, vt, q_s, k_new, v_new, cache_kt, cache_vt):
    bsz, nb = qt.shape[:2]
    w = A_HEADS * A_HD
    nrow = DEC_SEQ * A_HEADS
    n_slots = MOBA_LOOKAHEAD + 1
    step = lambda b, j: b * nb + j
    sample_of = lambda b, j: step(b, j) // TILES_PER_SAMPLE
    first_page = lambda b, j: (step(b, j) % TILES_PER_SAMPLE) * SAMPLE_PAGES_PER_TILE
    page_spec = lambda i: pl.BlockSpec(
        (None, A_HEADS, A_HD, PAGE_SIZE),
        lambda b, j, pt: (pt[sample_of(b, j), first_page(b, j) + i], 0, 0, 0))
    per_sample = lambda rows: pl.BlockSpec((None, rows, w), lambda b, j, pt: (sample_of(b, j), 0, 0))
    whole_row = lambda shape: pl.BlockSpec((None,) + shape, lambda b, j, pt: (b, 0, 0, 0),
                                           pipeline_mode=pl.Buffered(1))
    grid_spec = pltpu.PrefetchScalarGridSpec(
        num_scalar_prefetch=1,
        grid=(bsz, nb),
        in_specs=([pl.BlockSpec((None, None, w, MOBA_BLOCK), lambda b, j, pt: (b, j, 0, 0)),
                   whole_row((nb, MOBA_BLOCK, w)), whole_row((nb, V_ROWS, MOBA_BLOCK)),
                   per_sample(DEC_SEQ), per_sample(SUBLANES), per_sample(SUBLANES)]
                  + [page_spec(i) for i in range(SAMPLE_PAGES_PER_TILE)]
                  + [page_spec(i) for i in range(SAMPLE_PAGES_PER_TILE)]),
        out_specs=[pl.BlockSpec((MOBA_BLOCK, w), lambda b, j, pt: (step(b, j), 0)),
                   per_sample(DEC_SEQ)],
        scratch_shapes=[pltpu.VMEM((A_HEADS * nb, w), F32),
                        pltpu.VMEM((A_HEADS, 2 * A_HD, MOBA_BLOCK), BF16),
                        pltpu.VMEM((A_HEADS * nb, MOBA_BLOCK), F32),
                        pltpu.VMEM((A_HEADS, 1, MOBA_BLOCK), F32),
                        pltpu.VMEM((V_ROWS, MOBA_BLOCK), F32),
                        pltpu.VMEM((n_slots, MOBA_BLOCK, MOBA_BLOCK), F32),
                        pltpu.VMEM((n_slots, 1, MOBA_BLOCK), F32),
                        pltpu.VMEM((n_slots, 1, MOBA_BLOCK), F32),
                        pltpu.VMEM((N_PAST_BLOCKS, nrow, LANES), F32),
                        pltpu.VMEM((N_PAST_BLOCKS, nrow, LANES), F32),
                        pltpu.VMEM((N_PAST_BLOCKS, nrow, LANES), F32),
                        pltpu.VMEM((N_PAST_BLOCKS, nrow, w), F32)],
    )
    vmem = (nb * MOBA_BLOCK * (w + V_ROWS) * 2 + 2 * (w * MOBA_BLOCK * 2 + MOBA_BLOCK * w * 2)
            + n_slots * MOBA_BLOCK * MOBA_BLOCK * 4
            + 2 * 2 * SAMPLE_PAGES_PER_TILE * PAGE_SIZE * w * 4
            + N_PAST_BLOCKS * nrow * (3 * LANES + w) * 4 + 4 * MIB)
    return pl.pallas_call(
        _moba_kernel,
        grid_spec=grid_spec,
        out_shape=[jax.ShapeDtypeStruct((bsz * nb * MOBA_BLOCK, w), BF16),
                   jax.ShapeDtypeStruct((DEC_BATCH, DEC_SEQ, w), F32)],
        compiler_params=_params(("arbitrary", "arbitrary"), vmem / MIB + 6),
        name="moba",
    )(page_table, qt, kb, vt, q_s, k_new, v_new,
      *([cache_kt] * SAMPLE_PAGES_PER_TILE), *([cache_vt] * SAMPLE_PAGES_PER_TILE))


def _gmlp_kernel(u_ref, gv_ref, w_ref, bias_ref, avg_ref, lng_ref, lnb_ref, ob_ref, *vn_refs):
    chunks = [slice(c * B_CHUNK, (c + 1) * B_CHUNK) for c in range(u_ref.shape[0] // B_CHUNK)]
    avg = avg_ref[...]
    gvs = [_gelu_tanh(gv_ref[rows, :]) for rows in chunks]
    centred = [gv - _dot_exact_rhs(gv, avg) for gv in gvs]
    variances = [_dot_exact_rhs(gc * gc, avg, pieces=2) for gc in centred]
    pair_w = 2 * B_GD
    lane = lax.broadcasted_iota(jnp.int32, (B_CHUNK, pair_w), 1)
    for rows, gc, var in zip(chunks, centred, variances):
        vn = gc * lax.rsqrt(var + LN_EPS) * lng_ref[...] + lnb_ref[...]
        if vn_refs:
            vn_refs[0][rows, :] = vn
        vb = vn.astype(BF16)
        u = _gelu_tanh(u_ref[rows, :])
        for pr in range(B_GROUPS // 2):
            sl = slice(pr * pair_w, (pr + 1) * pair_w)
            vp = vb[:, sl]
            zero = jnp.zeros_like(vp)
            mixed = (_dot(w_ref[2 * pr], jnp.where(lane < B_GD, vp, zero))
                     + _dot(w_ref[2 * pr + 1], jnp.where(lane >= B_GD, vp, zero)))
            ob_ref[rows, sl] = (u[:, sl] * (mixed + bias_ref[:, sl])).astype(ob_ref.dtype)


def _gmlp(src, u_col, gv_col, w_masked, bias, avg, ln_g, ln_b, *, out_dtype, emit_vn):
    rows = src.shape[0]
    w = B_GROUPS * B_GD
    tile = B_CHUNK * min(GMLP_GROUP, rows // B_CHUNK)
    out_shape = [jax.ShapeDtypeStruct((rows, w), out_dtype)]
    out_specs = [pl.BlockSpec((tile, w), lambda c: (c, 0))]
    if emit_vn:
        out_shape.append(jax.ShapeDtypeStruct((rows, w), F32))
        out_specs.append(pl.BlockSpec((tile, w), lambda c: (c, 0)))
    return pl.pallas_call(
        _gmlp_kernel,
        grid=(rows // tile,),
        in_specs=[pl.BlockSpec((tile, w), lambda c: (c, u_col)),
                  pl.BlockSpec((tile, w), lambda c: (c, gv_col)),
                  _resident((B_GROUPS, B_CHUNK, B_CHUNK)), _resident((B_CHUNK, w)),
                  _resident((w, w)), _resident((1, w)), _resident((1, w))],
        out_specs=out_specs,
        out_shape=out_shape,
        compiler_params=_params(("parallel",), 24),
        name="gmlp_gate",
    )(src, src, w_masked, bias, avg, ln_g.reshape(1, w), ln_b.reshape(1, w))


def _pool_kernel(halo_ref, x_ref, w_ref, sc_ref, y_ref, *, rows, pos0, tiles_per_seq, fresh):
    t = pl.program_id(0) % tiles_per_seq
    halo = halo_ref[...]
    if fresh:
        halo = jnp.where(t == 0, 0.0, halo)
    x = x_ref[...]
    ext = jnp.concatenate([halo, x], axis=0)
    pos = pos0 + t * rows + lax.broadcasted_iota(jnp.int32, (rows, C_GD), 0)
    for gi, win in enumerate(POOL_WINDOWS):
        sl = slice(gi * C_GD, (gi + 1) * C_GD)
        s = ext[:, sl]
        sh = 1
        while sh < win:
            s = s + pltpu.roll(s, sh, 0)
            sh *= 2
        cnt = jnp.minimum(win, pos + 1).astype(F32)
        pooled = s[POOL_HALO:, :] / cnt - x[:, sl]
        y_ref[:, sl] = (_dot(pooled.astype(BF16), w_ref[gi]) * sc_ref[:, sl]).astype(y_ref.dtype)


def _pool(halo_src, x_src, w_bf16, scale, *, n_rows, rows, pos0, tiles_per_seq, fresh, out_dtype):
    w = len(POOL_WINDOWS) * C_GD
    if fresh:
        step = rows // POOL_HALO
        halo_map = lambda i: (jnp.maximum(i * step - 1, 0), 0)
    else:
        halo_map = lambda i: (i, 0)
    return pl.pallas_call(
        functools.partial(_pool_kernel, rows=rows, pos0=pos0, tiles_per_seq=tiles_per_seq,
                          fresh=fresh),
        grid=(n_rows // rows,),
        in_specs=[pl.BlockSpec((POOL_HALO, w), halo_map),
                  pl.BlockSpec((rows, w), lambda i: (i, 0)),
                  _resident((len(POOL_WINDOWS), C_GD, C_GD)), _resident((1, w))],
        out_specs=pl.BlockSpec((rows, w), lambda i: (i, 0)),
        out_shape=jax.ShapeDtypeStruct((n_rows, w), out_dtype),
        compiler_params=_params(("arbitrary",), 24),
        name="pool_mix",
    )(halo_src, x_src, w_bf16, scale.reshape(1, w))


def _hgrn_kernel(q_ref, f_ref, i_ref, g_ref, s0_ref, lb_ref, ng_ref, o_ref, sfin_ref, s_scr,
                 *, rows, in_rows, valid):
    t = pl.program_id(1)
    c_rows = HGRN_CHUNK

    @pl.when(t == 0)
    def _():
        s_scr[...] = s0_ref[...]

    r_io = lax.broadcasted_iota(jnp.int32, (c_rows, c_rows), 0)
    c_io = lax.broadcasted_iota(jnp.int32, (c_rows, c_rows), 1)
    causal = r_io >= c_io
    ltri = jnp.where(causal, 1.0, 0.0).astype(BF16)
    lgrp = jnp.where(c_io < (r_io // HGRN_SUB) * HGRN_SUB + HGRN_SUB // 2, 1.0, 0.0).astype(BF16)
    lsum = jnp.concatenate([ltri, lgrp], axis=0)
    eye = r_io == c_io
    row_id = lax.broadcasted_iota(jnp.int32, (c_rows, D_HK), 0)

    heads = [slice(hd * D_HK, (hd + 1) * D_HK) for hd in range(D_HEADS)]

    def load(ref, r0, sl):
        if in_rows == rows:
            return ref[pl.ds(r0, c_rows), sl]
        return jnp.concatenate([ref[:, sl], jnp.zeros((c_rows - in_rows, D_HK), F32)], axis=0)

    def decay_sums(r0):
        gates = []
        for sl in heads:
            lb = lb_ref[:, sl]
            half_span = 0.5 * (1.0 - lb)
            f = (lb + half_span) + half_span * jnp.tanh(0.5 * load(f_ref, r0, sl))
            logf = jnp.log2(f)
            kk = 1.0 - f
            if valid < rows:
                live = (t * rows + r0 + row_id) < valid
                logf = jnp.where(live, logf, 0.0)
                kk = jnp.where(live, kk, 0.0)
            sums = _dot_exact_lhs(lsum, logf)
            gates.append((kk, sums[:c_rows, :], sums[c_rows:, :]))
        return gates

    def chunk_scores(r0, gates):
        attns, queries = [], []
        for sl, (kk, cg, ref) in zip(heads, gates):
            q = _silu(load(q_ref, r0, sl))
            qd = (q * jnp.exp2(cg - ref)).astype(BF16)
            blocks = []
            for i in range(c_rows // HGRN_SUB):
                ref_i = ref[i * HGRN_SUB:i * HGRN_SUB + 1, :]
                e = jnp.where(row_id < (i + 1) * HGRN_SUB, ref_i - cg, 0.0)
                k_i = (kk * jnp.exp2(e)).astype(BF16)
                blocks.append(_dot_nt(qd[i * HGRN_SUB:(i + 1) * HGRN_SUB, :], k_i))
            attns.append(jnp.where(causal, jnp.concatenate(blocks, axis=0), 0.0).astype(BF16))
            queries.append((q * jnp.exp2(cg)).astype(BF16))
        return attns, queries

    def advance_state(r0, gates, attns, queries):
        outs = []
        for hd, (sl, (kk, cg, _)) in enumerate(zip(heads, gates)):
            vb = load(i_ref, r0, sl).astype(BF16)
            state = s_scr[hd]
            outs.append(_dot(queries[hd], state.astype(BF16)) + _dot(attns[hd], vb))
            g_last = cg[c_rows - 1:c_rows, :]
            kd = kk * jnp.exp2(g_last - cg)
            decay_col = jnp.sum(
                jnp.where(eye, jnp.broadcast_to(jnp.exp2(g_last), (c_rows, D_HK)), 0.0),
                axis=1, keepdims=True)
            s_scr[hd] = state * decay_col + _dot(kd.T.astype(BF16), vb)
        return outs

    def finish(r0, outs):
        for sl, o in zip(heads, outs):
            o = o * lax.rsqrt(jnp.mean(o * o, axis=-1, keepdims=True) + RMS_EPS) * ng_ref[...]
            o = (o * _silu(load(g_ref, r0, sl))).astype(o_ref.dtype)
            if in_rows == rows:
                o_ref[pl.ds(r0, c_rows), sl] = o
            else:
                o_ref[:, sl] = o[:in_rows, :]

    group = min(HGRN_GROUP, rows // c_rows)

    def chunk_group(c, carry):
        starts = [pl.multiple_of((c * group + k) * c_rows, c_rows) for k in range(group)]
        gates = [decay_sums(r0) for r0 in starts]
        outs = []
        for r0, gt in zip(starts, gates):
            attns, queries = chunk_scores(r0, gt)
            outs.append(advance_state(r0, gt, attns, queries))
        for r0, o in zip(starts, outs):
            finish(r0, o)
        return carry

    lax.fori_loop(0, rows // (c_rows * group), chunk_group, 0)

    @pl.when(t == pl.num_programs(1) - 1)
    def _():
        sfin_ref[...] = s_scr[...]


def _hgrn(src, s0, lb, norm_g, *, bsz, length, rows, in_rows, valid, out_dtype):
    w = D_HEADS * D_HK
    tiles = length // rows
    col = lambda k: pl.BlockSpec((in_rows, w), lambda b, t: (b * tiles + t, 1 + k))
    state_spec = pl.BlockSpec((None, D_HEADS, D_HK, D_HK), lambda b, t: (b, 0, 0, 0))
    return pl.pallas_call(
        functools.partial(_hgrn_kernel, rows=rows, in_rows=in_rows, valid=valid),
        grid=(bsz, tiles),
        in_specs=[col(0), col(1), col(2), col(3), state_spec, _resident((1, w)),
                  _resident((1, D_HK))],
        out_specs=[pl.BlockSpec((in_rows, w), lambda b, t: (b * tiles + t, 0)), state_spec],
        out_shape=[jax.ShapeDtypeStruct((bsz * tiles * in_rows, w), out_dtype),
                   jax.ShapeDtypeStruct((bsz, D_HEADS, D_HK, D_HK), F32)],
        scratch_shapes=[pltpu.VMEM((D_HEADS, D_HK, D_HK), F32)],
        compiler_params=_params(("parallel", "arbitrary"), 32),
        name="hgrn2",
    )(src, src, src, src, s0, lb.reshape(1, w), norm_g.reshape(1, D_HK))


def kernel(x_prompt, x_sample, cache_k, cache_v, state_pool, state_hgrn, page_table, w_in_even, w_out_even, gmlp_ws, gmlp_bs, gmlp_ln_g, gmlp_ln_b, w_in_odd, w_out_odd, pool_w, pool_scale, hgrn_lb_param, hgrn_norm_g, ln_mix_g, ln_mix_b, ln_ffn_g, ln_ffn_b, ffn_w1, ffn_w2):
    w = HALF_W
    xp = x_prompt.reshape(N_PROMPT, D_MODEL)
    xs = x_sample.reshape(N_SAMPLE, D_MODEL)
    pad_tokens = ((0, 0), (0, SUBLANES - DEC_SEQ), (0, 0))

    col_scale = jnp.where(jnp.arange(EVEN_IN) < w, SCORE_SCALE, 1.0)
    w_even = (w_in_even[0] * col_scale).astype(BF16)
    kb, ugv, qt, vt, kt32, vt32 = _proj_even_prompt(x_prompt, w_even, w_even[:, :3 * w].T)
    h0s = _proj(xs, w_even, SAMPLE_TILE)
    hs = h0s.reshape(DEC_BATCH, DEC_SEQ, EVEN_IN)
    oa_prompt, oa_sample = _moba(page_table, qt, kb, vt, hs[..., :w],
                                 jnp.pad(hs[..., w:2 * w], pad_tokens), jnp.pad(hs[..., 2 * w:3 * w], pad_tokens),
                                 cache_k[0].transpose(0, 2, 3, 1), cache_v[0].transpose(0, 2, 3, 1))

    tril = jnp.tril(jnp.ones((B_CHUNK, B_CHUNK), bool))
    ws_prompt = jnp.where(tril[None], gmlp_ws[0], 0.0)
    owner = jnp.arange(B_CHUNK) // DEC_SEQ
    ws_sample = jnp.where(owner[:, None] == owner[None, :],
                          jnp.tile(ws_prompt[:, :DEC_SEQ, :DEC_SEQ], (1, DEC_BATCH, DEC_BATCH)), 0.0)
    bias_prompt = jnp.repeat(gmlp_bs[0].T, B_GD, axis=1)
    bias_sample = jnp.tile(bias_prompt[:DEC_SEQ], (DEC_BATCH, 1))
    grp = jnp.arange(w) // B_GD
    avg = jnp.where(grp[:, None] == grp[None, :], 1.0 / B_GD, 0.0).astype(BF16)
    ob_prompt, = _gmlp(ugv, 0, 1, ws_prompt.astype(BF16), bias_prompt, avg, gmlp_ln_g[0], gmlp_ln_b[0],
                       out_dtype=BF16, emit_vn=False)
    ob_sample, vn_sample = _gmlp(h0s, 3, 4, ws_sample.astype(BF16), bias_sample, avg, gmlp_ln_g[0],
                                 gmlp_ln_b[0], out_dtype=F32, emit_vn=True)

    layer0 = (w_out_even[0].astype(BF16), ln_mix_g[0], ln_mix_b[0],
              ffn_w1[0].astype(BF16), ffn_w2[0].astype(BF16), ln_ffn_g[0], ln_ffn_b[0])
    xp = _mix_ffn(oa_prompt, ob_prompt, xp, *layer0, PROMPT_TILE)
    xs = _mix_ffn(oa_sample.reshape(N_SAMPLE, w), ob_sample, xs, *layer0, SAMPLE_TILE)

    per_head = lambda t: t.reshape(BATCH, A_HEADS, A_HD, SEQ).transpose(0, 3, 1, 2)[None]
    new_k_prompt = per_head(kt32)
    new_v_prompt = per_head(vt32)
    new_k_sample = hs[..., w:2 * w].reshape(1, DEC_BATCH, DEC_SEQ, A_HEADS, A_HD)
    new_v_sample = hs[..., 2 * w:3 * w].reshape(1, DEC_BATCH, DEC_SEQ, A_HEADS, A_HD)
    new_gmlp_v_sample = vn_sample.reshape(1, DEC_BATCH, DEC_SEQ, w)

    w_odd = w_in_odd[0].astype(BF16)
    lb = jax.nn.softmax(hgrn_lb_param.astype(F32), axis=0)[0]
    pool_wb = pool_w[0].astype(BF16)

    h1p = _proj(xp, w_odd, PROMPT_TILE)
    oc_prompt = _pool(h1p, h1p, pool_wb, pool_scale[0], n_rows=N_PROMPT, rows=POOL_ROWS, pos0=0,
                      tiles_per_seq=SEQ // POOL_ROWS, fresh=True, out_dtype=BF16)
    od_prompt, s_prompt = _hgrn(h1p, jnp.zeros((BATCH, D_HEADS, D_HK, D_HK), F32), lb, hgrn_norm_g[0],
                                bsz=BATCH, length=SEQ, rows=HGRN_ROWS, in_rows=HGRN_ROWS, valid=SEQ,
                                out_dtype=BF16)

    h1s = _proj(xs, w_odd, SAMPLE_TILE).reshape(DEC_BATCH, DEC_SEQ, ODD_IN)
    h1s_pad = jnp.pad(h1s, pad_tokens).reshape(DEC_BATCH * SUBLANES, ODD_IN)
    halo_s = jnp.pad(state_pool[0], ((0, 0), (POOL_HALO - POOL_BUF, 0), (0, 0)))
    oc_sample = _pool(halo_s.reshape(DEC_BATCH * POOL_HALO, w), h1s_pad, pool_wb, pool_scale[0],
                      n_rows=DEC_BATCH * SUBLANES, rows=SUBLANES, pos0=PAST_LEN, tiles_per_seq=1,
                      fresh=False, out_dtype=F32)
    od_sample, s_sample = _hgrn(h1s_pad, state_hgrn[0], lb, hgrn_norm_g[0], bsz=DEC_BATCH,
                                length=HGRN_CHUNK, rows=HGRN_CHUNK, in_rows=SUBLANES, valid=DEC_SEQ,
                                out_dtype=F32)
    real_rows = lambda t: t.reshape(DEC_BATCH, SUBLANES, w)[:, :DEC_SEQ].reshape(N_SAMPLE, w)

    layer1 = (w_out_odd[0].astype(BF16), ln_mix_g[1], ln_mix_b[1],
              ffn_w1[1].astype(BF16), ffn_w2[1].astype(BF16), ln_ffn_g[1], ln_ffn_b[1])
    xp = _mix_ffn(oc_prompt, od_prompt, xp, *layer1, PROMPT_TILE)
    xs = _mix_ffn(real_rows(oc_sample), real_rows(od_sample), xs, *layer1, SAMPLE_TILE)

    new_pool_prompt = h1p.reshape(BATCH, SEQ, ODD_IN)[:, SEQ - POOL_BUF:, :w][None]
    new_pool_sample = jnp.concatenate([state_pool[0][:, DEC_SEQ:], h1s[..., :w]], axis=1)[None]
    return (xp.reshape(BATCH, SEQ, D_MODEL), xs.reshape(DEC_BATCH, DEC_SEQ, D_MODEL),
            new_k_prompt, new_v_prompt, new_k_sample, new_v_sample, new_gmlp_v_sample,
            new_pool_prompt, new_pool_sample, s_prompt[None], s_sample[None])
```

```python
import functools
import math

import jax
import jax.numpy as jnp
from jax import lax
from jax.experimental import pallas as pl
from jax.experimental.pallas import tpu as pltpu

F32 = jnp.float32
BF16 = jnp.bfloat16

D_MODEL = 1024
BATCH = 4
SEQ = 4096
DEPTH = 2
DEC_BATCH = 32
DEC_SEQ = 4
PAST_LEN = 8192
PAGE_SIZE = 128
HALF_W = D_MODEL // 2
A_HD = 64
A_HEADS = HALF_W // A_HD
MOBA_BLOCK = 256
MOBA_TOPK = 3
B_GROUPS = 8
B_GD = HALF_W // B_GROUPS
B_CHUNK = 128
POOL_WINDOWS = (2, 4, 8, 16)
C_GD = HALF_W // len(POOL_WINDOWS)
POOL_BUF = max(POOL_WINDOWS) - 1
D_HK = 128
D_HEADS = HALF_W // D_HK
D_FF = 4 * D_MODEL
EVEN_IN = 5 * HALF_W
ODD_IN = 5 * HALF_W
ALPHA = (2 * DEPTH) ** 0.25
LN_EPS = 1e-5
RMS_EPS = 1e-6
NEG = -1e30

N_PROMPT = BATCH * SEQ
N_SAMPLE = DEC_BATCH * DEC_SEQ
N_PAGES = PAST_LEN // PAGE_SIZE
N_PAST_BLOCKS = PAST_LEN // MOBA_BLOCK
PAGES_PER_BLOCK = MOBA_BLOCK // PAGE_SIZE
N_KEY_BLOCKS = SEQ // MOBA_BLOCK

LANES = 128
SUBLANES = 8
PROMPT_TILE = 512
SAMPLE_TILE = N_SAMPLE
FF_CHUNK = 1024
FFN_ROW_GROUPS = 2
MOBA_LOOKAHEAD = 7
SAMPLE_PAGES_PER_TILE = DEC_BATCH * N_PAGES // (BATCH * N_KEY_BLOCKS)
TILES_PER_SAMPLE = N_PAGES // SAMPLE_PAGES_PER_TILE
assert SAMPLE_PAGES_PER_TILE * BATCH * N_KEY_BLOCKS == DEC_BATCH * N_PAGES
assert TILES_PER_SAMPLE * SAMPLE_PAGES_PER_TILE == N_PAGES and N_KEY_BLOCKS % TILES_PER_SAMPLE == 0
assert SAMPLE_PAGES_PER_TILE % PAGES_PER_BLOCK == 0
V_TAIL = 16
V_SLAB = A_HD + V_TAIL
V_ROWS = A_HEADS * V_SLAB
assert A_HEADS % (MOBA_LOOKAHEAD + 1) == 0
SCORE_SCALE = (A_HD ** -0.5) * math.log2(math.e)
HGRN_CHUNK = 128
HGRN_SUB = 32
HGRN_ROWS = 512
HGRN_GROUP = 2
GMLP_GROUP = 4
POOL_HALO = 16
MIB = 1024 * 1024


def _dot(a, b):
    return jnp.dot(a, b, preferred_element_type=F32)


def _dot_nt(a, b):
    return lax.dot_general(a, b, (((1,), (1,)), ((), ())), preferred_element_type=F32)


def _split3(x):
    p0 = x.astype(BF16)
    r1 = x - p0.astype(F32)
    p1 = r1.astype(BF16)
    p2 = (r1 - p1.astype(F32)).astype(BF16)
    return p0, p1, p2


def _dot_exact_lhs(m_bf16, x):
    p0, p1, p2 = _split3(x)
    return _dot(m_bf16, p0) + _dot(m_bf16, p1) + _dot(m_bf16, p2)


def _dot_exact_rhs(x, m_bf16, pieces=3):
    parts = _split3(x)[:pieces]
    out = _dot(parts[0], m_bf16)
    for part in parts[1:]:
        out = out + _dot(part, m_bf16)
    return out


def _gelu_tanh(x):
    return 0.5 * x * (1.0 + jnp.tanh(0.7978845608028654 * (x + 0.044715 * (x * x * x))))


def _silu(x):
    h = 0.5 * x
    return h + h * jnp.tanh(h)


def _layer_norm(z, g, b):
    mu = jnp.mean(z, axis=-1, keepdims=True)
    zc = z - mu
    var = jnp.mean(zc * zc, axis=-1, keepdims=True)
    return zc * lax.rsqrt(var + LN_EPS) * g + b


def _params(sem, vmem_mib):
    return pltpu.CompilerParams(dimension_semantics=sem, vmem_limit_bytes=int(vmem_mib * MIB))


def _resident(shape):
    return pl.BlockSpec(shape, lambda *_: (0,) * len(shape), pipeline_mode=pl.Buffered(1))


def _proj_kernel(x_ref, w_ref, h_ref):
    xb = x_ref[...].astype(BF16)
    for c in range(w_ref.shape[1] // HALF_W):
        sl = slice(c * HALF_W, (c + 1) * HALF_W)
        h_ref[:, sl] = _dot(xb, w_ref[:, sl])


def _proj(x, w_bf16, tile):
    n, d = x.shape
    n_out = w_bf16.shape[1]
    vmem = 2 * (tile * d * 4 + tile * n_out * 4) + d * n_out * 2
    return pl.pallas_call(
        _proj_kernel,
        grid=(n // tile,),
        in_specs=[pl.BlockSpec((tile, d), lambda i: (i, 0)), _resident((d, n_out))],
        out_specs=pl.BlockSpec((tile, n_out), lambda i: (i, 0)),
        out_shape=jax.ShapeDtypeStruct((n, n_out), F32),
        compiler_params=_params(("parallel",), vmem / MIB + 8),
        name="in_proj",
    )(x, w_bf16)


def _proj_even_kernel(x_ref, wn_ref, wt_ref, kb_ref, ugv_ref, qt_ref, vt_ref, kt32_ref, vt32_ref):
    w = HALF_W
    xb = x_ref[...].astype(BF16)
    k = _dot(xb, wn_ref[:, w:2 * w])
    for half in range(PROMPT_TILE // MOBA_BLOCK):
        kb_ref[half] = k[half * MOBA_BLOCK:(half + 1) * MOBA_BLOCK, :].astype(BF16)
    ugv_ref[:, :w] = _dot(xb, wn_ref[:, 3 * w:4 * w])
    ugv_ref[:, w:] = _dot(xb, wn_ref[:, 4 * w:])
    qt = _dot_nt(wt_ref[:w, :], xb)
    kt32_ref[...] = k.T
    vt = _dot_nt(wt_ref[2 * w:, :], xb)
    vt32_ref[...] = vt
    ones_row = jnp.where(lax.broadcasted_iota(jnp.int32, (V_TAIL, MOBA_BLOCK), 0) == 0,
                         1.0, 0.0).astype(BF16)
    for half in range(PROMPT_TILE // MOBA_BLOCK):
        cols = slice(half * MOBA_BLOCK, (half + 1) * MOBA_BLOCK)
        qt_ref[half] = qt[:, cols].astype(BF16)
        for h in range(A_HEADS):
            vt_ref[half, h * V_SLAB:h * V_SLAB + A_HD, :] = vt[h * A_HD:(h + 1) * A_HD, cols].astype(BF16)
            vt_ref[half, h * V_SLAB + A_HD:(h + 1) * V_SLAB, :] = ones_row


def _proj_even_prompt(x_prompt, w_nat, w_tr):
    w = HALF_W
    tiles = SEQ // PROMPT_TILE
    per_tile = PROMPT_TILE // MOBA_BLOCK
    blocked = lambda shape: pl.BlockSpec((None, per_tile) + shape, lambda b, t: (b, t, 0, 0))
    vmem = (2 * (PROMPT_TILE * D_MODEL * 4 + PROMPT_TILE * 2 * w * 4 + 3 * PROMPT_TILE * w * 2
                 + 2 * PROMPT_TILE * w * 4) + 2 * D_MODEL * 3 * w * 2 + 4 * PROMPT_TILE * w * 4)
    return pl.pallas_call(
        _proj_even_kernel,
        grid=(BATCH, tiles),
        in_specs=[pl.BlockSpec((None, PROMPT_TILE, D_MODEL), lambda b, t: (b, t, 0)),
                  _resident((D_MODEL, EVEN_IN)), _resident((3 * w, D_MODEL))],
        out_specs=[blocked((MOBA_BLOCK, w)),
                   pl.BlockSpec((PROMPT_TILE, 2 * w), lambda b, t: (b * tiles + t, 0)),
                   blocked((w, MOBA_BLOCK)), blocked((V_ROWS, MOBA_BLOCK)),
                   pl.BlockSpec((None, w, PROMPT_TILE), lambda b, t: (b, 0, t)),
                   pl.BlockSpec((None, w, PROMPT_TILE), lambda b, t: (b, 0, t))],
        out_shape=[jax.ShapeDtypeStruct((BATCH, N_KEY_BLOCKS, MOBA_BLOCK, w), BF16),
                   jax.ShapeDtypeStruct((N_PROMPT, 2 * w), F32),
                   jax.ShapeDtypeStruct((BATCH, N_KEY_BLOCKS, w, MOBA_BLOCK), BF16),
                   jax.ShapeDtypeStruct((BATCH, N_KEY_BLOCKS, V_ROWS, MOBA_BLOCK), BF16),
                   jax.ShapeDtypeStruct((BATCH, w, SEQ), F32),
                   jax.ShapeDtypeStruct((BATCH, w, SEQ), F32)],
        compiler_params=_params(("parallel", "parallel"), vmem / MIB + 8),
        name="in_proj_even",
    )(x_prompt, w_nat, w_tr)


def _mix_ffn_kernel(oa_ref, ob_ref, x_ref, wo_ref, g1_ref, b1_ref, w1_ref, w2_ref, g2_ref, b2_ref,
                    y_ref):
    tile = x_ref.shape[0]
    groups = [slice(r, r + tile // FFN_ROW_GROUPS) for r in range(0, tile, tile // FFN_ROW_GROUPS)]
    mixed = [_dot(oa_ref[rows, :].astype(BF16), wo_ref[:HALF_W, :])
             + _dot(ob_ref[rows, :].astype(BF16), wo_ref[HALF_W:, :]) for rows in groups]
    outs = []
    for rows, mix in zip(groups, mixed):
        x = _layer_norm(ALPHA * x_ref[rows, :] + mix, g1_ref[...], b1_ref[...])
        xb = x.astype(BF16)
        acc = jnp.zeros(x.shape, F32)
        for c in range(D_FF // FF_CHUNK):
            sl = slice(c * FF_CHUNK, (c + 1) * FF_CHUNK)
            hid = jnp.maximum(_dot(xb, w1_ref[:, sl]), 0.0)
            acc = acc + _dot((hid * hid).astype(BF16), w2_ref[sl, :])
        outs.append(ALPHA * x + acc)
    for rows, z in zip(groups, outs):
        y_ref[rows, :] = _layer_norm(z, g2_ref[...], b2_ref[...])


def _mix_ffn(oa, ob, x, wo_bf16, g1, b1, w1_bf16, w2_bf16, g2, b2, tile):
    n, d = x.shape
    row = lambda i: (i, 0)
    vec = lambda v: v.reshape(1, d)
    vmem = (2 * (2 * tile * HALF_W * oa.dtype.itemsize + 2 * tile * d * 4)
            + (d * d + 2 * d * D_FF) * 2 + 8 * tile * FF_CHUNK * 4)
    return pl.pallas_call(
        _mix_ffn_kernel,
        grid=(n // tile,),
        in_specs=[pl.BlockSpec((tile, HALF_W), row), pl.BlockSpec((tile, HALF_W), row),
                  pl.BlockSpec((tile, d), row), _resident((d, d)),
                  _resident((1, d)), _resident((1, d)),
                  _resident((d, D_FF)), _resident((D_FF, d)),
                  _resident((1, d)), _resident((1, d))],
        out_specs=pl.BlockSpec((tile, d), row),
        out_shape=jax.ShapeDtypeStruct((n, d), F32),
        compiler_params=_params(("parallel",), vmem / MIB + 8),
        name="mix_ffn_ln",
    )(oa, ob, x, wo_bf16, vec(g1), vec(b1), w1_bf16, w2_bf16, vec(g2), vec(b2))


def _moba_kernel(pt_ref, qt_ref, k_ref, vt_ref, qs_ref, kn_ref, vn_ref, *refs):
    del pt_ref
    kp_refs = refs[:SAMPLE_PAGES_PER_TILE]
    vp_refs = refs[SAMPLE_PAGES_PER_TILE:2 * SAMPLE_PAGES_PER_TILE]
    (o_ref, os_ref, kmt_scr, qtz_scr, selb_scr, m_scr, acc_scr, s_scr, ref_scr, a_scr,
     sm_scr, sl_scr, sgate_scr, sacc_scr) = refs[2 * SAMPLE_PAGES_PER_TILE:]
    j = pl.program_id(1)
    nb = k_ref.shape[0]
    blk = MOBA_BLOCK
    pair_w = 2 * A_HD
    w = A_HEADS * A_HD
    nrow = DEC_SEQ * A_HEADS
    part = (pl.program_id(0) * nb + j) % TILES_PER_SAMPLE
    wide = lambda c: jnp.broadcast_to(c, (nrow, LANES))

    @pl.when(j == 0)
    def _():
        rows = [jnp.sum(k_ref[n].astype(F32), axis=0, keepdims=True) * (1.0 / blk)
                for n in range(nb)]
        kmean = jnp.concatenate(rows, axis=0)
        tiled = jnp.concatenate([kmean] * A_HEADS, axis=0)
        rh = lax.broadcasted_iota(jnp.int32, tiled.shape, 0) // nb
        ch = lax.broadcasted_iota(jnp.int32, tiled.shape, 1) // A_HD
        kmt_scr[...] = jnp.where(rh == ch, tiled, 0.0)

    head_mask = (lax.broadcasted_iota(jnp.int32, (A_HEADS, w), 1) // A_HD
                 == lax.broadcasted_iota(jnp.int32, (A_HEADS, w), 0))
    qs = qs_ref[...]
    qbd = jnp.concatenate(
        [jnp.where(head_mask, jnp.broadcast_to(qs[i:i + 1, :], (A_HEADS, w)), 0.0)
         for i in range(DEC_SEQ)], axis=0).astype(BF16)
    page_scores = [_dot(qbd, kp_refs[i][...].reshape(w, PAGE_SIZE).astype(BF16))
                   for i in range(SAMPLE_PAGES_PER_TILE)]

    def sample_blocks():
        probs = []
        for bl in range(SAMPLE_PAGES_PER_TILE // PAGES_PER_BLOCK):
            s = jnp.concatenate(page_scores[bl * PAGES_PER_BLOCK:(bl + 1) * PAGES_PER_BLOCK], axis=1)
            blk_id = part * (SAMPLE_PAGES_PER_TILE // PAGES_PER_BLOCK) + bl
            m = jnp.max(s, axis=-1, keepdims=True)
            p = jnp.exp2(s - m)
            sm_scr[blk_id] = wide(m)
            sl_scr[blk_id] = wide(jnp.sum(p, axis=-1, keepdims=True))
            sgate_scr[blk_id] = wide(jnp.sum(s, axis=-1, keepdims=True) * (1.0 / MOBA_BLOCK))
            probs.append(p.astype(BF16))
        for bl, p in enumerate(probs):
            blk_id = part * (SAMPLE_PAGES_PER_TILE // PAGES_PER_BLOCK) + bl
            acc = None
            for r in range(PAGES_PER_BLOCK):
                vpage = vp_refs[bl * PAGES_PER_BLOCK + r][...].reshape(w, PAGE_SIZE).astype(BF16)
                pv = _dot_nt(p[:, r * PAGE_SIZE:(r + 1) * PAGE_SIZE], vpage)
                acc = pv if acc is None else acc + pv
            sacc_scr[blk_id] = acc

    qts = qt_ref[...]
    k0, k1, k2 = _split3(kmt_scr[...])
    gate = _dot(k0, qts) + _dot(k1, qts) + _dot(k2, qts)

    n_io = lax.broadcasted_iota(jnp.int32, (nb, blk), 0)
    half = lax.broadcasted_iota(jnp.int32, (pair_w, blk), 0) // A_HD
    for h in range(A_HEADS):
        g = jnp.where(n_io < j, gate[h * nb:(h + 1) * nb, :], NEG)
        bias = jnp.full((nb, blk), NEG, F32)
        for _ in range(MOBA_TOPK):
            mx = jnp.max(g, axis=0, keepdims=True)
            cand = jnp.where((g == mx) & (mx > 0.5 * NEG), n_io, nb)
            pick = n_io == jnp.min(cand, axis=0, keepdims=True)
            bias = jnp.where(pick, 0.0, bias)
            g = jnp.where(pick, NEG, g)
        selb_scr[h * nb:(h + 1) * nb, :] = bias
        pr = h // 2
        qpair = qts[pr * pair_w:(pr + 1) * pair_w, :]
        qtz_scr[h] = jnp.where(half == (h % 2), qpair, jnp.zeros_like(qpair))

    causal = (lax.broadcasted_iota(jnp.int32, (blk, blk), 0)
              <= lax.broadcasted_iota(jnp.int32, (blk, blk), 1))

    def fold_rows(x, op):
        return op(x.reshape(blk // SUBLANES, SUBLANES, blk), axis=0)

    def stage_scores(n, h, slot, own):
        pr = h // 2
        s = _dot(k_ref[n, :, pr * pair_w:(pr + 1) * pair_w], qtz_scr[h])
        if own:
            s = jnp.where(causal, s, NEG)
        s_scr[slot] = s
        col_max = jnp.max(fold_rows(s, jnp.max), axis=0, keepdims=True)
        if own:
            m_scr[h] = col_max
            ref_scr[slot] = col_max
        else:
            bias = selb_scr[pl.ds(h * nb + n, 1), :]
            m_old = m_scr[h]
            m_new = jnp.maximum(m_old, col_max + bias)
            m_scr[h] = m_new
            a_scr[slot] = jnp.exp2(m_old - m_new)
            ref_scr[slot] = m_new - bias

    def accumulate(n, h, slot, own):
        vs = slice(h * V_SLAB, (h + 1) * V_SLAB)
        p = jnp.exp2(s_scr[slot] - ref_scr[slot])
        pv = _dot(vt_ref[n, vs, :], p.astype(BF16))
        if own:
            acc_scr[vs, :] = pv
        else:
            acc_scr[vs, :] = a_scr[slot] * acc_scr[vs, :] + pv

    n_slots = MOBA_LOOKAHEAD + 1

    def stage_ahead(n, n_next, h, own):
        ha = h + MOBA_LOOKAHEAD
        if ha < A_HEADS:
            stage_scores(n, ha, ha % n_slots, own=own)
        else:
            stage_scores(n_next, ha - A_HEADS, ha % n_slots, own=False)

    for h in range(MOBA_LOOKAHEAD):
        stage_scores(j, h, h % n_slots, own=True)
    sample_blocks()
    for h in range(A_HEADS):
        stage_ahead(j, 0, h, own=True)
        accumulate(j, h, h % n_slots, own=True)

    def past_block(n, carry):
        for h in range(A_HEADS):
            stage_ahead(n, n + 1, h, own=False)
            accumulate(n, h, h % n_slots, own=False)
        return carry

    lax.fori_loop(0, j, past_block, 0)

    heads_out = []
    for h in range(A_HEADS):
        slab = acc_scr[h * V_SLAB:(h + 1) * V_SLAB, :]
        heads_out.append(slab[:A_HD, :] / slab[A_HD:A_HD + 1, :])
    o_ref[...] = jnp.concatenate(heads_out, axis=0).T.astype(o_ref.dtype)

    @pl.when(part == TILES_PER_SAMPLE - 1)
    def _():
        tile4 = lambda c: jnp.concatenate([c] * (w // LANES), axis=1)
        s = _dot_nt(qbd, kn_ref[...].astype(BF16))
        col = lax.broadcasted_iota(jnp.int32, s.shape, 1)
        qi = lax.broadcasted_iota(jnp.int32, s.shape, 0) // A_HEADS
        s = jnp.where(col <= qi, s, NEG)
        m_col = jnp.max(s, axis=-1, keepdims=True)
        p = jnp.exp2(s - m_col)
        m_own = wide(m_col)
        l_own = wide(jnp.sum(p, axis=-1, keepdims=True))
        o_own = _dot(p, vn_ref[...])

        for _ in range(MOBA_TOPK):
            best = jnp.full((nrow, LANES), NEG, F32)
            bidx = jnp.zeros((nrow, LANES), jnp.int32)
            for n in range(N_PAST_BLOCKS):
                gn = sgate_scr[n]
                upd = gn > best
                best = jnp.where(upd, gn, best)
                bidx = jnp.where(upd, n, bidx)
            for n in range(N_PAST_BLOCKS):
                sgate_scr[n] = jnp.where(bidx == n, -jnp.inf, sgate_scr[n])

        m_all = m_own
        for n in range(N_PAST_BLOCKS):
            m_all = jnp.maximum(m_all, jnp.where(sgate_scr[n] == -jnp.inf, sm_scr[n], NEG))
        w_own = jnp.exp2(m_own - m_all)
        l_all = w_own * l_own
        o_all = tile4(w_own) * o_own
        for n in range(N_PAST_BLOCKS):
            wn = jnp.where(sgate_scr[n] == -jnp.inf, jnp.exp2(sm_scr[n] - m_all), 0.0)
            l_all = l_all + wn * sl_scr[n]
            o_all = o_all + tile4(wn) * sacc_scr[n]
        out = o_all / tile4(l_all)
        for i in range(DEC_SEQ):
            rows = out[i * A_HEADS:(i + 1) * A_HEADS, :]
            os_ref[i:i + 1, :] = jnp.sum(jnp.where(head_mask, rows, 0.0), axis=0, keepdims=True)


def _moba(page_table, qt, kb, vt, q_s, k_new, v_new, cache_kt, cache_vt):
    bsz, nb = qt.shape[:2]
    w = A_HEADS * A_HD
    nrow = DEC_SEQ * A_HEADS
    n_slots = MOBA_LOOKAHEAD + 1
    step = lambda b, j: b * nb + j
    sample_of = lambda b, j: step(b, j) // TILES_PER_SAMPLE
    first_page = lambda b, j: (step(b, j) % TILES_PER_SAMPLE) * SAMPLE_PAGES_PER_TILE
    page_spec = lambda i: pl.BlockSpec(
        (None, A_HEADS, A_HD, PAGE_SIZE),
        lambda b, j, pt: (pt[sample_of(b, j), first_page(b, j) + i], 0, 0, 0))
    per_sample = lambda rows: pl.BlockSpec((None, rows, w), lambda b, j, pt: (sample_of(b, j), 0, 0))
    whole_row = lambda shape: pl.BlockSpec((None,) + shape, lambda b, j, pt: (b, 0, 0, 0),
                                           pipeline_mode=pl.Buffered(1))
    grid_spec = pltpu.PrefetchScalarGridSpec(
        num_scalar_prefetch=1,
        grid=(bsz, nb),
        in_specs=([pl.BlockSpec((None, None, w, MOBA_BLOCK), lambda b, j, pt: (b, j, 0, 0)),
                   whole_row((nb, MOBA_BLOCK, w)), whole_row((nb, V_ROWS, MOBA_BLOCK)),
                   per_sample(DEC_SEQ), per_sample(SUBLANES), per_sample(SUBLANES)]
                  + [page_spec(i) for i in range(SAMPLE_PAGES_PER_TILE)]
                  + [page_spec(i) for i in range(SAMPLE_PAGES_PER_TILE)]),
        out_specs=[pl.BlockSpec((MOBA_BLOCK, w), lambda b, j, pt: (step(b, j), 0)),
                   per_sample(DEC_SEQ)],
        scratch_shapes=[pltpu.VMEM((A_HEADS * nb, w), F32),
                        pltpu.VMEM((A_HEADS, 2 * A_HD, MOBA_BLOCK), BF16),
                        pltpu.VMEM((A_HEADS * nb, MOBA_BLOCK), F32),
                        pltpu.VMEM((A_HEADS, 1, MOBA_BLOCK), F32),
                        pltpu.VMEM((V_ROWS, MOBA_BLOCK), F32),
                        pltpu.VMEM((n_slots, MOBA_BLOCK, MOBA_BLOCK), F32),
                        pltpu.VMEM((n_slots, 1, MOBA_BLOCK), F32),
                        pltpu.VMEM((n_slots, 1, MOBA_BLOCK), F32),
                        pltpu.VMEM((N_PAST_BLOCKS, nrow, LANES), F32),
                        pltpu.VMEM((N_PAST_BLOCKS, nrow, LANES), F32),
                        pltpu.VMEM((N_PAST_BLOCKS, nrow, LANES), F32),
                        pltpu.VMEM((N_PAST_BLOCKS, nrow, w), F32)],
    )
    vmem = (nb * MOBA_BLOCK * (w + V_ROWS) * 2 + 2 * (w * MOBA_BLOCK * 2 + MOBA_BLOCK * w * 2)
            + n_slots * MOBA_BLOCK * MOBA_BLOCK * 4
            + 2 * 2 * SAMPLE_PAGES_PER_TILE * PAGE_SIZE * w * 4
            + N_PAST_BLOCKS * nrow * (3 * LANES + w) * 4 + 4 * MIB)
    return pl.pallas_call(
        _moba_kernel,
        grid_spec=grid_spec,
        out_shape=[jax.ShapeDtypeStruct((bsz * nb * MOBA_BLOCK, w), BF16),
                   jax.ShapeDtypeStruct((DEC_BATCH, DEC_SEQ, w), F32)],
        compiler_params=_params(("arbitrary", "arbitrary"), vmem / MIB + 6),
        name="moba",
    )(page_table, qt, kb, vt, q_s, k_new, v_new,
      *([cache_kt] * SAMPLE_PAGES_PER_TILE), *([cache_vt] * SAMPLE_PAGES_PER_TILE))


def _gmlp_kernel(u_ref, gv_ref, w_ref, bias_ref, avg_ref, lng_ref, lnb_ref, ob_ref, *vn_refs):
    chunks = [slice(c * B_CHUNK, (c + 1) * B_CHUNK) for c in range(u_ref.shape[0] // B_CHUNK)]
    avg = avg_ref[...]
    gvs = [_gelu_tanh(gv_ref[rows, :]) for rows in chunks]
    centred = [gv - _dot_exact_rhs(gv, avg) for gv in gvs]
    variances = [_dot_exact_rhs(gc * gc, avg, pieces=2) for gc in centred]
    pair_w = 2 * B_GD
    lane = lax.broadcasted_iota(jnp.int32, (B_CHUNK, pair_w), 1)
    for rows, gc, var in zip(chunks, centred, variances):
        vn = gc * lax.rsqrt(var + LN_EPS) * lng_ref[...] + lnb_ref[...]
        if vn_refs:
            vn_refs[0][rows, :] = vn
        vb = vn.astype(BF16)
        u = _gelu_tanh(u_ref[rows, :])
        for pr in range(B_GROUPS // 2):
            sl = slice(pr * pair_w, (pr + 1) * pair_w)
            vp = vb[:, sl]
            zero = jnp.zeros_like(vp)
            mixed = (_dot(w_ref[2 * pr], jnp.where(lane < B_GD, vp, zero))
                     + _dot(w_ref[2 * pr + 1], jnp.where(lane >= B_GD, vp, zero)))
            ob_ref[rows, sl] = (u[:, sl] * (mixed + bias_ref[:, sl])).astype(ob_ref.dtype)


def _gmlp(src, u_col, gv_col, w_masked, bias, avg, ln_g, ln_b, *, out_dtype, emit_vn):
    rows = src.shape[0]
    w = B_GROUPS * B_GD
    tile = B_CHUNK * min(GMLP_GROUP, rows // B_CHUNK)
    out_shape = [jax.ShapeDtypeStruct((rows, w), out_dtype)]
    out_specs = [pl.BlockSpec((tile, w), lambda c: (c, 0))]
    if emit_vn:
        out_shape.append(jax.ShapeDtypeStruct((rows, w), F32))
        out_specs.append(pl.BlockSpec((tile, w), lambda c: (c, 0)))
    return pl.pallas_call(
        _gmlp_kernel,
        grid=(rows // tile,),
        in_specs=[pl.BlockSpec((tile, w), lambda c: (c, u_col)),
                  pl.BlockSpec((tile, w), lambda c: (c, gv_col)),
                  _resident((B_GROUPS, B_CHUNK, B_CHUNK)), _resident((B_CHUNK, w)),
                  _resident((w, w)), _resident((1, w)), _resident((1, w))],
        out_specs=out_specs,
        out_shape=out_shape,
        compiler_params=_params(("parallel",), 24),
        name="gmlp_gate",
    )(src, src, w_masked, bias, avg, ln_g.reshape(1, w), ln_b.reshape(1, w))


def _pool_tile(halo, x, pos_first, w_ref, sc_ref, y_ref):
    rows = x.shape[0]
    ext = jnp.concatenate([halo, x], axis=0)
    pos = pos_first + lax.broadcasted_iota(jnp.int32, (rows, C_GD), 0)
    for gi, win in enumerate(POOL_WINDOWS):
        sl = slice(gi * C_GD, (gi + 1) * C_GD)
        s = ext[:, sl]
        sh = 1
        while sh < win:
            s = s + pltpu.roll(s, sh, 0)
            sh *= 2
        cnt = jnp.minimum(win, pos + 1).astype(F32)
        pooled = s[POOL_HALO:, :] / cnt - x[:, sl]
        y_ref[:, sl] = (_dot(pooled.astype(BF16), w_ref[gi]) * sc_ref[:, sl]).astype(y_ref.dtype)


def _pool_kernel(halo_ref, x_ref, w_ref, sc_ref, y_ref, *, pos0):
    _pool_tile(halo_ref[...], x_ref[...], pos0, w_ref, sc_ref, y_ref)


def _proj_pool_kernel(xh_ref, x_ref, w_ref, pw_ref, sc_ref, h_ref, y_ref, *, tiles_per_seq):
    t = pl.program_id(0) % tiles_per_seq
    rows = x_ref.shape[0]
    xb = x_ref[...].astype(BF16)
    xc = _dot(xb, w_ref[:, :HALF_W])
    h_ref[:, :HALF_W] = xc
    for c in range(1, w_ref.shape[1] // HALF_W):
        sl = slice(c * HALF_W, (c + 1) * HALF_W)
        h_ref[:, sl] = _dot(xb, w_ref[:, sl])
    halo = _dot(xh_ref[...].astype(BF16), w_ref[:, :HALF_W])
    _pool_tile(jnp.where(t == 0, 0.0, halo), xc, t * rows, pw_ref, sc_ref, y_ref)


def _pool(halo_src, x_src, w_bf16, scale, *, rows, pos0):
    w = len(POOL_WINDOWS) * C_GD
    n_rows = x_src.shape[0]
    return pl.pallas_call(
        functools.partial(_pool_kernel, pos0=pos0),
        grid=(n_rows // rows,),
        in_specs=[pl.BlockSpec((POOL_HALO, w), lambda i: (i, 0)),
                  pl.BlockSpec((rows, w), lambda i: (i, 0)),
                  _resident((len(POOL_WINDOWS), C_GD, C_GD)), _resident((1, w))],
        out_specs=pl.BlockSpec((rows, w), lambda i: (i, 0)),
        out_shape=jax.ShapeDtypeStruct((n_rows, w), F32),
        compiler_params=_params(("parallel",), 24),
        name="pool_mix",
    )(halo_src, x_src, w_bf16, scale.reshape(1, w))


def _proj_pool(x, w_bf16, pool_w_bf16, scale):
    n, d = x.shape
    n_out = w_bf16.shape[1]
    w = len(POOL_WINDOWS) * C_GD
    tile = PROMPT_TILE
    halo_step = tile // POOL_HALO
    vmem = (2 * (tile * d * 4 + POOL_HALO * d * 4 + tile * n_out * 4 + tile * w * 2)
            + d * n_out * 2 + 6 * tile * w * 4)
    return pl.pallas_call(
        functools.partial(_proj_pool_kernel, tiles_per_seq=SEQ // tile),
        grid=(n // tile,),
        in_specs=[pl.BlockSpec((POOL_HALO, d), lambda i: (jnp.maximum(i * halo_step - 1, 0), 0)),
                  pl.BlockSpec((tile, d), lambda i: (i, 0)),
                  _resident((d, n_out)),
                  _resident((len(POOL_WINDOWS), C_GD, C_GD)), _resident((1, w))],
        out_specs=[pl.BlockSpec((tile, n_out), lambda i: (i, 0)),
                   pl.BlockSpec((tile, w), lambda i: (i, 0))],
        out_shape=[jax.ShapeDtypeStruct((n, n_out), F32), jax.ShapeDtypeStruct((n, w), BF16)],
        compiler_params=_params(("parallel",), vmem / MIB + 8),
        name="in_proj_pool",
    )(x, x, w_bf16, pool_w_bf16, scale.reshape(1, w))


def _hgrn_kernel(q_ref, f_ref, i_ref, g_ref, s0_ref, lb_ref, ng_ref, o_ref, sfin_ref, s_scr,
                 *, rows, in_rows, valid):
    t = pl.program_id(1)
    c_rows = HGRN_CHUNK

    @pl.when(t == 0)
    def _():
        s_scr[...] = s0_ref[...]

    r_io = lax.broadcasted_iota(jnp.int32, (c_rows, c_rows), 0)
    c_io = lax.broadcasted_iota(jnp.int32, (c_rows, c_rows), 1)
    causal = r_io >= c_io
    ltri = jnp.where(causal, 1.0, 0.0).astype(BF16)
    lgrp = jnp.where(c_io < (r_io // HGRN_SUB) * HGRN_SUB + HGRN_SUB // 2, 1.0, 0.0).astype(BF16)
    lsum = jnp.concatenate([ltri, lgrp], axis=0)
    eye = r_io == c_io
    row_id = lax.broadcasted_iota(jnp.int32, (c_rows, D_HK), 0)

    heads = [slice(hd * D_HK, (hd + 1) * D_HK) for hd in range(D_HEADS)]

    def load(ref, r0, sl):
        if in_rows == rows:
            return ref[pl.ds(r0, c_rows), sl]
        return jnp.concatenate([ref[:, sl], jnp.zeros((c_rows - in_rows, D_HK), F32)], axis=0)

    def decay_sums(r0):
        gates = []
        for sl in heads:
            lb = lb_ref[:, sl]
            half_span = 0.5 * (1.0 - lb)
            f = (lb + half_span) + half_span * jnp.tanh(0.5 * load(f_ref, r0, sl))
            logf = jnp.log2(f)
            kk = 1.0 - f
            if valid < rows:
                live = (t * rows + r0 + row_id) < valid
                logf = jnp.where(live, logf, 0.0)
                kk = jnp.where(live, kk, 0.0)
            sums = _dot_exact_lhs(lsum, logf)
            gates.append((kk, sums[:c_rows, :], sums[c_rows:, :]))
        return gates

    def chunk_scores(r0, gates):
        attns, queries = [], []
        for sl, (kk, cg, ref) in zip(heads, gates):
            q = _silu(load(q_ref, r0, sl))
            qd = (q * jnp.exp2(cg - ref)).astype(BF16)
            blocks = []
            for i in range(c_rows // HGRN_SUB):
                ref_i = ref[i * HGRN_SUB:i * HGRN_SUB + 1, :]
                e = jnp.where(row_id < (i + 1) * HGRN_SUB, ref_i - cg, 0.0)
                k_i = (kk * jnp.exp2(e)).astype(BF16)
                blocks.append(_dot_nt(qd[i * HGRN_SUB:(i + 1) * HGRN_SUB, :], k_i))
            attns.append(jnp.where(causal, jnp.concatenate(blocks, axis=0), 0.0).astype(BF16))
            queries.append((q * jnp.exp2(cg)).astype(BF16))
        return attns, queries

    def advance_state(r0, gates, attns, queries):
        outs = []
        for hd, (sl, (kk, cg, _)) in enumerate(zip(heads, gates)):
            vb = load(i_ref, r0, sl).astype(BF16)
            state = s_scr[hd]
            outs.append(_dot(queries[hd], state.astype(BF16)) + _dot(attns[hd], vb))
            g_last = cg[c_rows - 1:c_rows, :]
            kd = kk * jnp.exp2(g_last - cg)
            decay_col = jnp.sum(
                jnp.where(eye, jnp.broadcast_to(jnp.exp2(g_last), (c_rows, D_HK)), 0.0),
                axis=1, keepdims=True)
            s_scr[hd] = state * decay_col + _dot(kd.T.astype(BF16), vb)
        return outs

    def finish(r0, outs):
        for sl, o in zip(heads, outs):
            o = o * lax.rsqrt(jnp.mean(o * o, axis=-1, keepdims=True) + RMS_EPS) * ng_ref[...]
            o = (o * _silu(load(g_ref, r0, sl))).astype(o_ref.dtype)
            if in_rows == rows:
                o_ref[pl.ds(r0, c_rows), sl] = o
            else:
                o_ref[:, sl] = o[:in_rows, :]

    group = min(HGRN_GROUP, rows // c_rows)

    def chunk_group(c, carry):
        starts = [pl.multiple_of((c * group + k) * c_rows, c_rows) for k in range(group)]
        gates = [decay_sums(r0) for r0 in starts]
        outs = []
        for r0, gt in zip(starts, gates):
            attns, queries = chunk_scores(r0, gt)
            outs.append(advance_state(r0, gt, attns, queries))
        for r0, o in zip(starts, outs):
            finish(r0, o)
        return carry

    lax.fori_loop(0, rows // (c_rows * group), chunk_group, 0)

    @pl.when(t == pl.num_programs(1) - 1)
    def _():
        sfin_ref[...] = s_scr[...]


def _hgrn(src, s0, lb, norm_g, *, bsz, length, rows, in_rows, valid, out_dtype):
    w = D_HEADS * D_HK
    tiles = length // rows
    col = lambda k: pl.BlockSpec((in_rows, w), lambda b, t: (b * tiles + t, 1 + k))
    state_spec = pl.BlockSpec((None, D_HEADS, D_HK, D_HK), lambda b, t: (b, 0, 0, 0))
    return pl.pallas_call(
        functools.partial(_hgrn_kernel, rows=rows, in_rows=in_rows, valid=valid),
        grid=(bsz, tiles),
        in_specs=[col(0), col(1), col(2), col(3), state_spec, _resident((1, w)),
                  _resident((1, D_HK))],
        out_specs=[pl.BlockSpec((in_rows, w), lambda b, t: (b * tiles + t, 0)), state_spec],
        out_shape=[jax.ShapeDtypeStruct((bsz * tiles * in_rows, w), out_dtype),
                   jax.ShapeDtypeStruct((bsz, D_HEADS, D_HK, D_HK), F32)],
        scratch_shapes=[pltpu.VMEM((D_HEADS, D_HK, D_HK), F32)],
        compiler_params=_params(("parallel", "arbitrary"), 32),
        name="hgrn2",
    )(src, src, src, src, s0, lb.reshape(1, w), norm_g.reshape(1, D_HK))


def kernel(x_prompt, x_sample, cache_k, cache_v, state_pool, state_hgrn, page_table, w_in_even, w_out_even, gmlp_ws, gmlp_bs, gmlp_ln_g, gmlp_ln_b, w_in_odd, w_out_odd, pool_w, pool_scale, hgrn_lb_param, hgrn_norm_g, ln_mix_g, ln_mix_b, ln_ffn_g, ln_ffn_b, ffn_w1, ffn_w2):
    w = HALF_W
    xp = x_prompt.reshape(N_PROMPT, D_MODEL)
    xs = x_sample.reshape(N_SAMPLE, D_MODEL)
    pad_tokens = ((0, 0), (0, SUBLANES - DEC_SEQ), (0, 0))

    col_scale = jnp.where(jnp.arange(EVEN_IN) < w, SCORE_SCALE, 1.0)
    w_even = (w_in_even[0] * col_scale).astype(BF16)
    kb, ugv, qt, vt, kt32, vt32 = _proj_even_prompt(x_prompt, w_even, w_even[:, :3 * w].T)
    h0s = _proj(xs, w_even, SAMPLE_TILE)
    hs = h0s.reshape(DEC_BATCH, DEC_SEQ, EVEN_IN)
    oa_prompt, oa_sample = _moba(page_table, qt, kb, vt, hs[..., :w],
                                 jnp.pad(hs[..., w:2 * w], pad_tokens), jnp.pad(hs[..., 2 * w:3 * w], pad_tokens),
                                 cache_k[0].transpose(0, 2, 3, 1), cache_v[0].transpose(0, 2, 3, 1))

    tril = jnp.tril(jnp.ones((B_CHUNK, B_CHUNK), bool))
    ws_prompt = jnp.where(tril[None], gmlp_ws[0], 0.0)
    owner = jnp.arange(B_CHUNK) // DEC_SEQ
    ws_sample = jnp.where(owner[:, None] == owner[None, :],
                          jnp.tile(ws_prompt[:, :DEC_SEQ, :DEC_SEQ], (1, DEC_BATCH, DEC_BATCH)), 0.0)
    bias_prompt = jnp.repeat(gmlp_bs[0].T, B_GD, axis=1)
    bias_sample = jnp.tile(bias_prompt[:DEC_SEQ], (DEC_BATCH, 1))
    grp = jnp.arange(w) // B_GD
    avg = jnp.where(grp[:, None] == grp[None, :], 1.0 / B_GD, 0.0).astype(BF16)
    ob_prompt, = _gmlp(ugv, 0, 1, ws_prompt.astype(BF16), bias_prompt, avg, gmlp_ln_g[0], gmlp_ln_b[0],
                       out_dtype=BF16, emit_vn=False)
    ob_sample, vn_sample = _gmlp(h0s, 3, 4, ws_sample.astype(BF16), bias_sample, avg, gmlp_ln_g[0],
                                 gmlp_ln_b[0], out_dtype=F32, emit_vn=True)

    layer0 = (w_out_even[0].astype(BF16), ln_mix_g[0], ln_mix_b[0],
              ffn_w1[0].astype(BF16), ffn_w2[0].astype(BF16), ln_ffn_g[0], ln_ffn_b[0])
    xp = _mix_ffn(oa_prompt, ob_prompt, xp, *layer0, PROMPT_TILE)
    xs = _mix_ffn(oa_sample.reshape(N_SAMPLE, w), ob_sample, xs, *layer0, SAMPLE_TILE)

    per_head = lambda t: t.reshape(BATCH, A_HEADS, A_HD, SEQ).transpose(0, 3, 1, 2)[None]
    new_k_prompt = per_head(kt32)
    new_v_prompt = per_head(vt32)
    new_k_sample = hs[..., w:2 * w].reshape(1, DEC_BATCH, DEC_SEQ, A_HEADS, A_HD)
    new_v_sample = hs[..., 2 * w:3 * w].reshape(1, DEC_BATCH, DEC_SEQ, A_HEADS, A_HD)
    new_gmlp_v_sample = vn_sample.reshape(1, DEC_BATCH, DEC_SEQ, w)

    w_odd = w_in_odd[0].astype(BF16)
    lb = jax.nn.softmax(hgrn_lb_param.astype(F32), axis=0)[0]
    pool_wb = pool_w[0].astype(BF16)

    h1p, oc_prompt = _proj_pool(xp, w_odd, pool_wb, pool_scale[0])
    od_prompt, s_prompt = _hgrn(h1p, jnp.zeros((BATCH, D_HEADS, D_HK, D_HK), F32), lb, hgrn_norm_g[0],
                                bsz=BATCH, length=SEQ, rows=HGRN_ROWS, in_rows=HGRN_ROWS, valid=SEQ,
                                out_dtype=BF16)

    h1s = _proj(xs, w_odd, SAMPLE_TILE).reshape(DEC_BATCH, DEC_SEQ, ODD_IN)
    h1s_pad = jnp.pad(h1s, pad_tokens).reshape(DEC_BATCH * SUBLANES, ODD_IN)
    halo_s = jnp.pad(state_pool[0], ((0, 0), (POOL_HALO - POOL_BUF, 0), (0, 0)))
    oc_sample = _pool(halo_s.reshape(DEC_BATCH * POOL_HALO, w), h1s_pad, pool_wb, pool_scale[0],
                      rows=SUBLANES, pos0=PAST_LEN)
    od_sample, s_sample = _hgrn(h1s_pad, state_hgrn[0], lb, hgrn_norm_g[0], bsz=DEC_BATCH,
                                length=HGRN_CHUNK, rows=HGRN_CHUNK, in_rows=SUBLANES, valid=DEC_SEQ,
                                out_dtype=F32)
    real_rows = lambda t: t.reshape(DEC_BATCH, SUBLANES, w)[:, :DEC_SEQ].reshape(N_SAMPLE, w)

    layer1 = (w_out_odd[0].astype(BF16), ln_mix_g[1], ln_mix_b[1],
              ffn_w1[1].astype(BF16), ffn_w2[1].astype(BF16), ln_ffn_g[1], ln_ffn_b[1])
    xp = _mix_ffn(oc_prompt, od_prompt, xp, *layer1, PROMPT_TILE)
    xs = _mix_ffn(real_rows(oc_sample), real_rows(od_sample), xs, *layer1, SAMPLE_TILE)

    new_pool_prompt = h1p.reshape(BATCH, SEQ, ODD_IN)[:, SEQ - POOL_BUF:, :w][None]
    new_pool_sample = jnp.concatenate([state_pool[0][:, DEC_SEQ:], h1s[..., :w]], axis=1)[None]
    return (xp.reshape(BATCH, SEQ, D_MODEL), xs.reshape(DEC_BATCH, DEC_SEQ, D_MODEL),
            new_k_prompt, new_v_prompt, new_k_sample, new_v_sample, new_gmlp_v_sample,
            new_pool_prompt, new_pool_sample, s_prompt[None], s_sample[None])
```

```python
import functools
import math

import jax
import jax.numpy as jnp
from jax import lax
from jax.experimental import pallas as pl
from jax.experimental.pallas import tpu as pltpu

F32 = jnp.float32
BF16 = jnp.bfloat16

D_MODEL = 1024
BATCH = 4
SEQ = 4096
DEPTH = 2
DEC_BATCH = 32
DEC_SEQ = 4
PAST_LEN = 8192
PAGE_SIZE = 128
HALF_W = D_MODEL // 2
A_HD = 64
A_HEADS = HALF_W // A_HD
MOBA_BLOCK = 256
MOBA_TOPK = 3
B_GROUPS = 8
B_GD = HALF_W // B_GROUPS
B_CHUNK = 128
POOL_WINDOWS = (2, 4, 8, 16)
C_GD = HALF_W // len(POOL_WINDOWS)
POOL_BUF = max(POOL_WINDOWS) - 1
D_HK = 128
D_HEADS = HALF_W // D_HK
D_FF = 4 * D_MODEL
EVEN_IN = 5 * HALF_W
ODD_IN = 5 * HALF_W
ALPHA = (2 * DEPTH) ** 0.25
LN_EPS = 1e-5
RMS_EPS = 1e-6
NEG = -1e30

N_PROMPT = BATCH * SEQ
N_SAMPLE = DEC_BATCH * DEC_SEQ
N_PAGES = PAST_LEN // PAGE_SIZE
N_PAST_BLOCKS = PAST_LEN // MOBA_BLOCK
PAGES_PER_BLOCK = MOBA_BLOCK // PAGE_SIZE
N_KEY_BLOCKS = SEQ // MOBA_BLOCK

LANES = 128
SUBLANES = 8
PROMPT_TILE = 512
SAMPLE_TILE = N_SAMPLE
FF_CHUNK = 1024
FFN_TILE = 1024
FFN_GROUP_ROWS = 256
MOBA_LOOKAHEAD = 7
SAMPLE_PAGES_PER_TILE = DEC_BATCH * N_PAGES // (BATCH * N_KEY_BLOCKS)
TILES_PER_SAMPLE = N_PAGES // SAMPLE_PAGES_PER_TILE
assert SAMPLE_PAGES_PER_TILE * BATCH * N_KEY_BLOCKS == DEC_BATCH * N_PAGES
assert TILES_PER_SAMPLE * SAMPLE_PAGES_PER_TILE == N_PAGES and N_KEY_BLOCKS % TILES_PER_SAMPLE == 0
assert SAMPLE_PAGES_PER_TILE % PAGES_PER_BLOCK == 0
V_TAIL = 16
V_SLAB = A_HD + V_TAIL
V_ROWS = A_HEADS * V_SLAB
assert A_HEADS % (MOBA_LOOKAHEAD + 1) == 0
SCORE_SCALE = (A_HD ** -0.5) * math.log2(math.e)
HGRN_CHUNK = 128
HGRN_SUB = 32
HGRN_ROWS = 512
HGRN_GROUP = 2
GMLP_GROUP = 4
POOL_HALO = 16
MIB = 1024 * 1024


def _dot(a, b):
    return jnp.dot(a, b, preferred_element_type=F32)


def _dot_nt(a, b):
    return lax.dot_general(a, b, (((1,), (1,)), ((), ())), preferred_element_type=F32)


def _split3(x):
    p0 = x.astype(BF16)
    r1 = x - p0.astype(F32)
    p1 = r1.astype(BF16)
    p2 = (r1 - p1.astype(F32)).astype(BF16)
    return p0, p1, p2


def _dot_exact_lhs(m_bf16, x):
    p0, p1, p2 = _split3(x)
    return _dot(m_bf16, p0) + _dot(m_bf16, p1) + _dot(m_bf16, p2)


def _dot_exact_rhs(x, m_bf16, pieces=3):
    parts = _split3(x)[:pieces]
    out = _dot(parts[0], m_bf16)
    for part in parts[1:]:
        out = out + _dot(part, m_bf16)
    return out


def _gelu_tanh(x):
    return 0.5 * x * (1.0 + jnp.tanh(0.7978845608028654 * (x + 0.044715 * (x * x * x))))


def _silu(x):
    h = 0.5 * x
    return h + h * jnp.tanh(h)


def _layer_norm(z, g, b):
    mu = jnp.mean(z, axis=-1, keepdims=True)
    zc = z - mu
    var = jnp.mean(zc * zc, axis=-1, keepdims=True)
    return zc * lax.rsqrt(var + LN_EPS) * g + b


def _params(sem, vmem_mib):
    return pltpu.CompilerParams(dimension_semantics=sem, vmem_limit_bytes=int(vmem_mib * MIB))


def _resident(shape):
    return pl.BlockSpec(shape, lambda *_: (0,) * len(shape), pipeline_mode=pl.Buffered(1))


def _proj_kernel(x_ref, w_ref, h_ref):
    xb = x_ref[...].astype(BF16)
    for c in range(w_ref.shape[1] // HALF_W):
        sl = slice(c * HALF_W, (c + 1) * HALF_W)
        h_ref[:, sl] = _dot(xb, w_ref[:, sl])


def _proj(x, w_bf16, tile):
    n, d = x.shape
    n_out = w_bf16.shape[1]
    vmem = 2 * (tile * d * 4 + tile * n_out * 4) + d * n_out * 2
    return pl.pallas_call(
        _proj_kernel,
        grid=(n // tile,),
        in_specs=[pl.BlockSpec((tile, d), lambda i: (i, 0)), _resident((d, n_out))],
        out_specs=pl.BlockSpec((tile, n_out), lambda i: (i, 0)),
        out_shape=jax.ShapeDtypeStruct((n, n_out), F32),
        compiler_params=_params(("parallel",), vmem / MIB + 8),
        name="in_proj",
    )(x, w_bf16)


def _proj_even_kernel(x_ref, wn_ref, wt_ref, kb_ref, ugv_ref, qt_ref, vt_ref, kt32_ref, vt32_ref):
    w = HALF_W
    xb = x_ref[...].astype(BF16)
    k = _dot(xb, wn_ref[:, w:2 * w])
    for half in range(PROMPT_TILE // MOBA_BLOCK):
        kb_ref[half] = k[half * MOBA_BLOCK:(half + 1) * MOBA_BLOCK, :].astype(BF16)
    ugv_ref[:, :w] = _dot(xb, wn_ref[:, 3 * w:4 * w])
    ugv_ref[:, w:] = _dot(xb, wn_ref[:, 4 * w:])
    qt = _dot_nt(wt_ref[:w, :], xb)
    kt32_ref[...] = k.T
    vt = _dot_nt(wt_ref[2 * w:, :], xb)
    vt32_ref[...] = vt
    ones_row = jnp.where(lax.broadcasted_iota(jnp.int32, (V_TAIL, MOBA_BLOCK), 0) == 0,
                         1.0, 0.0).astype(BF16)
    for half in range(PROMPT_TILE // MOBA_BLOCK):
        cols = slice(half * MOBA_BLOCK, (half + 1) * MOBA_BLOCK)
        qt_ref[half] = qt[:, cols].astype(BF16)
        for h in range(A_HEADS):
            vt_ref[half, h * V_SLAB:h * V_SLAB + A_HD, :] = vt[h * A_HD:(h + 1) * A_HD, cols].astype(BF16)
            vt_ref[half, h * V_SLAB + A_HD:(h + 1) * V_SLAB, :] = ones_row


def _proj_even_prompt(x_prompt, w_nat, w_tr):
    w = HALF_W
    tiles = SEQ // PROMPT_TILE
    per_tile = PROMPT_TILE // MOBA_BLOCK
    blocked = lambda shape: pl.BlockSpec((None, per_tile) + shape, lambda b, t: (b, t, 0, 0))
    vmem = (2 * (PROMPT_TILE * D_MODEL * 4 + PROMPT_TILE * 2 * w * 4 + 3 * PROMPT_TILE * w * 2
                 + 2 * PROMPT_TILE * w * 4) + 2 * D_MODEL * 3 * w * 2 + 4 * PROMPT_TILE * w * 4)
    return pl.pallas_call(
        _proj_even_kernel,
        grid=(BATCH, tiles),
        in_specs=[pl.BlockSpec((None, PROMPT_TILE, D_MODEL), lambda b, t: (b, t, 0)),
                  _resident((D_MODEL, EVEN_IN)), _resident((3 * w, D_MODEL))],
        out_specs=[blocked((MOBA_BLOCK, w)),
                   pl.BlockSpec((PROMPT_TILE, 2 * w), lambda b, t: (b * tiles + t, 0)),
                   blocked((w, MOBA_BLOCK)), blocked((V_ROWS, MOBA_BLOCK)),
                   pl.BlockSpec((None, w, PROMPT_TILE), lambda b, t: (b, 0, t)),
                   pl.BlockSpec((None, w, PROMPT_TILE), lambda b, t: (b, 0, t))],
        out_shape=[jax.ShapeDtypeStruct((BATCH, N_KEY_BLOCKS, MOBA_BLOCK, w), BF16),
                   jax.ShapeDtypeStruct((N_PROMPT, 2 * w), F32),
                   jax.ShapeDtypeStruct((BATCH, N_KEY_BLOCKS, w, MOBA_BLOCK), BF16),
                   jax.ShapeDtypeStruct((BATCH, N_KEY_BLOCKS, V_ROWS, MOBA_BLOCK), BF16),
                   jax.ShapeDtypeStruct((BATCH, w, SEQ), F32),
                   jax.ShapeDtypeStruct((BATCH, w, SEQ), F32)],
        compiler_params=_params(("parallel", "parallel"), vmem / MIB + 8),
        name="in_proj_even",
    )(x_prompt, w_nat, w_tr)


def _mix_ffn_kernel(oa_ref, ob_ref, x_ref, wo_ref, g1_ref, b1_ref, w1_ref, w2_ref, g2_ref, b2_ref,
                    y_ref):
    tile = x_ref.shape[0]
    group_rows = min(FFN_GROUP_ROWS, tile)
    groups = [slice(r, r + group_rows) for r in range(0, tile, group_rows)]
    mixed = [_dot(oa_ref[rows, :].astype(BF16), wo_ref[:HALF_W, :])
             + _dot(ob_ref[rows, :].astype(BF16), wo_ref[HALF_W:, :]) for rows in groups]
    for rows, mix in zip(groups, mixed):
        x = _layer_norm(ALPHA * x_ref[rows, :] + mix, g1_ref[...], b1_ref[...])
        xb = x.astype(BF16)
        acc = jnp.zeros(x.shape, F32)
        for c in range(D_FF // FF_CHUNK):
            sl = slice(c * FF_CHUNK, (c + 1) * FF_CHUNK)
            hid = jnp.maximum(_dot(xb, w1_ref[:, sl]), 0.0)
            acc = acc + _dot((hid * hid).astype(BF16), w2_ref[sl, :])
        y_ref[rows, :] = _layer_norm(ALPHA * x + acc, g2_ref[...], b2_ref[...])


def _mix_ffn(oa, ob, x, wo_bf16, g1, b1, w1_bf16, w2_bf16, g2, b2, tile):
    n, d = x.shape
    row = lambda i: (i, 0)
    vec = lambda v: v.reshape(1, d)
    vmem = (2 * (2 * tile * HALF_W * oa.dtype.itemsize + 2 * tile * d * 4)
            + (d * d + 2 * d * D_FF) * 2
            + 8 * min(tile, FFN_GROUP_ROWS) * FF_CHUNK * 4)
    return pl.pallas_call(
        _mix_ffn_kernel,
        grid=(n // tile,),
        in_specs=[pl.BlockSpec((tile, HALF_W), row), pl.BlockSpec((tile, HALF_W), row),
                  pl.BlockSpec((tile, d), row), _resident((d, d)),
                  _resident((1, d)), _resident((1, d)),
                  _resident((d, D_FF)), _resident((D_FF, d)),
                  _resident((1, d)), _resident((1, d))],
        out_specs=pl.BlockSpec((tile, d), row),
        out_shape=jax.ShapeDtypeStruct((n, d), F32),
        compiler_params=_params(("parallel",), vmem / MIB + 8),
        name="mix_ffn_ln",
    )(oa, ob, x, wo_bf16, vec(g1), vec(b1), w1_bf16, w2_bf16, vec(g2), vec(b2))


def _moba_kernel(pt_ref, qt_ref, k_ref, vt_ref, qs_ref, kn_ref, vn_ref, *refs):
    del pt_ref
    kp_refs = refs[:SAMPLE_PAGES_PER_TILE]
    vp_refs = refs[SAMPLE_PAGES_PER_TILE:2 * SAMPLE_PAGES_PER_TILE]
    (o_ref, os_ref, kmt_scr, qtz_scr, selb_scr, m_scr, acc_scr, s_scr, ref_scr, a_scr,
     sm_scr, sl_scr, sgate_scr, sacc_scr) = refs[2 * SAMPLE_PAGES_PER_TILE:]
    j = pl.program_id(1)
    nb = k_ref.shape[0]
    blk = MOBA_BLOCK
    pair_w = 2 * A_HD
    w = A_HEADS * A_HD
    nrow = DEC_SEQ * A_HEADS
    part = (pl.program_id(0) * nb + j) % TILES_PER_SAMPLE
    wide = lambda c: jnp.broadcast_to(c, (nrow, LANES))

    @pl.when(j == 0)
    def _():
        rows = [jnp.sum(k_ref[n].astype(F32), axis=0, keepdims=True) * (1.0 / blk)
                for n in range(nb)]
        kmean = jnp.concatenate(rows, axis=0)
        tiled = jnp.concatenate([kmean] * A_HEADS, axis=0)
        rh = lax.broadcasted_iota(jnp.int32, tiled.shape, 0) // nb
        ch = lax.broadcasted_iota(jnp.int32, tiled.shape, 1) // A_HD
        kmt_scr[...] = jnp.where(rh == ch, tiled, 0.0)

    head_mask = (lax.broadcasted_iota(jnp.int32, (A_HEADS, w), 1) // A_HD
                 == lax.broadcasted_iota(jnp.int32, (A_HEADS, w), 0))
    qs = qs_ref[...]
    qbd = jnp.concatenate(
        [jnp.where(head_mask, jnp.broadcast_to(qs[i:i + 1, :], (A_HEADS, w)), 0.0)
         for i in range(DEC_SEQ)], axis=0).astype(BF16)
    page_scores = [_dot(qbd, kp_refs[i][...].reshape(w, PAGE_SIZE).astype(BF16))
                   for i in range(SAMPLE_PAGES_PER_TILE)]

    def sample_blocks():
        probs = []
        for bl in range(SAMPLE_PAGES_PER_TILE // PAGES_PER_BLOCK):
            s = jnp.concatenate(page_scores[bl * PAGES_PER_BLOCK:(bl + 1) * PAGES_PER_BLOCK], axis=1)
            blk_id = part * (SAMPLE_PAGES_PER_TILE // PAGES_PER_BLOCK) + bl
            m = jnp.max(s, axis=-1, keepdims=True)
            p = jnp.exp2(s - m)
            sm_scr[blk_id] = wide(m)
            sl_scr[blk_id] = wide(jnp.sum(p, axis=-1, keepdims=True))
            sgate_scr[blk_id] = wide(jnp.sum(s, axis=-1, keepdims=True) * (1.0 / MOBA_BLOCK))
            probs.append(p.astype(BF16))
        for bl, p in enumerate(probs):
            blk_id = part * (SAMPLE_PAGES_PER_TILE // PAGES_PER_BLOCK) + bl
            acc = None
            for r in range(PAGES_PER_BLOCK):
                vpage = vp_refs[bl * PAGES_PER_BLOCK + r][...].reshape(w, PAGE_SIZE).astype(BF16)
                pv = _dot_nt(p[:, r * PAGE_SIZE:(r + 1) * PAGE_SIZE], vpage)
                acc = pv if acc is None else acc + pv
            sacc_scr[blk_id] = acc

    qts = qt_ref[...]
    k0, k1, k2 = _split3(kmt_scr[...])
    gate = _dot(k0, qts) + _dot(k1, qts) + _dot(k2, qts)

    n_io = lax.broadcasted_iota(jnp.int32, (nb, blk), 0)
    half = lax.broadcasted_iota(jnp.int32, (pair_w, blk), 0) // A_HD
    for h in range(A_HEADS):
        g = jnp.where(n_io < j, gate[h * nb:(h + 1) * nb, :], NEG)
        bias = jnp.full((nb, blk), NEG, F32)
        for _ in range(MOBA_TOPK):
            mx = jnp.max(g, axis=0, keepdims=True)
            cand = jnp.where((g == mx) & (mx > 0.5 * NEG), n_io, nb)
            pick = n_io == jnp.min(cand, axis=0, keepdims=True)
            bias = jnp.where(pick, 0.0, bias)
            g = jnp.where(pick, NEG, g)
        selb_scr[h * nb:(h + 1) * nb, :] = bias
        pr = h // 2
        qpair = qts[pr * pair_w:(pr + 1) * pair_w, :]
        qtz_scr[h] = jnp.where(half == (h % 2), qpair, jnp.zeros_like(qpair))

    causal = (lax.broadcasted_iota(jnp.int32, (blk, blk), 0)
              <= lax.broadcasted_iota(jnp.int32, (blk, blk), 1))

    def fold_rows(x, op):
        return op(x.reshape(blk // SUBLANES, SUBLANES, blk), axis=0)

    def stage_scores(n, h, slot, own):
        pr = h // 2
        s = _dot(k_ref[n, :, pr * pair_w:(pr + 1) * pair_w], qtz_scr[h])
        if own:
            s = jnp.where(causal, s, NEG)
        s_scr[slot] = s
        col_max = jnp.max(fold_rows(s, jnp.max), axis=0, keepdims=True)
        if own:
            m_scr[h] = col_max
            ref_scr[slot] = col_max
        else:
            bias = selb_scr[pl.ds(h * nb + n, 1), :]
            m_old = m_scr[h]
            m_new = jnp.maximum(m_old, col_max + bias)
            m_scr[h] = m_new
            a_scr[slot] = jnp.exp2(m_old - m_new)
            ref_scr[slot] = m_new - bias

    def accumulate(n, h, slot, own):
        vs = slice(h * V_SLAB, (h + 1) * V_SLAB)
        p = jnp.exp2(s_scr[slot] - ref_scr[slot])
        pv = _dot(vt_ref[n, vs, :], p.astype(BF16))
        if own:
            acc_scr[vs, :] = pv
        else:
            acc_scr[vs, :] = a_scr[slot] * acc_scr[vs, :] + pv

    n_slots = MOBA_LOOKAHEAD + 1

    def stage_ahead(n, n_next, h, own):
        ha = h + MOBA_LOOKAHEAD
        if ha < A_HEADS:
            stage_scores(n, ha, ha % n_slots, own=own)
        else:
            stage_scores(n_next, ha - A_HEADS, ha % n_slots, own=False)

    for h in range(MOBA_LOOKAHEAD):
        stage_scores(j, h, h % n_slots, own=True)
    sample_blocks()
    for h in range(A_HEADS):
        stage_ahead(j, 0, h, own=True)
        accumulate(j, h, h % n_slots, own=True)

    def past_block(n, carry):
        for h in range(A_HEADS):
            stage_ahead(n, n + 1, h, own=False)
            accumulate(n, h, h % n_slots, own=False)
        return carry

    lax.fori_loop(0, j, past_block, 0)

    heads_out = []
    for h in range(A_HEADS):
        slab = acc_scr[h * V_SLAB:(h + 1) * V_SLAB, :]
        heads_out.append(slab[:A_HD, :] / slab[A_HD:A_HD + 1, :])
    o_ref[...] = jnp.concatenate(heads_out, axis=0).T.astype(o_ref.dtype)

    @pl.when(part == TILES_PER_SAMPLE - 1)
    def _():
        tile4 = lambda c: jnp.concatenate([c] * (w // LANES), axis=1)
        s = _dot_nt(qbd, kn_ref[...].astype(BF16))
        col = lax.broadcasted_iota(jnp.int32, s.shape, 1)
        qi = lax.broadcasted_iota(jnp.int32, s.shape, 0) // A_HEADS
        s = jnp.where(col <= qi, s, NEG)
        m_col = jnp.max(s, axis=-1, keepdims=True)
        p = jnp.exp2(s - m_col)
        m_own = wide(m_col)
        l_own = wide(jnp.sum(p, axis=-1, keepdims=True))
        o_own = _dot(p, vn_ref[...])

        for _ in range(MOBA_TOPK):
            best = jnp.full((nrow, LANES), NEG, F32)
            bidx = jnp.zeros((nrow, LANES), jnp.int32)
            for n in range(N_PAST_BLOCKS):
                gn = sgate_scr[n]
                upd = gn > best
                best = jnp.where(upd, gn, best)
                bidx = jnp.where(upd, n, bidx)
            for n in range(N_PAST_BLOCKS):
                sgate_scr[n] = jnp.where(bidx == n, -jnp.inf, sgate_scr[n])

        m_all = m_own
        for n in range(N_PAST_BLOCKS):
            m_all = jnp.maximum(m_all, jnp.where(sgate_scr[n] == -jnp.inf, sm_scr[n], NEG))
        w_own = jnp.exp2(m_own - m_all)
        l_all = w_own * l_own
        o_all = tile4(w_own) * o_own
        for n in range(N_PAST_BLOCKS):
            wn = jnp.where(sgate_scr[n] == -jnp.inf, jnp.exp2(sm_scr[n] - m_all), 0.0)
            l_all = l_all + wn * sl_scr[n]
            o_all = o_all + tile4(wn) * sacc_scr[n]
        out = o_all / tile4(l_all)
        for i in range(DEC_SEQ):
            rows = out[i * A_HEADS:(i + 1) * A_HEADS, :]
            os_ref[i:i + 1, :] = jnp.sum(jnp.where(head_mask, rows, 0.0), axis=0, keepdims=True)


def _moba(page_table, qt, kb, vt, q_s, k_new, v_new, cache_kt, cache_vt):
    bsz, nb = qt.shape[:2]
    w = A_HEADS * A_HD
    nrow = DEC_SEQ * A_HEADS
    n_slots = MOBA_LOOKAHEAD + 1
    step = lambda b, j: b * nb + j
    sample_of = lambda b, j: step(b, j) // TILES_PER_SAMPLE
    first_page = lambda b, j: (step(b, j) % TILES_PER_SAMPLE) * SAMPLE_PAGES_PER_TILE
    page_spec = lambda i: pl.BlockSpec(
        (None, A_HEADS, A_HD, PAGE_SIZE),
        lambda b, j, pt: (pt[sample_of(b, j), first_page(b, j) + i], 0, 0, 0))
    per_sample = lambda rows: pl.BlockSpec((None, rows, w), lambda b, j, pt: (sample_of(b, j), 0, 0))
    whole_row = lambda shape: pl.BlockSpec((None,) + shape, lambda b, j, pt: (b, 0, 0, 0),
                                           pipeline_mode=pl.Buffered(1))
    grid_spec = pltpu.PrefetchScalarGridSpec(
        num_scalar_prefetch=1,
        grid=(bsz, nb),
        in_specs=([pl.BlockSpec((None, None, w, MOBA_BLOCK), lambda b, j, pt: (b, j, 0, 0)),
                   whole_row((nb, MOBA_BLOCK, w)), whole_row((nb, V_ROWS, MOBA_BLOCK)),
                   per_sample(DEC_SEQ), per_sample(SUBLANES), per_sample(SUBLANES)]
                  + [page_spec(i) for i in range(SAMPLE_PAGES_PER_TILE)]
                  + [page_spec(i) for i in range(SAMPLE_PAGES_PER_TILE)]),
        out_specs=[pl.BlockSpec((MOBA_BLOCK, w), lambda b, j, pt: (step(b, j), 0)),
                   per_sample(DEC_SEQ)],
        scratch_shapes=[pltpu.VMEM((A_HEADS * nb, w), F32),
                        pltpu.VMEM((A_HEADS, 2 * A_HD, MOBA_BLOCK), BF16),
                        pltpu.VMEM((A_HEADS * nb, MOBA_BLOCK), F32),
                        pltpu.VMEM((A_HEADS, 1, MOBA_BLOCK), F32),
                        pltpu.VMEM((V_ROWS, MOBA_BLOCK), F32),
                        pltpu.VMEM((n_slots, MOBA_BLOCK, MOBA_BLOCK), F32),
                        pltpu.VMEM((n_slots, 1, MOBA_BLOCK), F32),
                        pltpu.VMEM((n_slots, 1, MOBA_BLOCK), F32),
                        pltpu.VMEM((N_PAST_BLOCKS, nrow, LANES), F32),
                        pltpu.VMEM((N_PAST_BLOCKS, nrow, LANES), F32),
                        pltpu.VMEM((N_PAST_BLOCKS, nrow, LANES), F32),
                        pltpu.VMEM((N_PAST_BLOCKS, nrow, w), F32)],
    )
    vmem = (nb * MOBA_BLOCK * (w + V_ROWS) * 2 + 2 * (w * MOBA_BLOCK * 2 + MOBA_BLOCK * w * 2)
            + n_slots * MOBA_BLOCK * MOBA_BLOCK * 4
            + 2 * 2 * SAMPLE_PAGES_PER_TILE * PAGE_SIZE * w * 4
            + N_PAST_BLOCKS * nrow * (3 * LANES + w) * 4 + 4 * MIB)
    return pl.pallas_call(
        _moba_kernel,
        grid_spec=grid_spec,
        out_shape=[jax.ShapeDtypeStruct((bsz * nb * MOBA_BLOCK, w), BF16),
                   jax.ShapeDtypeStruct((DEC_BATCH, DEC_SEQ, w), F32)],
        compiler_params=_params(("arbitrary", "arbitrary"), vmem / MIB + 6),
        name="moba",
    )(page_table, qt, kb, vt, q_s, k_new, v_new,
      *([cache_kt] * SAMPLE_PAGES_PER_TILE), *([cache_vt] * SAMPLE_PAGES_PER_TILE))


def _gmlp_kernel(u_ref, gv_ref, w_ref, bias_ref, avg_ref, lng_ref, lnb_ref, ob_ref, *vn_refs):
    chunks = [slice(c * B_CHUNK, (c + 1) * B_CHUNK) for c in range(u_ref.shape[0] // B_CHUNK)]
    avg = avg_ref[...]
    gvs = [_gelu_tanh(gv_ref[rows, :]) for rows in chunks]
    centred = [gv - _dot_exact_rhs(gv, avg) for gv in gvs]
    variances = [_dot_exact_rhs(gc * gc, avg, pieces=2) for gc in centred]
    pair_w = 2 * B_GD
    lane = lax.broadcasted_iota(jnp.int32, (B_CHUNK, pair_w), 1)
    for rows, gc, var in zip(chunks, centred, variances):
        vn = gc * lax.rsqrt(var + LN_EPS) * lng_ref[...] + lnb_ref[...]
        if vn_refs:
            vn_refs[0][rows, :] = vn
        vb = vn.astype(BF16)
        u = _gelu_tanh(u_ref[rows, :])
        for pr in range(B_GROUPS // 2):
            sl = slice(pr * pair_w, (pr + 1) * pair_w)
            vp = vb[:, sl]
            zero = jnp.zeros_like(vp)
            mixed = (_dot(w_ref[2 * pr], jnp.where(lane < B_GD, vp, zero))
                     + _dot(w_ref[2 * pr + 1], jnp.where(lane >= B_GD, vp, zero)))
            ob_ref[rows, sl] = (u[:, sl] * (mixed + bias_ref[:, sl])).astype(ob_ref.dtype)


def _gmlp(src, u_col, gv_col, w_masked, bias, avg, ln_g, ln_b, *, out_dtype, emit_vn):
    rows = src.shape[0]
    w = B_GROUPS * B_GD
    tile = B_CHUNK * min(GMLP_GROUP, rows // B_CHUNK)
    out_shape = [jax.ShapeDtypeStruct((rows, w), out_dtype)]
    out_specs = [pl.BlockSpec((tile, w), lambda c: (c, 0))]
    if emit_vn:
        out_shape.append(jax.ShapeDtypeStruct((rows, w), F32))
        out_specs.append(pl.BlockSpec((tile, w), lambda c: (c, 0)))
    return pl.pallas_call(
        _gmlp_kernel,
        grid=(rows // tile,),
        in_specs=[pl.BlockSpec((tile, w), lambda c: (c, u_col)),
                  pl.BlockSpec((tile, w), lambda c: (c, gv_col)),
                  _resident((B_GROUPS, B_CHUNK, B_CHUNK)), _resident((B_CHUNK, w)),
                  _resident((w, w)), _resident((1, w)), _resident((1, w))],
        out_specs=out_specs,
        out_shape=out_shape,
        compiler_params=_params(("parallel",), 24),
        name="gmlp_gate",
    )(src, src, w_masked, bias, avg, ln_g.reshape(1, w), ln_b.reshape(1, w))


def _pool_tile(halo, x, pos_first, w_ref, sc_ref, y_ref):
    rows = x.shape[0]
    ext = jnp.concatenate([halo, x], axis=0)
    pos = pos_first + lax.broadcasted_iota(jnp.int32, (rows, C_GD), 0)
    for gi, win in enumerate(POOL_WINDOWS):
        sl = slice(gi * C_GD, (gi + 1) * C_GD)
        s = ext[:, sl]
        sh = 1
        while sh < win:
            s = s + pltpu.roll(s, sh, 0)
            sh *= 2
        cnt = jnp.minimum(win, pos + 1).astype(F32)
        pooled = s[POOL_HALO:, :] / cnt - x[:, sl]
        y_ref[:, sl] = (_dot(pooled.astype(BF16), w_ref[gi]) * sc_ref[:, sl]).astype(y_ref.dtype)


def _pool_kernel(halo_ref, x_ref, w_ref, sc_ref, y_ref, *, pos0):
    _pool_tile(halo_ref[...], x_ref[...], pos0, w_ref, sc_ref, y_ref)


def _proj_pool_kernel(xh_ref, x_ref, w_ref, pw_ref, sc_ref, h_ref, y_ref, *, tiles_per_seq):
    t = pl.program_id(0) % tiles_per_seq
    rows = x_ref.shape[0]
    xb = x_ref[...].astype(BF16)
    xc = _dot(xb, w_ref[:, :HALF_W])
    h_ref[:, :HALF_W] = xc
    for c in range(1, w_ref.shape[1] // HALF_W):
        sl = slice(c * HALF_W, (c + 1) * HALF_W)
        h_ref[:, sl] = _dot(xb, w_ref[:, sl])
    halo = _dot(xh_ref[...].astype(BF16), w_ref[:, :HALF_W])
    _pool_tile(jnp.where(t == 0, 0.0, halo), xc, t * rows, pw_ref, sc_ref, y_ref)


def _pool(halo_src, x_src, w_bf16, scale, *, rows, pos0):
    w = len(POOL_WINDOWS) * C_GD
    n_rows = x_src.shape[0]
    return pl.pallas_call(
        functools.partial(_pool_kernel, pos0=pos0),
        grid=(n_rows // rows,),
        in_specs=[pl.BlockSpec((POOL_HALO, w), lambda i: (i, 0)),
                  pl.BlockSpec((rows, w), lambda i: (i, 0)),
                  _resident((len(POOL_WINDOWS), C_GD, C_GD)), _resident((1, w))],
        out_specs=pl.BlockSpec((rows, w), lambda i: (i, 0)),
        out_shape=jax.ShapeDtypeStruct((n_rows, w), F32),
        compiler_params=_params(("parallel",), 24),
        name="pool_mix",
    )(halo_src, x_src, w_bf16, scale.reshape(1, w))


def _proj_pool(x, w_bf16, pool_w_bf16, scale):
    n, d = x.shape
    n_out = w_bf16.shape[1]
    w = len(POOL_WINDOWS) * C_GD
    tile = PROMPT_TILE
    halo_step = tile // POOL_HALO
    vmem = (2 * (tile * d * 4 + POOL_HALO * d * 4 + tile * n_out * 4 + tile * w * 2)
            + d * n_out * 2 + 6 * tile * w * 4)
    return pl.pallas_call(
        functools.partial(_proj_pool_kernel, tiles_per_seq=SEQ // tile),
        grid=(n // tile,),
        in_specs=[pl.BlockSpec((POOL_HALO, d), lambda i: (jnp.maximum(i * halo_step - 1, 0), 0)),
                  pl.BlockSpec((tile, d), lambda i: (i, 0)),
                  _resident((d, n_out)),
                  _resident((len(POOL_WINDOWS), C_GD, C_GD)), _resident((1, w))],
        out_specs=[pl.BlockSpec((tile, n_out), lambda i: (i, 0)),
                   pl.BlockSpec((tile, w), lambda i: (i, 0))],
        out_shape=[jax.ShapeDtypeStruct((n, n_out), F32), jax.ShapeDtypeStruct((n, w), BF16)],
        compiler_params=_params(("parallel",), vmem / MIB + 8),
        name="in_proj_pool",
    )(x, x, w_bf16, pool_w_bf16, scale.reshape(1, w))


def _hgrn_kernel(q_ref, f_ref, i_ref, g_ref, s0_ref, lb_ref, ng_ref, o_ref, sfin_ref, s_scr,
                 *, rows, in_rows, valid):
    t = pl.program_id(1)
    c_rows = HGRN_CHUNK

    @pl.when(t == 0)
    def _():
        s_scr[...] = s0_ref[...]

    r_io = lax.broadcasted_iota(jnp.int32, (c_rows, c_rows), 0)
    c_io = lax.broadcasted_iota(jnp.int32, (c_rows, c_rows), 1)
    causal = r_io >= c_io
    ltri = jnp.where(causal, 1.0, 0.0).astype(BF16)
    lgrp = jnp.where(c_io < (r_io // HGRN_SUB) * HGRN_SUB + HGRN_SUB // 2, 1.0, 0.0).astype(BF16)
    lsum = jnp.concatenate([ltri, lgrp], axis=0)
    eye = r_io == c_io
    row_id = lax.broadcasted_iota(jnp.int32, (c_rows, D_HK), 0)

    heads = [slice(hd * D_HK, (hd + 1) * D_HK) for hd in range(D_HEADS)]

    def load(ref, r0, sl):
        if in_rows == rows:
            return ref[pl.ds(r0, c_rows), sl]
        return jnp.concatenate([ref[:, sl], jnp.zeros((c_rows - in_rows, D_HK), F32)], axis=0)

    def decay_sums(r0):
        gates = []
        for sl in heads:
            lb = lb_ref[:, sl]
            half_span = 0.5 * (1.0 - lb)
            f = (lb + half_span) + half_span * jnp.tanh(0.5 * load(f_ref, r0, sl))
            logf = jnp.log2(f)
            kk = 1.0 - f
            if valid < rows:
                live = (t * rows + r0 + row_id) < valid
                logf = jnp.where(live, logf, 0.0)
                kk = jnp.where(live, kk, 0.0)
            sums = _dot_exact_lhs(lsum, logf)
            gates.append((kk, sums[:c_rows, :], sums[c_rows:, :]))
        return gates

    def chunk_scores(r0, gates):
        attns, queries = [], []
        for sl, (kk, cg, ref) in zip(heads, gates):
            q = _silu(load(q_ref, r0, sl))
            qd = (q * jnp.exp2(cg - ref)).astype(BF16)
            blocks = []
            for i in range(c_rows // HGRN_SUB):
                ref_i = ref[i * HGRN_SUB:i * HGRN_SUB + 1, :]
                e = jnp.where(row_id < (i + 1) * HGRN_SUB, ref_i - cg, 0.0)
                k_i = (kk * jnp.exp2(e)).astype(BF16)
                blocks.append(_dot_nt(qd[i * HGRN_SUB:(i + 1) * HGRN_SUB, :], k_i))
            attns.append(jnp.where(causal, jnp.concatenate(blocks, axis=0), 0.0).astype(BF16))
            queries.append((q * jnp.exp2(cg)).astype(BF16))
        return attns, queries

    def advance_state(r0, gates, attns, queries):
        outs = []
        for hd, (sl, (kk, cg, _)) in enumerate(zip(heads, gates)):
            vb = load(i_ref, r0, sl).astype(BF16)
            state = s_scr[hd]
            outs.append(_dot(queries[hd], state.astype(BF16)) + _dot(attns[hd], vb))
            g_last = cg[c_rows - 1:c_rows, :]
            kd = kk * jnp.exp2(g_last - cg)
            decay_col = jnp.sum(
                jnp.where(eye, jnp.broadcast_to(jnp.exp2(g_last), (c_rows, D_HK)), 0.0),
                axis=1, keepdims=True)
            s_scr[hd] = state * decay_col + _dot(kd.T.astype(BF16), vb)
        return outs

    def finish(r0, outs):
        for sl, o in zip(heads, outs):
            o = o * lax.rsqrt(jnp.mean(o * o, axis=-1, keepdims=True) + RMS_EPS) * ng_ref[...]
            o = (o * _silu(load(g_ref, r0, sl))).astype(o_ref.dtype)
            if in_rows == rows:
                o_ref[pl.ds(r0, c_rows), sl] = o
            else:
                o_ref[:, sl] = o[:in_rows, :]

    group = min(HGRN_GROUP, rows // c_rows)

    def chunk_group(c, carry):
        starts = [pl.multiple_of((c * group + k) * c_rows, c_rows) for k in range(group)]
        gates = [decay_sums(r0) for r0 in starts]
        outs = []
        for r0, gt in zip(starts, gates):
            attns, queries = chunk_scores(r0, gt)
            outs.append(advance_state(r0, gt, attns, queries))
        for r0, o in zip(starts, outs):
            finish(r0, o)
        return carry

    lax.fori_loop(0, rows // (c_rows * group), chunk_group, 0)

    @pl.when(t == pl.num_programs(1) - 1)
    def _():
        sfin_ref[...] = s_scr[...]


def _hgrn(src, s0, lb, norm_g, *, bsz, length, rows, in_rows, valid, out_dtype):
    w = D_HEADS * D_HK
    tiles = length // rows
    col = lambda k: pl.BlockSpec((in_rows, w), lambda b, t: (b * tiles + t, 1 + k))
    state_spec = pl.BlockSpec((None, D_HEADS, D_HK, D_HK), lambda b, t: (b, 0, 0, 0))
    return pl.pallas_call(
        functools.partial(_hgrn_kernel, rows=rows, in_rows=in_rows, valid=valid),
        grid=(bsz, tiles),
        in_specs=[col(0), col(1), col(2), col(3), state_spec, _resident((1, w)),
                  _resident((1, D_HK))],
        out_specs=[pl.BlockSpec((in_rows, w), lambda b, t: (b * tiles + t, 0)), state_spec],
        out_shape=[jax.ShapeDtypeStruct((bsz * tiles * in_rows, w), out_dtype),
                   jax.ShapeDtypeStruct((bsz, D_HEADS, D_HK, D_HK), F32)],
        scratch_shapes=[pltpu.VMEM((D_HEADS, D_HK, D_HK), F32)],
        compiler_params=_params(("parallel", "arbitrary"), 32),
        name="hgrn2",
    )(src, src, src, src, s0, lb.reshape(1, w), norm_g.reshape(1, D_HK))


def kernel(x_prompt, x_sample, cache_k, cache_v, state_pool, state_hgrn, page_table, w_in_even, w_out_even, gmlp_ws, gmlp_bs, gmlp_ln_g, gmlp_ln_b, w_in_odd, w_out_odd, pool_w, pool_scale, hgrn_lb_param, hgrn_norm_g, ln_mix_g, ln_mix_b, ln_ffn_g, ln_ffn_b, ffn_w1, ffn_w2):
    w = HALF_W
    xp = x_prompt.reshape(N_PROMPT, D_MODEL)
    xs = x_sample.reshape(N_SAMPLE, D_MODEL)
    pad_tokens = ((0, 0), (0, SUBLANES - DEC_SEQ), (0, 0))

    col_scale = jnp.where(jnp.arange(EVEN_IN) < w, SCORE_SCALE, 1.0)
    w_even = (w_in_even[0] * col_scale).astype(BF16)
    kb, ugv, qt, vt, kt32, vt32 = _proj_even_prompt(x_prompt, w_even, w_even[:, :3 * w].T)
    h0s = _proj(xs, w_even, SAMPLE_TILE)
    hs = h0s.reshape(DEC_BATCH, DEC_SEQ, EVEN_IN)
    oa_prompt, oa_sample = _moba(page_table, qt, kb, vt, hs[..., :w],
                                 jnp.pad(hs[..., w:2 * w], pad_tokens), jnp.pad(hs[..., 2 * w:3 * w], pad_tokens),
                                 cache_k[0].transpose(0, 2, 3, 1), cache_v[0].transpose(0, 2, 3, 1))

    tril = jnp.tril(jnp.ones((B_CHUNK, B_CHUNK), bool))
    ws_prompt = jnp.where(tril[None], gmlp_ws[0], 0.0)
    owner = jnp.arange(B_CHUNK) // DEC_SEQ
    ws_sample = jnp.where(owner[:, None] == owner[None, :],
                          jnp.tile(ws_prompt[:, :DEC_SEQ, :DEC_SEQ], (1, DEC_BATCH, DEC_BATCH)), 0.0)
    bias_prompt = jnp.repeat(gmlp_bs[0].T, B_GD, axis=1)
    bias_sample = jnp.tile(bias_prompt[:DEC_SEQ], (DEC_BATCH, 1))
    grp = jnp.arange(w) // B_GD
    avg = jnp.where(grp[:, None] == grp[None, :], 1.0 / B_GD, 0.0).astype(BF16)
    ob_prompt, = _gmlp(ugv, 0, 1, ws_prompt.astype(BF16), bias_prompt, avg, gmlp_ln_g[0], gmlp_ln_b[0],
                       out_dtype=BF16, emit_vn=False)
    ob_sample, vn_sample = _gmlp(h0s, 3, 4, ws_sample.astype(BF16), bias_sample, avg, gmlp_ln_g[0],
                                 gmlp_ln_b[0], out_dtype=F32, emit_vn=True)

    layer0 = (w_out_even[0].astype(BF16), ln_mix_g[0], ln_mix_b[0],
              ffn_w1[0].astype(BF16), ffn_w2[0].astype(BF16), ln_ffn_g[0], ln_ffn_b[0])
    xp = _mix_ffn(oa_prompt, ob_prompt, xp, *layer0, FFN_TILE)
    xs = _mix_ffn(oa_sample.reshape(N_SAMPLE, w), ob_sample, xs, *layer0, SAMPLE_TILE)

    per_head = lambda t: t.reshape(BATCH, A_HEADS, A_HD, SEQ).transpose(0, 3, 1, 2)[None]
    new_k_prompt = per_head(kt32)
    new_v_prompt = per_head(vt32)
    new_k_sample = hs[..., w:2 * w].reshape(1, DEC_BATCH, DEC_SEQ, A_HEADS, A_HD)
    new_v_sample = hs[..., 2 * w:3 * w].reshape(1, DEC_BATCH, DEC_SEQ, A_HEADS, A_HD)
    new_gmlp_v_sample = vn_sample.reshape(1, DEC_BATCH, DEC_SEQ, w)

    w_odd = w_in_odd[0].astype(BF16)
    lb = jax.nn.softmax(hgrn_lb_param.astype(F32), axis=0)[0]
    pool_wb = pool_w[0].astype(BF16)

    h1p, oc_prompt = _proj_pool(xp, w_odd, pool_wb, pool_scale[0])
    od_prompt, s_prompt = _hgrn(h1p, jnp.zeros((BATCH, D_HEADS, D_HK, D_HK), F32), lb, hgrn_norm_g[0],
                                bsz=BATCH, length=SEQ, rows=HGRN_ROWS, in_rows=HGRN_ROWS, valid=SEQ,
                                out_dtype=BF16)

    h1s = _proj(xs, w_odd, SAMPLE_TILE).reshape(DEC_BATCH, DEC_SEQ, ODD_IN)
    h1s_pad = jnp.pad(h1s, pad_tokens).reshape(DEC_BATCH * SUBLANES, ODD_IN)
    halo_s = jnp.pad(state_pool[0], ((0, 0), (POOL_HALO - POOL_BUF, 0), (0, 0)))
    oc_sample = _pool(halo_s.reshape(DEC_BATCH * POOL_HALO, w), h1s_pad, pool_wb, pool_scale[0],
                      rows=SUBLANES, pos0=PAST_LEN)
    od_sample, s_sample = _hgrn(h1s_pad, state_hgrn[0], lb, hgrn_norm_g[0], bsz=DEC_BATCH,
                                length=HGRN_CHUNK, rows=HGRN_CHUNK, in_rows=SUBLANES, valid=DEC_SEQ,
                                out_dtype=F32)
    real_rows = lambda t: t.reshape(DEC_BATCH, SUBLANES, w)[:, :DEC_SEQ].reshape(N_SAMPLE, w)

    layer1 = (w_out_odd[0].astype(BF16), ln_mix_g[1], ln_mix_b[1],
              ffn_w1[1].astype(BF16), ffn_w2[1].astype(BF16), ln_ffn_g[1], ln_ffn_b[1])
    xp = _mix_ffn(oc_prompt, od_prompt, xp, *layer1, FFN_TILE)
    xs = _mix_ffn(real_rows(oc_sample), real_rows(od_sample), xs, *layer1, SAMPLE_TILE)

    new_pool_prompt = h1p.reshape(BATCH, SEQ, ODD_IN)[:, SEQ - POOL_BUF:, :w][None]
    new_pool_sample = jnp.concatenate([state_pool[0][:, DEC_SEQ:], h1s[..., :w]], axis=1)[None]
    return (xp.reshape(BATCH, SEQ, D_MODEL), xs.reshape(DEC_BATCH, DEC_SEQ, D_MODEL),
            new_k_prompt, new_v_prompt, new_k_sample, new_v_sample, new_gmlp_v_sample,
            new_pool_prompt, new_pool_sample, s_prompt[None], s_sample[None])
```

```python
import functools
import math

import jax
import jax.numpy as jnp
from jax import lax
from jax.experimental import pallas as pl
from jax.experimental.pallas import tpu as pltpu

F32 = jnp.float32
BF16 = jnp.bfloat16

D_MODEL = 1024
BATCH = 4
SEQ = 4096
DEPTH = 2
DEC_BATCH = 32
DEC_SEQ = 4
PAST_LEN = 8192
PAGE_SIZE = 128
HALF_W = D_MODEL // 2
A_HD = 64
A_HEADS = HALF_W // A_HD
MOBA_BLOCK = 256
MOBA_TOPK = 3
B_GROUPS = 8
B_GD = HALF_W // B_GROUPS
B_CHUNK = 128
POOL_WINDOWS = (2, 4, 8, 16)
C_GD = HALF_W // len(POOL_WINDOWS)
POOL_BUF = max(POOL_WINDOWS) - 1
D_HK = 128
D_HEADS = HALF_W // D_HK
D_FF = 4 * D_MODEL
EVEN_IN = 5 * HALF_W
ODD_IN = 5 * HALF_W
ALPHA = (2 * DEPTH) ** 0.25
LN_EPS = 1e-5
RMS_EPS = 1e-6
NEG = -1e30

N_PROMPT = BATCH * SEQ
N_SAMPLE = DEC_BATCH * DEC_SEQ
N_PAGES = PAST_LEN // PAGE_SIZE
N_PAST_BLOCKS = PAST_LEN // MOBA_BLOCK
PAGES_PER_BLOCK = MOBA_BLOCK // PAGE_SIZE
N_KEY_BLOCKS = SEQ // MOBA_BLOCK

LANES = 128
SUBLANES = 8
PROMPT_TILE = 1024
SAMPLE_TILE = N_SAMPLE
FF_CHUNK = 1024
FFN_TILE = 1024
FFN_GROUP_ROWS = 256
MOBA_LOOKAHEAD = 7
SAMPLE_PAGES_PER_TILE = DEC_BATCH * N_PAGES // (BATCH * N_KEY_BLOCKS)
TILES_PER_SAMPLE = N_PAGES // SAMPLE_PAGES_PER_TILE
assert SAMPLE_PAGES_PER_TILE * BATCH * N_KEY_BLOCKS == DEC_BATCH * N_PAGES
assert TILES_PER_SAMPLE * SAMPLE_PAGES_PER_TILE == N_PAGES and N_KEY_BLOCKS % TILES_PER_SAMPLE == 0
assert SAMPLE_PAGES_PER_TILE % PAGES_PER_BLOCK == 0
V_TAIL = 16
V_SLAB = A_HD + V_TAIL
V_ROWS = A_HEADS * V_SLAB
assert A_HEADS % (MOBA_LOOKAHEAD + 1) == 0
SCORE_SCALE = (A_HD ** -0.5) * math.log2(math.e)
HGRN_CHUNK = 128
HGRN_SUB = 32
HGRN_ROWS = 512
HGRN_GROUP = 2
HGRN_SAMPLE_SEQS = 4
GMLP_GROUP = 4
POOL_HALO = 16
MIB = 1024 * 1024


def _dot(a, b):
    return jnp.dot(a, b, preferred_element_type=F32)


def _dot_nt(a, b):
    return lax.dot_general(a, b, (((1,), (1,)), ((), ())), preferred_element_type=F32)


def _split3(x):
    p0 = x.astype(BF16)
    r1 = x - p0.astype(F32)
    p1 = r1.astype(BF16)
    p2 = (r1 - p1.astype(F32)).astype(BF16)
    return p0, p1, p2


def _dot_exact_lhs(m_bf16, x):
    p0, p1, p2 = _split3(x)
    return _dot(m_bf16, p0) + _dot(m_bf16, p1) + _dot(m_bf16, p2)


def _dot_exact_rhs(x, m_bf16, pieces=3):
    parts = _split3(x)[:pieces]
    out = _dot(parts[0], m_bf16)
    for part in parts[1:]:
        out = out + _dot(part, m_bf16)
    return out


def _gelu_tanh(x):
    return 0.5 * x * (1.0 + jnp.tanh(0.7978845608028654 * (x + 0.044715 * (x * x * x))))


def _silu(x):
    h = 0.5 * x
    return h + h * jnp.tanh(h)


def _layer_norm(z, g, b):
    mu = jnp.mean(z, axis=-1, keepdims=True)
    zc = z - mu
    var = jnp.mean(zc * zc, axis=-1, keepdims=True)
    return zc * lax.rsqrt(var + LN_EPS) * g + b


def _params(sem, vmem_mib):
    return pltpu.CompilerParams(dimension_semantics=sem, vmem_limit_bytes=int(vmem_mib * MIB))


def _resident(shape):
    return pl.BlockSpec(shape, lambda *_: (0,) * len(shape), pipeline_mode=pl.Buffered(1))


def _proj_kernel(x_ref, w_ref, h_ref):
    xb = x_ref[...].astype(BF16)
    for c in range(w_ref.shape[1] // HALF_W):
        sl = slice(c * HALF_W, (c + 1) * HALF_W)
        h_ref[:, sl] = _dot(xb, w_ref[:, sl])


def _proj(x, w_bf16, tile):
    n, d = x.shape
    n_out = w_bf16.shape[1]
    vmem = 2 * (tile * d * 4 + tile * n_out * 4) + d * n_out * 2
    return pl.pallas_call(
        _proj_kernel,
        grid=(n // tile,),
        in_specs=[pl.BlockSpec((tile, d), lambda i: (i, 0)), _resident((d, n_out))],
        out_specs=pl.BlockSpec((tile, n_out), lambda i: (i, 0)),
        out_shape=jax.ShapeDtypeStruct((n, n_out), F32),
        compiler_params=_params(("parallel",), vmem / MIB + 8),
        name="in_proj",
    )(x, w_bf16)


def _proj_even_kernel(x_ref, wn_ref, wt_ref, kb_ref, ugv_ref, qt_ref, vt_ref, kt32_ref, vt32_ref):
    w = HALF_W
    xb = x_ref[...].astype(BF16)
    k = _dot(xb, wn_ref[:, w:2 * w])
    for half in range(PROMPT_TILE // MOBA_BLOCK):
        kb_ref[half] = k[half * MOBA_BLOCK:(half + 1) * MOBA_BLOCK, :].astype(BF16)
    ugv_ref[:, :w] = _dot(xb, wn_ref[:, 3 * w:4 * w])
    ugv_ref[:, w:] = _dot(xb, wn_ref[:, 4 * w:])
    qt = _dot_nt(wt_ref[:w, :], xb)
    kt32_ref[...] = k.T
    vt = _dot_nt(wt_ref[2 * w:, :], xb)
    vt32_ref[...] = vt
    ones_row = jnp.where(lax.broadcasted_iota(jnp.int32, (V_TAIL, MOBA_BLOCK), 0) == 0,
                         1.0, 0.0).astype(BF16)
    for half in range(PROMPT_TILE // MOBA_BLOCK):
        cols = slice(half * MOBA_BLOCK, (half + 1) * MOBA_BLOCK)
        qt_ref[half] = qt[:, cols].astype(BF16)
        for h in range(A_HEADS):
            vt_ref[half, h * V_SLAB:h * V_SLAB + A_HD, :] = vt[h * A_HD:(h + 1) * A_HD, cols].astype(BF16)
            vt_ref[half, h * V_SLAB + A_HD:(h + 1) * V_SLAB, :] = ones_row


def _proj_even_prompt(x_prompt, w_nat, w_tr):
    w = HALF_W
    tiles = SEQ // PROMPT_TILE
    per_tile = PROMPT_TILE // MOBA_BLOCK
    blocked = lambda shape: pl.BlockSpec((None, per_tile) + shape, lambda b, t: (b, t, 0, 0))
    vmem = (2 * (PROMPT_TILE * D_MODEL * 4 + PROMPT_TILE * 2 * w * 4 + 3 * PROMPT_TILE * w * 2
                 + 2 * PROMPT_TILE * w * 4) + 2 * D_MODEL * 3 * w * 2 + 4 * PROMPT_TILE * w * 4)
    return pl.pallas_call(
        _proj_even_kernel,
        grid=(BATCH, tiles),
        in_specs=[pl.BlockSpec((None, PROMPT_TILE, D_MODEL), lambda b, t: (b, t, 0)),
                  _resident((D_MODEL, EVEN_IN)), _resident((3 * w, D_MODEL))],
        out_specs=[blocked((MOBA_BLOCK, w)),
                   pl.BlockSpec((PROMPT_TILE, 2 * w), lambda b, t: (b * tiles + t, 0)),
                   blocked((w, MOBA_BLOCK)), blocked((V_ROWS, MOBA_BLOCK)),
                   pl.BlockSpec((None, w, PROMPT_TILE), lambda b, t: (b, 0, t)),
                   pl.BlockSpec((None, w, PROMPT_TILE), lambda b, t: (b, 0, t))],
        out_shape=[jax.ShapeDtypeStruct((BATCH, N_KEY_BLOCKS, MOBA_BLOCK, w), BF16),
                   jax.ShapeDtypeStruct((N_PROMPT, 2 * w), F32),
                   jax.ShapeDtypeStruct((BATCH, N_KEY_BLOCKS, w, MOBA_BLOCK), BF16),
                   jax.ShapeDtypeStruct((BATCH, N_KEY_BLOCKS, V_ROWS, MOBA_BLOCK), BF16),
                   jax.ShapeDtypeStruct((BATCH, w, SEQ), F32),
                   jax.ShapeDtypeStruct((BATCH, w, SEQ), F32)],
        compiler_params=_params(("parallel", "parallel"), vmem / MIB + 8),
        name="in_proj_even",
    )(x_prompt, w_nat, w_tr)


def _mix_ffn_kernel(oa_ref, ob_ref, x_ref, wo_ref, g1_ref, b1_ref, w1_ref, w2_ref, g2_ref, b2_ref,
                    y_ref):
    tile = x_ref.shape[0]
    group_rows = min(FFN_GROUP_ROWS, tile)
    groups = [slice(r, r + group_rows) for r in range(0, tile, group_rows)]
    mixed = [_dot(oa_ref[rows, :].astype(BF16), wo_ref[:HALF_W, :])
             + _dot(ob_ref[rows, :].astype(BF16), wo_ref[HALF_W:, :]) for rows in groups]
    for rows, mix in zip(groups, mixed):
        x = _layer_norm(ALPHA * x_ref[rows, :] + mix, g1_ref[...], b1_ref[...])
        xb = x.astype(BF16)
        acc = jnp.zeros(x.shape, F32)
        for c in range(D_FF // FF_CHUNK):
            sl = slice(c * FF_CHUNK, (c + 1) * FF_CHUNK)
            hid = jnp.maximum(_dot(xb, w1_ref[:, sl]), 0.0)
            acc = acc + _dot((hid * hid).astype(BF16), w2_ref[sl, :])
        y_ref[rows, :] = _layer_norm(ALPHA * x + acc, g2_ref[...], b2_ref[...])


def _mix_ffn(oa, ob, x, wo_bf16, g1, b1, w1_bf16, w2_bf16, g2, b2, tile):
    n, d = x.shape
    row = lambda i: (i, 0)
    vec = lambda v: v.reshape(1, d)
    vmem = (2 * (2 * tile * HALF_W * oa.dtype.itemsize + 2 * tile * d * 4)
            + (d * d + 2 * d * D_FF) * 2
            + 8 * min(tile, FFN_GROUP_ROWS) * FF_CHUNK * 4)
    return pl.pallas_call(
        _mix_ffn_kernel,
        grid=(n // tile,),
        in_specs=[pl.BlockSpec((tile, HALF_W), row), pl.BlockSpec((tile, HALF_W), row),
                  pl.BlockSpec((tile, d), row), _resident((d, d)),
                  _resident((1, d)), _resident((1, d)),
                  _resident((d, D_FF)), _resident((D_FF, d)),
                  _resident((1, d)), _resident((1, d))],
        out_specs=pl.BlockSpec((tile, d), row),
        out_shape=jax.ShapeDtypeStruct((n, d), F32),
        compiler_params=_params(("parallel",), vmem / MIB + 8),
        name="mix_ffn_ln",
    )(oa, ob, x, wo_bf16, vec(g1), vec(b1), w1_bf16, w2_bf16, vec(g2), vec(b2))


def _moba_kernel(pt_ref, qt_ref, k_ref, vt_ref, qs_ref, kn_ref, vn_ref, *refs):
    del pt_ref
    kp_refs = refs[:SAMPLE_PAGES_PER_TILE]
    vp_refs = refs[SAMPLE_PAGES_PER_TILE:2 * SAMPLE_PAGES_PER_TILE]
    (o_ref, os_ref, kmt_scr, qtz_scr, selb_scr, m_scr, acc_scr, s_scr, ref_scr, a_scr,
     sm_scr, sl_scr, sgate_scr, sacc_scr) = refs[2 * SAMPLE_PAGES_PER_TILE:]
    j = pl.program_id(1)
    nb = k_ref.shape[0]
    blk = MOBA_BLOCK
    pair_w = 2 * A_HD
    w = A_HEADS * A_HD
    nrow = DEC_SEQ * A_HEADS
    part = (pl.program_id(0) * nb + j) % TILES_PER_SAMPLE
    wide = lambda c: jnp.broadcast_to(c, (nrow, LANES))

    @pl.when(j == 0)
    def _():
        rows = [jnp.sum(k_ref[n].astype(F32), axis=0, keepdims=True) * (1.0 / blk)
                for n in range(nb)]
        kmean = jnp.concatenate(rows, axis=0)
        tiled = jnp.concatenate([kmean] * A_HEADS, axis=0)
        rh = lax.broadcasted_iota(jnp.int32, tiled.shape, 0) // nb
        ch = lax.broadcasted_iota(jnp.int32, tiled.shape, 1) // A_HD
        kmt_scr[...] = jnp.where(rh == ch, tiled, 0.0)

    head_mask = (lax.broadcasted_iota(jnp.int32, (A_HEADS, w), 1) // A_HD
                 == lax.broadcasted_iota(jnp.int32, (A_HEADS, w), 0))
    qs = qs_ref[...]
    qbd = jnp.concatenate(
        [jnp.where(head_mask, jnp.broadcast_to(qs[i:i + 1, :], (A_HEADS, w)), 0.0)
         for i in range(DEC_SEQ)], axis=0).astype(BF16)
    page_scores = [_dot(qbd, kp_refs[i][...].reshape(w, PAGE_SIZE).astype(BF16))
                   for i in range(SAMPLE_PAGES_PER_TILE)]

    def sample_blocks():
        probs = []
        for bl in range(SAMPLE_PAGES_PER_TILE // PAGES_PER_BLOCK):
            s = jnp.concatenate(page_scores[bl * PAGES_PER_BLOCK:(bl + 1) * PAGES_PER_BLOCK], axis=1)
            blk_id = part * (SAMPLE_PAGES_PER_TILE // PAGES_PER_BLOCK) + bl
            m = jnp.max(s, axis=-1, keepdims=True)
            p = jnp.exp2(s - m)
            sm_scr[blk_id] = wide(m)
            sl_scr[blk_id] = wide(jnp.sum(p, axis=-1, keepdims=True))
            sgate_scr[blk_id] = wide(jnp.sum(s, axis=-1, keepdims=True) * (1.0 / MOBA_BLOCK))
            probs.append(p.astype(BF16))
        for bl, p in enumerate(probs):
            blk_id = part * (SAMPLE_PAGES_PER_TILE // PAGES_PER_BLOCK) + bl
            acc = None
            for r in range(PAGES_PER_BLOCK):
                vpage = vp_refs[bl * PAGES_PER_BLOCK + r][...].reshape(w, PAGE_SIZE).astype(BF16)
                pv = _dot_nt(p[:, r * PAGE_SIZE:(r + 1) * PAGE_SIZE], vpage)
                acc = pv if acc is None else acc + pv
            sacc_scr[blk_id] = acc

    qts = qt_ref[...]
    k0, k1, k2 = _split3(kmt_scr[...])
    gate = _dot(k0, qts) + _dot(k1, qts) + _dot(k2, qts)

    n_io = lax.broadcasted_iota(jnp.int32, (nb, blk), 0)
    half = lax.broadcasted_iota(jnp.int32, (pair_w, blk), 0) // A_HD
    for h in range(A_HEADS):
        g = jnp.where(n_io < j, gate[h * nb:(h + 1) * nb, :], NEG)
        bias = jnp.full((nb, blk), NEG, F32)
        for _ in range(MOBA_TOPK):
            mx = jnp.max(g, axis=0, keepdims=True)
            cand = jnp.where((g == mx) & (mx > 0.5 * NEG), n_io, nb)
            pick = n_io == jnp.min(cand, axis=0, keepdims=True)
            bias = jnp.where(pick, 0.0, bias)
            g = jnp.where(pick, NEG, g)
        selb_scr[h * nb:(h + 1) * nb, :] = bias
        pr = h // 2
        qpair = qts[pr * pair_w:(pr + 1) * pair_w, :]
        qtz_scr[h] = jnp.where(half == (h % 2), qpair, jnp.zeros_like(qpair))

    causal = (lax.broadcasted_iota(jnp.int32, (blk, blk), 0)
              <= lax.broadcasted_iota(jnp.int32, (blk, blk), 1))

    def fold_rows(x, op):
        return op(x.reshape(blk // SUBLANES, SUBLANES, blk), axis=0)

    def stage_scores(n, h, slot, own):
        pr = h // 2
        s = _dot(k_ref[n, :, pr * pair_w:(pr + 1) * pair_w], qtz_scr[h])
        if own:
            s = jnp.where(causal, s, NEG)
        s_scr[slot] = s
        col_max = jnp.max(fold_rows(s, jnp.max), axis=0, keepdims=True)
        if own:
            m_scr[h] = col_max
            ref_scr[slot] = col_max
        else:
            bias = selb_scr[pl.ds(h * nb + n, 1), :]
            m_old = m_scr[h]
            m_new = jnp.maximum(m_old, col_max + bias)
            m_scr[h] = m_new
            a_scr[slot] = jnp.exp2(m_old - m_new)
            ref_scr[slot] = m_new - bias

    def accumulate(n, h, slot, own):
        vs = slice(h * V_SLAB, (h + 1) * V_SLAB)
        p = jnp.exp2(s_scr[slot] - ref_scr[slot])
        pv = _dot(vt_ref[n, vs, :], p.astype(BF16))
        if own:
            acc_scr[vs, :] = pv
        else:
            acc_scr[vs, :] = a_scr[slot] * acc_scr[vs, :] + pv

    n_slots = MOBA_LOOKAHEAD + 1

    def stage_ahead(n, n_next, h, own):
        ha = h + MOBA_LOOKAHEAD
        if ha < A_HEADS:
            stage_scores(n, ha, ha % n_slots, own=own)
        else:
            stage_scores(n_next, ha - A_HEADS, ha % n_slots, own=False)

    for h in range(MOBA_LOOKAHEAD):
        stage_scores(j, h, h % n_slots, own=True)
    sample_blocks()
    for h in range(A_HEADS):
        stage_ahead(j, 0, h, own=True)
        accumulate(j, h, h % n_slots, own=True)

    def past_block(n, carry):
        for h in range(A_HEADS):
            stage_ahead(n, n + 1, h, own=False)
            accumulate(n, h, h % n_slots, own=False)
        return carry

    lax.fori_loop(0, j, past_block, 0)

    heads_out = []
    for h in range(A_HEADS):
        slab = acc_scr[h * V_SLAB:(h + 1) * V_SLAB, :]
        heads_out.append(slab[:A_HD, :] / slab[A_HD:A_HD + 1, :])
    o_ref[...] = jnp.concatenate(heads_out, axis=0).T.astype(o_ref.dtype)

    @pl.when(part == TILES_PER_SAMPLE - 1)
    def _():
        tile4 = lambda c: jnp.concatenate([c] * (w // LANES), axis=1)
        s = _dot_nt(qbd, kn_ref[...].astype(BF16))
        col = lax.broadcasted_iota(jnp.int32, s.shape, 1)
        qi = lax.broadcasted_iota(jnp.int32, s.shape, 0) // A_HEADS
        s = jnp.where(col <= qi, s, NEG)
        m_col = jnp.max(s, axis=-1, keepdims=True)
        p = jnp.exp2(s - m_col)
        m_own = wide(m_col)
        l_own = wide(jnp.sum(p, axis=-1, keepdims=True))
        o_own = _dot(p, vn_ref[...])

        for _ in range(MOBA_TOPK):
            best = jnp.full((nrow, LANES), NEG, F32)
            bidx = jnp.zeros((nrow, LANES), jnp.int32)
            for n in range(N_PAST_BLOCKS):
                gn = sgate_scr[n]
                upd = gn > best
                best = jnp.where(upd, gn, best)
                bidx = jnp.where(upd, n, bidx)
            for n in range(N_PAST_BLOCKS):
                sgate_scr[n] = jnp.where(bidx == n, -jnp.inf, sgate_scr[n])

        m_all = m_own
        for n in range(N_PAST_BLOCKS):
            m_all = jnp.maximum(m_all, jnp.where(sgate_scr[n] == -jnp.inf, sm_scr[n], NEG))
        w_own = jnp.exp2(m_own - m_all)
        l_all = w_own * l_own
        o_all = tile4(w_own) * o_own
        for n in range(N_PAST_BLOCKS):
            wn = jnp.where(sgate_scr[n] == -jnp.inf, jnp.exp2(sm_scr[n] - m_all), 0.0)
            l_all = l_all + wn * sl_scr[n]
            o_all = o_all + tile4(wn) * sacc_scr[n]
        out = o_all / tile4(l_all)
        for i in range(DEC_SEQ):
            rows = out[i * A_HEADS:(i + 1) * A_HEADS, :]
            os_ref[i:i + 1, :] = jnp.sum(jnp.where(head_mask, rows, 0.0), axis=0, keepdims=True)


def _moba(page_table, qt, kb, vt, q_s, k_new, v_new, cache_kt, cache_vt):
    bsz, nb = qt.shape[:2]
    w = A_HEADS * A_HD
    nrow = DEC_SEQ * A_HEADS
    n_slots = MOBA_LOOKAHEAD + 1
    step = lambda b, j: b * nb + j
    sample_of = lambda b, j: step(b, j) // TILES_PER_SAMPLE
    first_page = lambda b, j: (step(b, j) % TILES_PER_SAMPLE) * SAMPLE_PAGES_PER_TILE
    page_spec = lambda i: pl.BlockSpec(
        (None, A_HEADS, A_HD, PAGE_SIZE),
        lambda b, j, pt: (pt[sample_of(b, j), first_page(b, j) + i], 0, 0, 0))
    per_sample = lambda rows: pl.BlockSpec((None, rows, w), lambda b, j, pt: (sample_of(b, j), 0, 0))
    whole_row = lambda shape: pl.BlockSpec((None,) + shape, lambda b, j, pt: (b, 0, 0, 0),
                                           pipeline_mode=pl.Buffered(1))
    grid_spec = pltpu.PrefetchScalarGridSpec(
        num_scalar_prefetch=1,
        grid=(bsz, nb),
        in_specs=([pl.BlockSpec((None, None, w, MOBA_BLOCK), lambda b, j, pt: (b, j, 0, 0)),
                   whole_row((nb, MOBA_BLOCK, w)), whole_row((nb, V_ROWS, MOBA_BLOCK)),
                   per_sample(DEC_SEQ), per_sample(SUBLANES), per_sample(SUBLANES)]
                  + [page_spec(i) for i in range(SAMPLE_PAGES_PER_TILE)]
                  + [page_spec(i) for i in range(SAMPLE_PAGES_PER_TILE)]),
        out_specs=[pl.BlockSpec((MOBA_BLOCK, w), lambda b, j, pt: (step(b, j), 0)),
                   per_sample(DEC_SEQ)],
        scratch_shapes=[pltpu.VMEM((A_HEADS * nb, w), F32),
                        pltpu.VMEM((A_HEADS, 2 * A_HD, MOBA_BLOCK), BF16),
                        pltpu.VMEM((A_HEADS * nb, MOBA_BLOCK), F32),
                        pltpu.VMEM((A_HEADS, 1, MOBA_BLOCK), F32),
                        pltpu.VMEM((V_ROWS, MOBA_BLOCK), F32),
                        pltpu.VMEM((n_slots, MOBA_BLOCK, MOBA_BLOCK), F32),
                        pltpu.VMEM((n_slots, 1, MOBA_BLOCK), F32),
                        pltpu.VMEM((n_slots, 1, MOBA_BLOCK), F32),
                        pltpu.VMEM((N_PAST_BLOCKS, nrow, LANES), F32),
                        pltpu.VMEM((N_PAST_BLOCKS, nrow, LANES), F32),
                        pltpu.VMEM((N_PAST_BLOCKS, nrow, LANES), F32),
                        pltpu.VMEM((N_PAST_BLOCKS, nrow, w), F32)],
    )
    vmem = (nb * MOBA_BLOCK * (w + V_ROWS) * 2 + 2 * (w * MOBA_BLOCK * 2 + MOBA_BLOCK * w * 2)
            + n_slots * MOBA_BLOCK * MOBA_BLOCK * 4
            + 2 * 2 * SAMPLE_PAGES_PER_TILE * PAGE_SIZE * w * 4
            + N_PAST_BLOCKS * nrow * (3 * LANES + w) * 4 + 4 * MIB)
    return pl.pallas_call(
        _moba_kernel,
        grid_spec=grid_spec,
        out_shape=[jax.ShapeDtypeStruct((bsz * nb * MOBA_BLOCK, w), BF16),
                   jax.ShapeDtypeStruct((DEC_BATCH, DEC_SEQ, w), F32)],
        compiler_params=_params(("arbitrary", "arbitrary"), vmem / MIB + 6),
        name="moba",
    )(page_table, qt, kb, vt, q_s, k_new, v_new,
      *([cache_kt] * SAMPLE_PAGES_PER_TILE), *([cache_vt] * SAMPLE_PAGES_PER_TILE))


def _gmlp_kernel(u_ref, gv_ref, w_ref, bias_ref, avg_ref, lng_ref, lnb_ref, ob_ref, *vn_refs):
    chunks = [slice(c * B_CHUNK, (c + 1) * B_CHUNK) for c in range(u_ref.shape[0] // B_CHUNK)]
    avg = avg_ref[...]
    gvs = [_gelu_tanh(gv_ref[rows, :]) for rows in chunks]
    centred = [gv - _dot_exact_rhs(gv, avg) for gv in gvs]
    variances = [_dot_exact_rhs(gc * gc, avg, pieces=2) for gc in centred]
    pair_w = 2 * B_GD
    lane = lax.broadcasted_iota(jnp.int32, (B_CHUNK, pair_w), 1)
    for rows, gc, var in zip(chunks, centred, variances):
        vn = gc * lax.rsqrt(var + LN_EPS) * lng_ref[...] + lnb_ref[...]
        if vn_refs:
            vn_refs[0][rows, :] = vn
        vb = vn.astype(BF16)
        u = _gelu_tanh(u_ref[rows, :])
        for pr in range(B_GROUPS // 2):
            sl = slice(pr * pair_w, (pr + 1) * pair_w)
            vp = vb[:, sl]
            zero = jnp.zeros_like(vp)
            mixed = (_dot(w_ref[2 * pr], jnp.where(lane < B_GD, vp, zero))
                     + _dot(w_ref[2 * pr + 1], jnp.where(lane >= B_GD, vp, zero)))
            ob_ref[rows, sl] = (u[:, sl] * (mixed + bias_ref[:, sl])).astype(ob_ref.dtype)


def _gmlp(src, u_col, gv_col, w_masked, bias, avg, ln_g, ln_b, *, out_dtype, emit_vn):
    rows = src.shape[0]
    w = B_GROUPS * B_GD
    tile = B_CHUNK * min(GMLP_GROUP, rows // B_CHUNK)
    out_shape = [jax.ShapeDtypeStruct((rows, w), out_dtype)]
    out_specs = [pl.BlockSpec((tile, w), lambda c: (c, 0))]
    if emit_vn:
        out_shape.append(jax.ShapeDtypeStruct((rows, w), F32))
        out_specs.append(pl.BlockSpec((tile, w), lambda c: (c, 0)))
    return pl.pallas_call(
        _gmlp_kernel,
        grid=(rows // tile,),
        in_specs=[pl.BlockSpec((tile, w), lambda c: (c, u_col)),
                  pl.BlockSpec((tile, w), lambda c: (c, gv_col)),
                  _resident((B_GROUPS, B_CHUNK, B_CHUNK)), _resident((B_CHUNK, w)),
                  _resident((w, w)), _resident((1, w)), _resident((1, w))],
        out_specs=out_specs,
        out_shape=out_shape,
        compiler_params=_params(("parallel",), 24),
        name="gmlp_gate",
    )(src, src, w_masked, bias, avg, ln_g.reshape(1, w), ln_b.reshape(1, w))


def _pool_tile(halo, x, pos_first, w_ref, sc_ref, y_ref):
    rows = x.shape[0]
    ext = jnp.concatenate([halo, x], axis=0)
    pos = pos_first + lax.broadcasted_iota(jnp.int32, (rows, C_GD), 0)
    for gi, win in enumerate(POOL_WINDOWS):
        sl = slice(gi * C_GD, (gi + 1) * C_GD)
        s = ext[:, sl]
        sh = 1
        while sh < win:
            s = s + pltpu.roll(s, sh, 0)
            sh *= 2
        cnt = jnp.minimum(win, pos + 1).astype(F32)
        pooled = s[POOL_HALO:, :] / cnt - x[:, sl]
        y_ref[:, sl] = (_dot(pooled.astype(BF16), w_ref[gi]) * sc_ref[:, sl]).astype(y_ref.dtype)


def _pool_kernel(halo_ref, x_ref, w_ref, sc_ref, y_ref, *, pos0):
    _pool_tile(halo_ref[...], x_ref[...], pos0, w_ref, sc_ref, y_ref)


def _proj_pool_kernel(xh_ref, x_ref, w_ref, pw_ref, sc_ref, h_ref, y_ref, *, tiles_per_seq):
    t = pl.program_id(0) % tiles_per_seq
    rows = x_ref.shape[0]
    xb = x_ref[...].astype(BF16)
    xc = _dot(xb, w_ref[:, :HALF_W])
    h_ref[:, :HALF_W] = xc
    for c in range(1, w_ref.shape[1] // HALF_W):
        sl = slice(c * HALF_W, (c + 1) * HALF_W)
        h_ref[:, sl] = _dot(xb, w_ref[:, sl])
    halo = _dot(xh_ref[...].astype(BF16), w_ref[:, :HALF_W])
    _pool_tile(jnp.where(t == 0, 0.0, halo), xc, t * rows, pw_ref, sc_ref, y_ref)


def _pool(halo_src, x_src, w_bf16, scale, *, rows, pos0):
    w = len(POOL_WINDOWS) * C_GD
    n_rows = x_src.shape[0]
    return pl.pallas_call(
        functools.partial(_pool_kernel, pos0=pos0),
        grid=(n_rows // rows,),
        in_specs=[pl.BlockSpec((POOL_HALO, w), lambda i: (i, 0)),
                  pl.BlockSpec((rows, w), lambda i: (i, 0)),
                  _resident((len(POOL_WINDOWS), C_GD, C_GD)), _resident((1, w))],
        out_specs=pl.BlockSpec((rows, w), lambda i: (i, 0)),
        out_shape=jax.ShapeDtypeStruct((n_rows, w), F32),
        compiler_params=_params(("parallel",), 24),
        name="pool_mix",
    )(halo_src, x_src, w_bf16, scale.reshape(1, w))


def _proj_pool(x, w_bf16, pool_w_bf16, scale):
    n, d = x.shape
    n_out = w_bf16.shape[1]
    w = len(POOL_WINDOWS) * C_GD
    tile = PROMPT_TILE
    halo_step = tile // POOL_HALO
    vmem = (2 * (tile * d * 4 + POOL_HALO * d * 4 + tile * n_out * 4 + tile * w * 2)
            + d * n_out * 2 + 6 * tile * w * 4)
    return pl.pallas_call(
        functools.partial(_proj_pool_kernel, tiles_per_seq=SEQ // tile),
        grid=(n // tile,),
        in_specs=[pl.BlockSpec((POOL_HALO, d), lambda i: (jnp.maximum(i * halo_step - 1, 0), 0)),
                  pl.BlockSpec((tile, d), lambda i: (i, 0)),
                  _resident((d, n_out)),
                  _resident((len(POOL_WINDOWS), C_GD, C_GD)), _resident((1, w))],
        out_specs=[pl.BlockSpec((tile, n_out), lambda i: (i, 0)),
                   pl.BlockSpec((tile, w), lambda i: (i, 0))],
        out_shape=[jax.ShapeDtypeStruct((n, n_out), F32), jax.ShapeDtypeStruct((n, w), BF16)],
        compiler_params=_params(("parallel",), vmem / MIB + 8),
        name="in_proj_pool",
    )(x, x, w_bf16, pool_w_bf16, scale.reshape(1, w))


def _hgrn_kernel(q_ref, f_ref, i_ref, g_ref, s0_ref, lb_ref, ng_ref, o_ref, sfin_ref, s_scr,
                 *, rows, in_rows, valid, seqs):
    t = pl.program_id(1)
    c_rows = HGRN_CHUNK

    @pl.when(t == 0)
    def _():
        s_scr[...] = s0_ref[...]

    r_io = lax.broadcasted_iota(jnp.int32, (c_rows, c_rows), 0)
    c_io = lax.broadcasted_iota(jnp.int32, (c_rows, c_rows), 1)
    causal = r_io >= c_io
    ltri = jnp.where(causal, 1.0, 0.0).astype(BF16)
    lgrp = jnp.where(c_io < (r_io // HGRN_SUB) * HGRN_SUB + HGRN_SUB // 2, 1.0, 0.0).astype(BF16)
    lsum = jnp.concatenate([ltri, lgrp], axis=0)
    eye = r_io == c_io
    row_id = lax.broadcasted_iota(jnp.int32, (c_rows, D_HK), 0)

    heads = [slice(hd * D_HK, (hd + 1) * D_HK) for hd in range(D_HEADS)]

    def load(ref, item, sl):
        r0, seq = item
        if in_rows == rows:
            return ref[pl.ds(r0, c_rows), sl]
        return jnp.concatenate([ref[seq * in_rows:(seq + 1) * in_rows, sl],
                                jnp.zeros((c_rows - in_rows, D_HK), F32)], axis=0)

    def decay_sums(item):
        gates = []
        for sl in heads:
            lb = lb_ref[:, sl]
            half_span = 0.5 * (1.0 - lb)
            f = (lb + half_span) + half_span * jnp.tanh(0.5 * load(f_ref, item, sl))
            logf = jnp.log2(f)
            kk = 1.0 - f
            if valid < rows:
                live = (t * rows + item[0] + row_id) < valid
                logf = jnp.where(live, logf, 0.0)
                kk = jnp.where(live, kk, 0.0)
            sums = _dot_exact_lhs(lsum, logf)
            gates.append((kk, sums[:c_rows, :], sums[c_rows:, :]))
        return gates

    def chunk_scores(item, gates):
        attns, queries = [], []
        for sl, (kk, cg, ref) in zip(heads, gates):
            q = _silu(load(q_ref, item, sl))
            qd = (q * jnp.exp2(cg - ref)).astype(BF16)
            blocks = []
            for i in range(c_rows // HGRN_SUB):
                ref_i = ref[i * HGRN_SUB:i * HGRN_SUB + 1, :]
                e = jnp.where(row_id < (i + 1) * HGRN_SUB, ref_i - cg, 0.0)
                k_i = (kk * jnp.exp2(e)).astype(BF16)
                blocks.append(_dot_nt(qd[i * HGRN_SUB:(i + 1) * HGRN_SUB, :], k_i))
            attns.append(jnp.where(causal, jnp.concatenate(blocks, axis=0), 0.0).astype(BF16))
            queries.append((q * jnp.exp2(cg)).astype(BF16))
        return attns, queries

    def advance_state(item, gates, attns, queries):
        seq = item[1]
        outs = []
        for hd, (sl, (kk, cg, _)) in enumerate(zip(heads, gates)):
            vb = load(i_ref, item, sl).astype(BF16)
            state = s_scr[seq, hd]
            outs.append(_dot(queries[hd], state.astype(BF16)) + _dot(attns[hd], vb))
            g_last = cg[c_rows - 1:c_rows, :]
            kd = kk * jnp.exp2(g_last - cg)
            decay_col = jnp.sum(
                jnp.where(eye, jnp.broadcast_to(jnp.exp2(g_last), (c_rows, D_HK)), 0.0),
                axis=1, keepdims=True)
            s_scr[seq, hd] = state * decay_col + _dot(kd.T.astype(BF16), vb)
        return outs

    def finish(item, outs):
        r0, seq = item
        for sl, o in zip(heads, outs):
            o = o * lax.rsqrt(jnp.mean(o * o, axis=-1, keepdims=True) + RMS_EPS) * ng_ref[...]
            o = (o * _silu(load(g_ref, item, sl))).astype(o_ref.dtype)
            if in_rows == rows:
                o_ref[pl.ds(r0, c_rows), sl] = o
            else:
                o_ref[seq * in_rows:(seq + 1) * in_rows, sl] = o[:in_rows, :]

    def run_items(items):
        gates = [decay_sums(item) for item in items]
        outs = []
        for item, gt in zip(items, gates):
            attns, queries = chunk_scores(item, gt)
            outs.append(advance_state(item, gt, attns, queries))
        for item, o in zip(items, outs):
            finish(item, o)

    if in_rows == rows:
        group = min(HGRN_GROUP, rows // c_rows)

        def chunk_group(c, carry):
            run_items([(pl.multiple_of((c * group + k) * c_rows, c_rows), 0) for k in range(group)])
            return carry

        lax.fori_loop(0, rows // (c_rows * group), chunk_group, 0)
    else:
        run_items([(0, seq) for seq in range(seqs)])

    @pl.when(t == pl.num_programs(1) - 1)
    def _():
        sfin_ref[...] = s_scr[...]


def _hgrn(src, s0, lb, norm_g, *, bsz, length, rows, in_rows, valid, seqs, out_dtype):
    w = D_HEADS * D_HK
    tiles = length // rows
    assert seqs == 1 or tiles == 1
    blk_rows = seqs * in_rows
    col = lambda k: pl.BlockSpec((blk_rows, w), lambda b, t: (b * tiles + t, 1 + k))
    state_spec = pl.BlockSpec((seqs, D_HEADS, D_HK, D_HK), lambda b, t: (b, 0, 0, 0))
    return pl.pallas_call(
        functools.partial(_hgrn_kernel, rows=rows, in_rows=in_rows, valid=valid, seqs=seqs),
        grid=(bsz // seqs, tiles),
        in_specs=[col(0), col(1), col(2), col(3), state_spec, _resident((1, w)),
                  _resident((1, D_HK))],
        out_specs=[pl.BlockSpec((blk_rows, w), lambda b, t: (b * tiles + t, 0)), state_spec],
        out_shape=[jax.ShapeDtypeStruct((bsz * tiles * in_rows, w), out_dtype),
                   jax.ShapeDtypeStruct((bsz, D_HEADS, D_HK, D_HK), F32)],
        scratch_shapes=[pltpu.VMEM((seqs, D_HEADS, D_HK, D_HK), F32)],
        compiler_params=_params(("parallel", "arbitrary"), 32),
        name="hgrn2",
    )(src, src, src, src, s0, lb.reshape(1, w), norm_g.reshape(1, D_HK))


def kernel(x_prompt, x_sample, cache_k, cache_v, state_pool, state_hgrn, page_table, w_in_even, w_out_even, gmlp_ws, gmlp_bs, gmlp_ln_g, gmlp_ln_b, w_in_odd, w_out_odd, pool_w, pool_scale, hgrn_lb_param, hgrn_norm_g, ln_mix_g, ln_mix_b, ln_ffn_g, ln_ffn_b, ffn_w1, ffn_w2):
    w = HALF_W
    xp = x_prompt.reshape(N_PROMPT, D_MODEL)
    xs = x_sample.reshape(N_SAMPLE, D_MODEL)
    pad_tokens = ((0, 0), (0, SUBLANES - DEC_SEQ), (0, 0))

    col_scale = jnp.where(jnp.arange(EVEN_IN) < w, SCORE_SCALE, 1.0)
    w_even = (w_in_even[0] * col_scale).astype(BF16)
    kb, ugv, qt, vt, kt32, vt32 = _proj_even_prompt(x_prompt, w_even, w_even[:, :3 * w].T)
    h0s = _proj(xs, w_even, SAMPLE_TILE)
    hs = h0s.reshape(DEC_BATCH, DEC_SEQ, EVEN_IN)
    oa_prompt, oa_sample = _moba(page_table, qt, kb, vt, hs[..., :w],
                                 jnp.pad(hs[..., w:2 * w], pad_tokens), jnp.pad(hs[..., 2 * w:3 * w], pad_tokens),
                                 cache_k[0].transpose(0, 2, 3, 1), cache_v[0].transpose(0, 2, 3, 1))

    tril = jnp.tril(jnp.ones((B_CHUNK, B_CHUNK), bool))
    ws_prompt = jnp.where(tril[None], gmlp_ws[0], 0.0)
    owner = jnp.arange(B_CHUNK) // DEC_SEQ
    ws_sample = jnp.where(owner[:, None] == owner[None, :],
                          jnp.tile(ws_prompt[:, :DEC_SEQ, :DEC_SEQ], (1, DEC_BATCH, DEC_BATCH)), 0.0)
    bias_prompt = jnp.repeat(gmlp_bs[0].T, B_GD, axis=1)
    bias_sample = jnp.tile(bias_prompt[:DEC_SEQ], (DEC_BATCH, 1))
    grp = jnp.arange(w) // B_GD
    avg = jnp.where(grp[:, None] == grp[None, :], 1.0 / B_GD, 0.0).astype(BF16)
    ob_prompt, = _gmlp(ugv, 0, 1, ws_prompt.astype(BF16), bias_prompt, avg, gmlp_ln_g[0], gmlp_ln_b[0],
                       out_dtype=BF16, emit_vn=False)
    ob_sample, vn_sample = _gmlp(h0s, 3, 4, ws_sample.astype(BF16), bias_sample, avg, gmlp_ln_g[0],
                                 gmlp_ln_b[0], out_dtype=F32, emit_vn=True)

    layer0 = (w_out_even[0].astype(BF16), ln_mix_g[0], ln_mix_b[0],
              ffn_w1[0].astype(BF16), ffn_w2[0].astype(BF16), ln_ffn_g[0], ln_ffn_b[0])
    xp = _mix_ffn(oa_prompt, ob_prompt, xp, *layer0, FFN_TILE)
    xs = _mix_ffn(oa_sample.reshape(N_SAMPLE, w), ob_sample, xs, *layer0, SAMPLE_TILE)

    per_head = lambda t: t.reshape(BATCH, A_HEADS, A_HD, SEQ).transpose(0, 3, 1, 2)[None]
    new_k_prompt = per_head(kt32)
    new_v_prompt = per_head(vt32)
    new_k_sample = hs[..., w:2 * w].reshape(1, DEC_BATCH, DEC_SEQ, A_HEADS, A_HD)
    new_v_sample = hs[..., 2 * w:3 * w].reshape(1, DEC_BATCH, DEC_SEQ, A_HEADS, A_HD)
    new_gmlp_v_sample = vn_sample.reshape(1, DEC_BATCH, DEC_SEQ, w)

    w_odd = w_in_odd[0].astype(BF16)
    lb = jax.nn.softmax(hgrn_lb_param.astype(F32), axis=0)[0]
    pool_wb = pool_w[0].astype(BF16)

    h1p, oc_prompt = _proj_pool(xp, w_odd, pool_wb, pool_scale[0])
    od_prompt, s_prompt = _hgrn(h1p, jnp.zeros((BATCH, D_HEADS, D_HK, D_HK), F32), lb, hgrn_norm_g[0],
                                bsz=BATCH, length=SEQ, rows=HGRN_ROWS, in_rows=HGRN_ROWS, valid=SEQ,
                                seqs=1, out_dtype=BF16)

    h1s = _proj(xs, w_odd, SAMPLE_TILE).reshape(DEC_BATCH, DEC_SEQ, ODD_IN)
    h1s_pad = jnp.pad(h1s, pad_tokens).reshape(DEC_BATCH * SUBLANES, ODD_IN)
    halo_s = jnp.pad(state_pool[0], ((0, 0), (POOL_HALO - POOL_BUF, 0), (0, 0)))
    oc_sample = _pool(halo_s.reshape(DEC_BATCH * POOL_HALO, w), h1s_pad, pool_wb, pool_scale[0],
                      rows=SUBLANES, pos0=PAST_LEN)
    od_sample, s_sample = _hgrn(h1s_pad, state_hgrn[0], lb, hgrn_norm_g[0], bsz=DEC_BATCH,
                                length=HGRN_CHUNK, rows=HGRN_CHUNK, in_rows=SUBLANES, valid=DEC_SEQ,
                                seqs=HGRN_SAMPLE_SEQS, out_dtype=F32)
    real_rows = lambda t: t.reshape(DEC_BATCH, SUBLANES, w)[:, :DEC_SEQ].reshape(N_SAMPLE, w)

    layer1 = (w_out_odd[0].astype(BF16), ln_mix_g[1], ln_mix_b[1],
              ffn_w1[1].astype(BF16), ffn_w2[1].astype(BF16), ln_ffn_g[1], ln_ffn_b[1])
    xp = _mix_ffn(oc_prompt, od_prompt, xp, *layer1, FFN_TILE)
    xs = _mix_ffn(real_rows(oc_sample), real_rows(od_sample), xs, *layer1, SAMPLE_TILE)

    new_pool_prompt = h1p.reshape(BATCH, SEQ, ODD_IN)[:, SEQ - POOL_BUF:, :w][None]
    new_pool_sample = jnp.concatenate([state_pool[0][:, DEC_SEQ:], h1s[..., :w]], axis=1)[None]
    return (xp.reshape(BATCH, SEQ, D_MODEL), xs.reshape(DEC_BATCH, DEC_SEQ, D_MODEL),
            new_k_prompt, new_v_prompt, new_k_sample, new_v_sample, new_gmlp_v_sample,
            new_pool_prompt, new_pool_sample, s_prompt[None], s_sample[None])
```

```python
import functools
import math

import jax
import jax.numpy as jnp
from jax import lax
from jax.experimental import pallas as pl
from jax.experimental.pallas import tpu as pltpu

F32 = jnp.float32
BF16 = jnp.bfloat16

D_MODEL = 1024
BATCH = 4
SEQ = 4096
DEPTH = 2
DEC_BATCH = 32
DEC_SEQ = 4
PAST_LEN = 8192
PAGE_SIZE = 128
HALF_W = D_MODEL // 2
A_HD = 64
A_HEADS = HALF_W // A_HD
MOBA_BLOCK = 256
MOBA_TOPK = 3
B_GROUPS = 8
B_GD = HALF_W // B_GROUPS
B_CHUNK = 128
POOL_WINDOWS = (2, 4, 8, 16)
C_GD = HALF_W // len(POOL_WINDOWS)
POOL_BUF = max(POOL_WINDOWS) - 1
D_HK = 128
D_HEADS = HALF_W // D_HK
D_FF = 4 * D_MODEL
EVEN_IN = 5 * HALF_W
ODD_IN = 5 * HALF_W
ALPHA = (2 * DEPTH) ** 0.25
LN_EPS = 1e-5
RMS_EPS = 1e-6
NEG = -1e30

N_PROMPT = BATCH * SEQ
N_SAMPLE = DEC_BATCH * DEC_SEQ
N_PAGES = PAST_LEN // PAGE_SIZE
N_PAST_BLOCKS = PAST_LEN // MOBA_BLOCK
PAGES_PER_BLOCK = MOBA_BLOCK // PAGE_SIZE
N_KEY_BLOCKS = SEQ // MOBA_BLOCK

LANES = 128
SUBLANES = 8
PROMPT_TILE = 1024
SAMPLE_TILE = N_SAMPLE
FF_CHUNK = 1024
FFN_TILE = 1024
FFN_GROUP_ROWS = 256
MOBA_LOOKAHEAD = 7
SAMPLE_PAGES_PER_TILE = DEC_BATCH * N_PAGES // (BATCH * N_KEY_BLOCKS)
TILES_PER_SAMPLE = N_PAGES // SAMPLE_PAGES_PER_TILE
assert SAMPLE_PAGES_PER_TILE * BATCH * N_KEY_BLOCKS == DEC_BATCH * N_PAGES
assert TILES_PER_SAMPLE * SAMPLE_PAGES_PER_TILE == N_PAGES and N_KEY_BLOCKS % TILES_PER_SAMPLE == 0
assert SAMPLE_PAGES_PER_TILE % PAGES_PER_BLOCK == 0
V_TAIL = 16
V_SLAB = A_HD + V_TAIL
V_ROWS = A_HEADS * V_SLAB
assert A_HEADS % (MOBA_LOOKAHEAD + 1) == 0
SCORE_SCALE = (A_HD ** -0.5) * math.log2(math.e)
HGRN_CHUNK = 128
HGRN_SUB = 32
HGRN_ROWS = 1024
HGRN_GROUP = 2
HGRN_SAMPLE_SEQS = 8
GMLP_GROUP = 8
POOL_HALO = 16
MIB = 1024 * 1024


def _dot(a, b):
    return jnp.dot(a, b, preferred_element_type=F32)


def _dot_nt(a, b):
    return lax.dot_general(a, b, (((1,), (1,)), ((), ())), preferred_element_type=F32)


def _split3(x):
    p0 = x.astype(BF16)
    r1 = x - p0.astype(F32)
    p1 = r1.astype(BF16)
    p2 = (r1 - p1.astype(F32)).astype(BF16)
    return p0, p1, p2


def _dot_exact_lhs(m_bf16, x):
    p0, p1, p2 = _split3(x)
    return _dot(m_bf16, p0) + _dot(m_bf16, p1) + _dot(m_bf16, p2)


def _dot_exact_rhs(x, m_bf16, pieces=3):
    parts = _split3(x)[:pieces]
    out = _dot(parts[0], m_bf16)
    for part in parts[1:]:
        out = out + _dot(part, m_bf16)
    return out


def _gelu_tanh(x):
    return 0.5 * x * (1.0 + jnp.tanh(0.7978845608028654 * (x + 0.044715 * (x * x * x))))


def _silu(x):
    h = 0.5 * x
    return h + h * jnp.tanh(h)


def _layer_norm(z, g, b):
    mu = jnp.mean(z, axis=-1, keepdims=True)
    zc = z - mu
    var = jnp.mean(zc * zc, axis=-1, keepdims=True)
    return zc * lax.rsqrt(var + LN_EPS) * g + b


def _params(sem, vmem_mib):
    return pltpu.CompilerParams(dimension_semantics=sem, vmem_limit_bytes=int(vmem_mib * MIB))


def _resident(shape):
    return pl.BlockSpec(shape, lambda *_: (0,) * len(shape), pipeline_mode=pl.Buffered(1))


def _proj_kernel(x_ref, w_ref, h_ref):
    xb = x_ref[...].astype(BF16)
    for c in range(w_ref.shape[1] // HALF_W):
        sl = slice(c * HALF_W, (c + 1) * HALF_W)
        h_ref[:, sl] = _dot(xb, w_ref[:, sl])


def _proj(x, w_bf16, tile):
    n, d = x.shape
    n_out = w_bf16.shape[1]
    vmem = 2 * (tile * d * 4 + tile * n_out * 4) + d * n_out * 2
    return pl.pallas_call(
        _proj_kernel,
        grid=(n // tile,),
        in_specs=[pl.BlockSpec((tile, d), lambda i: (i, 0)), _resident((d, n_out))],
        out_specs=pl.BlockSpec((tile, n_out), lambda i: (i, 0)),
        out_shape=jax.ShapeDtypeStruct((n, n_out), F32),
        compiler_params=_params(("parallel",), vmem / MIB + 8),
        name="in_proj",
    )(x, w_bf16)


def _proj_even_kernel(x_ref, wn_ref, wt_ref, kb_ref, ugv_ref, qt_ref, vt_ref, kt32_ref, vt32_ref):
    w = HALF_W
    xb = x_ref[...].astype(BF16)
    k = _dot(xb, wn_ref[:, w:2 * w])
    for half in range(PROMPT_TILE // MOBA_BLOCK):
        kb_ref[half] = k[half * MOBA_BLOCK:(half + 1) * MOBA_BLOCK, :].astype(BF16)
    ugv_ref[:, :w] = _dot(xb, wn_ref[:, 3 * w:4 * w])
    ugv_ref[:, w:] = _dot(xb, wn_ref[:, 4 * w:])
    qt = _dot_nt(wt_ref[:w, :], xb)
    kt32_ref[...] = k.T
    vt = _dot_nt(wt_ref[2 * w:, :], xb)
    vt32_ref[...] = vt
    ones_row = jnp.where(lax.broadcasted_iota(jnp.int32, (V_TAIL, MOBA_BLOCK), 0) == 0,
                         1.0, 0.0).astype(BF16)
    for half in range(PROMPT_TILE // MOBA_BLOCK):
        cols = slice(half * MOBA_BLOCK, (half + 1) * MOBA_BLOCK)
        qt_ref[half] = qt[:, cols].astype(BF16)
        for h in range(A_HEADS):
            vt_ref[half, h * V_SLAB:h * V_SLAB + A_HD, :] = vt[h * A_HD:(h + 1) * A_HD, cols].astype(BF16)
            vt_ref[half, h * V_SLAB + A_HD:(h + 1) * V_SLAB, :] = ones_row


def _proj_even_prompt(x_prompt, w_nat, w_tr):
    w = HALF_W
    tiles = SEQ // PROMPT_TILE
    per_tile = PROMPT_TILE // MOBA_BLOCK
    blocked = lambda shape: pl.BlockSpec((None, per_tile) + shape, lambda b, t: (b, t, 0, 0))
    vmem = (2 * (PROMPT_TILE * D_MODEL * 4 + PROMPT_TILE * 2 * w * 4 + 3 * PROMPT_TILE * w * 2
                 + 2 * PROMPT_TILE * w * 4) + 2 * D_MODEL * 3 * w * 2 + 4 * PROMPT_TILE * w * 4)
    return pl.pallas_call(
        _proj_even_kernel,
        grid=(BATCH, tiles),
        in_specs=[pl.BlockSpec((None, PROMPT_TILE, D_MODEL), lambda b, t: (b, t, 0)),
                  _resident((D_MODEL, EVEN_IN)), _resident((3 * w, D_MODEL))],
        out_specs=[blocked((MOBA_BLOCK, w)),
                   pl.BlockSpec((PROMPT_TILE, 2 * w), lambda b, t: (b * tiles + t, 0)),
                   blocked((w, MOBA_BLOCK)), blocked((V_ROWS, MOBA_BLOCK)),
                   pl.BlockSpec((None, w, PROMPT_TILE), lambda b, t: (b, 0, t)),
                   pl.BlockSpec((None, w, PROMPT_TILE), lambda b, t: (b, 0, t))],
        out_shape=[jax.ShapeDtypeStruct((BATCH, N_KEY_BLOCKS, MOBA_BLOCK, w), BF16),
                   jax.ShapeDtypeStruct((N_PROMPT, 2 * w), F32),
                   jax.ShapeDtypeStruct((BATCH, N_KEY_BLOCKS, w, MOBA_BLOCK), BF16),
                   jax.ShapeDtypeStruct((BATCH, N_KEY_BLOCKS, V_ROWS, MOBA_BLOCK), BF16),
                   jax.ShapeDtypeStruct((BATCH, w, SEQ), F32),
                   jax.ShapeDtypeStruct((BATCH, w, SEQ), F32)],
        compiler_params=_params(("parallel", "parallel"), vmem / MIB + 8),
        name="in_proj_even",
    )(x_prompt, w_nat, w_tr)


def _mix_ffn_kernel(oa_ref, ob_ref, x_ref, wo_ref, g1_ref, b1_ref, w1_ref, w2_ref, g2_ref, b2_ref,
                    y_ref):
    tile = x_ref.shape[0]
    group_rows = min(FFN_GROUP_ROWS, tile)
    groups = [slice(r, r + group_rows) for r in range(0, tile, group_rows)]
    mixed = [_dot(oa_ref[rows, :].astype(BF16), wo_ref[:HALF_W, :])
             + _dot(ob_ref[rows, :].astype(BF16), wo_ref[HALF_W:, :]) for rows in groups]
    for rows, mix in zip(groups, mixed):
        x = _layer_norm(ALPHA * x_ref[rows, :] + mix, g1_ref[...], b1_ref[...])
        xb = x.astype(BF16)
        acc = jnp.zeros(x.shape, F32)
        for c in range(D_FF // FF_CHUNK):
            sl = slice(c * FF_CHUNK, (c + 1) * FF_CHUNK)
            hid = jnp.maximum(_dot(xb, w1_ref[:, sl]), 0.0)
            acc = acc + _dot((hid * hid).astype(BF16), w2_ref[sl, :])
        y_ref[rows, :] = _layer_norm(ALPHA * x + acc, g2_ref[...], b2_ref[...])


def _mix_ffn(oa, ob, x, wo_bf16, g1, b1, w1_bf16, w2_bf16, g2, b2, tile):
    n, d = x.shape
    row = lambda i: (i, 0)
    vec = lambda v: v.reshape(1, d)
    vmem = (2 * (2 * tile * HALF_W * oa.dtype.itemsize + 2 * tile * d * 4)
            + (d * d + 2 * d * D_FF) * 2
            + 8 * min(tile, FFN_GROUP_ROWS) * FF_CHUNK * 4)
    return pl.pallas_call(
        _mix_ffn_kernel,
        grid=(n // tile,),
        in_specs=[pl.BlockSpec((tile, HALF_W), row), pl.BlockSpec((tile, HALF_W), row),
                  pl.BlockSpec((tile, d), row), _resident((d, d)),
                  _resident((1, d)), _resident((1, d)),
                  _resident((d, D_FF)), _resident((D_FF, d)),
                  _resident((1, d)), _resident((1, d))],
        out_specs=pl.BlockSpec((tile, d), row),
        out_shape=jax.ShapeDtypeStruct((n, d), F32),
        compiler_params=_params(("parallel",), vmem / MIB + 8),
        name="mix_ffn_ln",
    )(oa, ob, x, wo_bf16, vec(g1), vec(b1), w1_bf16, w2_bf16, vec(g2), vec(b2))


def _moba_kernel(pt_ref, qt_ref, k_ref, vt_ref, qs_ref, kn_ref, vn_ref, *refs):
    del pt_ref
    kp_refs = refs[:SAMPLE_PAGES_PER_TILE]
    vp_refs = refs[SAMPLE_PAGES_PER_TILE:2 * SAMPLE_PAGES_PER_TILE]
    (o_ref, os_ref, kmt_scr, qtz_scr, selb_scr, m_scr, acc_scr, s_scr, ref_scr, a_scr,
     sm_scr, sl_scr, sgate_scr, sacc_scr) = refs[2 * SAMPLE_PAGES_PER_TILE:]
    j = pl.program_id(1)
    nb = k_ref.shape[0]
    blk = MOBA_BLOCK
    pair_w = 2 * A_HD
    w = A_HEADS * A_HD
    nrow = DEC_SEQ * A_HEADS
    part = (pl.program_id(0) * nb + j) % TILES_PER_SAMPLE
    wide = lambda c: jnp.broadcast_to(c, (nrow, LANES))

    @pl.when(j == 0)
    def _():
        rows = [jnp.sum(k_ref[n].astype(F32), axis=0, keepdims=True) * (1.0 / blk)
                for n in range(nb)]
        kmean = jnp.concatenate(rows, axis=0)
        tiled = jnp.concatenate([kmean] * A_HEADS, axis=0)
        rh = lax.broadcasted_iota(jnp.int32, tiled.shape, 0) // nb
        ch = lax.broadcasted_iota(jnp.int32, tiled.shape, 1) // A_HD
        kmt_scr[...] = jnp.where(rh == ch, tiled, 0.0)

    head_mask = (lax.broadcasted_iota(jnp.int32, (A_HEADS, w), 1) // A_HD
                 == lax.broadcasted_iota(jnp.int32, (A_HEADS, w), 0))
    qs = qs_ref[...]
    qbd = jnp.concatenate(
        [jnp.where(head_mask, jnp.broadcast_to(qs[i:i + 1, :], (A_HEADS, w)), 0.0)
         for i in range(DEC_SEQ)], axis=0).astype(BF16)
    page_scores = [_dot(qbd, kp_refs[i][...].reshape(w, PAGE_SIZE).astype(BF16))
                   for i in range(SAMPLE_PAGES_PER_TILE)]

    def sample_blocks():
        probs = []
        for bl in range(SAMPLE_PAGES_PER_TILE // PAGES_PER_BLOCK):
            s = jnp.concatenate(page_scores[bl * PAGES_PER_BLOCK:(bl + 1) * PAGES_PER_BLOCK], axis=1)
            blk_id = part * (SAMPLE_PAGES_PER_TILE // PAGES_PER_BLOCK) + bl
            m = jnp.max(s, axis=-1, keepdims=True)
            p = jnp.exp2(s - m)
            sm_scr[blk_id] = wide(m)
            sl_scr[blk_id] = wide(jnp.sum(p, axis=-1, keepdims=True))
            sgate_scr[blk_id] = wide(jnp.sum(s, axis=-1, keepdims=True) * (1.0 / MOBA_BLOCK))
            probs.append(p.astype(BF16))
        for bl, p in enumerate(probs):
            blk_id = part * (SAMPLE_PAGES_PER_TILE // PAGES_PER_BLOCK) + bl
            acc = None
            for r in range(PAGES_PER_BLOCK):
                vpage = vp_refs[bl * PAGES_PER_BLOCK + r][...].reshape(w, PAGE_SIZE).astype(BF16)
                pv = _dot_nt(p[:, r * PAGE_SIZE:(r + 1) * PAGE_SIZE], vpage)
                acc = pv if acc is None else acc + pv
            sacc_scr[blk_id] = acc

    qts = qt_ref[...]
    k0, k1, k2 = _split3(kmt_scr[...])
    gate = _dot(k0, qts) + _dot(k1, qts) + _dot(k2, qts)

    n_io = lax.broadcasted_iota(jnp.int32, (nb, blk), 0)
    half = lax.broadcasted_iota(jnp.int32, (pair_w, blk), 0) // A_HD
    for h in range(A_HEADS):
        g = jnp.where(n_io < j, gate[h * nb:(h + 1) * nb, :], NEG)
        bias = jnp.full((nb, blk), NEG, F32)
        for _ in range(MOBA_TOPK):
            mx = jnp.max(g, axis=0, keepdims=True)
            cand = jnp.where((g == mx) & (mx > 0.5 * NEG), n_io, nb)
            pick = n_io == jnp.min(cand, axis=0, keepdims=True)
            bias = jnp.where(pick, 0.0, bias)
            g = jnp.where(pick, NEG, g)
        selb_scr[h * nb:(h + 1) * nb, :] = bias
        pr = h // 2
        qpair = qts[pr * pair_w:(pr + 1) * pair_w, :]
        qtz_scr[h] = jnp.where(half == (h % 2), qpair, jnp.zeros_like(qpair))

    causal = (lax.broadcasted_iota(jnp.int32, (blk, blk), 0)
              <= lax.broadcasted_iota(jnp.int32, (blk, blk), 1))

    def fold_rows(x, op):
        return op(x.reshape(blk // SUBLANES, SUBLANES, blk), axis=0)

    def stage_scores(n, h, slot, own):
        pr = h // 2
        s = _dot(k_ref[n, :, pr * pair_w:(pr + 1) * pair_w], qtz_scr[h])
        if own:
            s = jnp.where(causal, s, NEG)
        s_scr[slot] = s
        col_max = jnp.max(fold_rows(s, jnp.max), axis=0, keepdims=True)
        if own:
            m_scr[h] = col_max
            ref_scr[slot] = col_max
        else:
            bias = selb_scr[pl.ds(h * nb + n, 1), :]
            m_old = m_scr[h]
            m_new = jnp.maximum(m_old, col_max + bias)
            m_scr[h] = m_new
            a_scr[slot] = jnp.exp2(m_old - m_new)
            ref_scr[slot] = m_new - bias

    def accumulate(n, h, slot, own):
        vs = slice(h * V_SLAB, (h + 1) * V_SLAB)
        p = jnp.exp2(s_scr[slot] - ref_scr[slot])
        pv = _dot(vt_ref[n, vs, :], p.astype(BF16))
        if own:
            acc_scr[vs, :] = pv
        else:
            acc_scr[vs, :] = a_scr[slot] * acc_scr[vs, :] + pv

    n_slots = MOBA_LOOKAHEAD + 1

    def stage_ahead(n, n_next, h, own):
        ha = h + MOBA_LOOKAHEAD
        if ha < A_HEADS:
            stage_scores(n, ha, ha % n_slots, own=own)
        else:
            stage_scores(n_next, ha - A_HEADS, ha % n_slots, own=False)

    for h in range(MOBA_LOOKAHEAD):
        stage_scores(j, h, h % n_slots, own=True)
    sample_blocks()
    for h in range(A_HEADS):
        stage_ahead(j, 0, h, own=True)
        accumulate(j, h, h % n_slots, own=True)

    def past_block(n, carry):
        for h in range(A_HEADS):
            stage_ahead(n, n + 1, h, own=False)
            accumulate(n, h, h % n_slots, own=False)
        return carry

    lax.fori_loop(0, j, past_block, 0)

    heads_out = []
    for h in range(A_HEADS):
        slab = acc_scr[h * V_SLAB:(h + 1) * V_SLAB, :]
        heads_out.append(slab[:A_HD, :] / slab[A_HD:A_HD + 1, :])
    o_ref[...] = jnp.concatenate(heads_out, axis=0).T.astype(o_ref.dtype)

    @pl.when(part == TILES_PER_SAMPLE - 1)
    def _():
        tile4 = lambda c: jnp.concatenate([c] * (w // LANES), axis=1)
        s = _dot_nt(qbd, kn_ref[...].astype(BF16))
        col = lax.broadcasted_iota(jnp.int32, s.shape, 1)
        qi = lax.broadcasted_iota(jnp.int32, s.shape, 0) // A_HEADS
        s = jnp.where(col <= qi, s, NEG)
        m_col = jnp.max(s, axis=-1, keepdims=True)
        p = jnp.exp2(s - m_col)
        m_own = wide(m_col)
        l_own = wide(jnp.sum(p, axis=-1, keepdims=True))
        o_own = _dot(p, vn_ref[...])

        for _ in range(MOBA_TOPK):
            best = jnp.full((nrow, LANES), NEG, F32)
            bidx = jnp.zeros((nrow, LANES), jnp.int32)
            for n in range(N_PAST_BLOCKS):
                gn = sgate_scr[n]
                upd = gn > best
                best = jnp.where(upd, gn, best)
                bidx = jnp.where(upd, n, bidx)
            for n in range(N_PAST_BLOCKS):
                sgate_scr[n] = jnp.where(bidx == n, -jnp.inf, sgate_scr[n])

        m_all = m_own
        for n in range(N_PAST_BLOCKS):
            m_all = jnp.maximum(m_all, jnp.where(sgate_scr[n] == -jnp.inf, sm_scr[n], NEG))
        w_own = jnp.exp2(m_own - m_all)
        l_all = w_own * l_own
        o_all = tile4(w_own) * o_own
        for n in range(N_PAST_BLOCKS):
            wn = jnp.where(sgate_scr[n] == -jnp.inf, jnp.exp2(sm_scr[n] - m_all), 0.0)
            l_all = l_all + wn * sl_scr[n]
            o_all = o_all + tile4(wn) * sacc_scr[n]
        out = o_all / tile4(l_all)
        for i in range(DEC_SEQ):
            rows = out[i * A_HEADS:(i + 1) * A_HEADS, :]
            os_ref[i:i + 1, :] = jnp.sum(jnp.where(head_mask, rows, 0.0), axis=0, keepdims=True)


def _moba(page_table, qt, kb, vt, q_s, k_new, v_new, cache_kt, cache_vt):
    bsz, nb = qt.shape[:2]
    w = A_HEADS * A_HD
    nrow = DEC_SEQ * A_HEADS
    n_slots = MOBA_LOOKAHEAD + 1
    step = lambda b, j: b * nb + j
    sample_of = lambda b, j: step(b, j) // TILES_PER_SAMPLE
    first_page = lambda b, j: (step(b, j) % TILES_PER_SAMPLE) * SAMPLE_PAGES_PER_TILE
    page_spec = lambda i: pl.BlockSpec(
        (None, A_HEADS, A_HD, PAGE_SIZE),
        lambda b, j, pt: (pt[sample_of(b, j), first_page(b, j) + i], 0, 0, 0))
    per_sample = lambda rows: pl.BlockSpec((None, rows, w), lambda b, j, pt: (sample_of(b, j), 0, 0))
    whole_row = lambda shape: pl.BlockSpec((None,) + shape, lambda b, j, pt: (b, 0, 0, 0),
                                           pipeline_mode=pl.Buffered(1))
    grid_spec = pltpu.PrefetchScalarGridSpec(
        num_scalar_prefetch=1,
        grid=(bsz, nb),
        in_specs=([pl.BlockSpec((None, None, w, MOBA_BLOCK), lambda b, j, pt: (b, j, 0, 0)),
                   whole_row((nb, MOBA_BLOCK, w)), whole_row((nb, V_ROWS, MOBA_BLOCK)),
                   per_sample(DEC_SEQ), per_sample(SUBLANES), per_sample(SUBLANES)]
                  + [page_spec(i) for i in range(SAMPLE_PAGES_PER_TILE)]
                  + [page_spec(i) for i in range(SAMPLE_PAGES_PER_TILE)]),
        out_specs=[pl.BlockSpec((MOBA_BLOCK, w), lambda b, j, pt: (step(b, j), 0)),
                   per_sample(DEC_SEQ)],
        scratch_shapes=[pltpu.VMEM((A_HEADS * nb, w), F32),
                        pltpu.VMEM((A_HEADS, 2 * A_HD, MOBA_BLOCK), BF16),
                        pltpu.VMEM((A_HEADS * nb, MOBA_BLOCK), F32),
                        pltpu.VMEM((A_HEADS, 1, MOBA_BLOCK), F32),
                        pltpu.VMEM((V_ROWS, MOBA_BLOCK), F32),
                        pltpu.VMEM((n_slots, MOBA_BLOCK, MOBA_BLOCK), F32),
                        pltpu.VMEM((n_slots, 1, MOBA_BLOCK), F32),
                        pltpu.VMEM((n_slots, 1, MOBA_BLOCK), F32),
                        pltpu.VMEM((N_PAST_BLOCKS, nrow, LANES), F32),
                        pltpu.VMEM((N_PAST_BLOCKS, nrow, LANES), F32),
                        pltpu.VMEM((N_PAST_BLOCKS, nrow, LANES), F32),
                        pltpu.VMEM((N_PAST_BLOCKS, nrow, w), F32)],
    )
    vmem = (nb * MOBA_BLOCK * (w + V_ROWS) * 2 + 2 * (w * MOBA_BLOCK * 2 + MOBA_BLOCK * w * 2)
            + n_slots * MOBA_BLOCK * MOBA_BLOCK * 4
            + 2 * 2 * SAMPLE_PAGES_PER_TILE * PAGE_SIZE * w * 4
            + N_PAST_BLOCKS * nrow * (3 * LANES + w) * 4 + 4 * MIB)
    return pl.pallas_call(
        _moba_kernel,
        grid_spec=grid_spec,
        out_shape=[jax.ShapeDtypeStruct((bsz * nb * MOBA_BLOCK, w), BF16),
                   jax.ShapeDtypeStruct((DEC_BATCH, DEC_SEQ, w), F32)],
        compiler_params=_params(("arbitrary", "arbitrary"), vmem / MIB + 6),
        name="moba",
    )(page_table, qt, kb, vt, q_s, k_new, v_new,
      *([cache_kt] * SAMPLE_PAGES_PER_TILE), *([cache_vt] * SAMPLE_PAGES_PER_TILE))


def _gmlp_kernel(u_ref, gv_ref, w_ref, bias_ref, avg_ref, lng_ref, lnb_ref, ob_ref, *vn_refs):
    chunks = [slice(c * B_CHUNK, (c + 1) * B_CHUNK) for c in range(u_ref.shape[0] // B_CHUNK)]
    avg = avg_ref[...]
    gvs = [_gelu_tanh(gv_ref[rows, :]) for rows in chunks]
    centred = [gv - _dot_exact_rhs(gv, avg) for gv in gvs]
    variances = [_dot_exact_rhs(gc * gc, avg, pieces=2) for gc in centred]
    pair_w = 2 * B_GD
    lane = lax.broadcasted_iota(jnp.int32, (B_CHUNK, pair_w), 1)
    for rows, gc, var in zip(chunks, centred, variances):
        vn = gc * lax.rsqrt(var + LN_EPS) * lng_ref[...] + lnb_ref[...]
        if vn_refs:
            vn_refs[0][rows, :] = vn
        vb = vn.astype(BF16)
        u = _gelu_tanh(u_ref[rows, :])
        for pr in range(B_GROUPS // 2):
            sl = slice(pr * pair_w, (pr + 1) * pair_w)
            vp = vb[:, sl]
            zero = jnp.zeros_like(vp)
            mixed = (_dot(w_ref[2 * pr], jnp.where(lane < B_GD, vp, zero))
                     + _dot(w_ref[2 * pr + 1], jnp.where(lane >= B_GD, vp, zero)))
            ob_ref[rows, sl] = (u[:, sl] * (mixed + bias_ref[:, sl])).astype(ob_ref.dtype)


def _gmlp(src, u_col, gv_col, w_masked, bias, avg, ln_g, ln_b, *, out_dtype, emit_vn):
    rows = src.shape[0]
    w = B_GROUPS * B_GD
    tile = B_CHUNK * min(GMLP_GROUP, rows // B_CHUNK)
    out_shape = [jax.ShapeDtypeStruct((rows, w), out_dtype)]
    out_specs = [pl.BlockSpec((tile, w), lambda c: (c, 0))]
    if emit_vn:
        out_shape.append(jax.ShapeDtypeStruct((rows, w), F32))
        out_specs.append(pl.BlockSpec((tile, w), lambda c: (c, 0)))
    return pl.pallas_call(
        _gmlp_kernel,
        grid=(rows // tile,),
        in_specs=[pl.BlockSpec((tile, w), lambda c: (c, u_col)),
                  pl.BlockSpec((tile, w), lambda c: (c, gv_col)),
                  _resident((B_GROUPS, B_CHUNK, B_CHUNK)), _resident((B_CHUNK, w)),
                  _resident((w, w)), _resident((1, w)), _resident((1, w))],
        out_specs=out_specs,
        out_shape=out_shape,
        compiler_params=_params(("parallel",), 2 * 4 * tile * w * 4 / MIB + 16),
        name="gmlp_gate",
    )(src, src, w_masked, bias, avg, ln_g.reshape(1, w), ln_b.reshape(1, w))


def _pool_tile(halo, x, pos_first, w_ref, sc_ref, y_ref):
    rows = x.shape[0]
    ext = jnp.concatenate([halo, x], axis=0)
    pos = pos_first + lax.broadcasted_iota(jnp.int32, (rows, C_GD), 0)
    for gi, win in enumerate(POOL_WINDOWS):
        sl = slice(gi * C_GD, (gi + 1) * C_GD)
        s = ext[:, sl]
        sh = 1
        while sh < win:
            s = s + pltpu.roll(s, sh, 0)
            sh *= 2
        cnt = jnp.minimum(win, pos + 1).astype(F32)
        pooled = s[POOL_HALO:, :] / cnt - x[:, sl]
        y_ref[:, sl] = (_dot(pooled.astype(BF16), w_ref[gi]) * sc_ref[:, sl]).astype(y_ref.dtype)


def _pool_kernel(halo_ref, x_ref, w_ref, sc_ref, y_ref, *, pos0):
    _pool_tile(halo_ref[...], x_ref[...], pos0, w_ref, sc_ref, y_ref)


def _proj_pool_kernel(xh_ref, x_ref, w_ref, pw_ref, sc_ref, h_ref, y_ref, *, tiles_per_seq):
    t = pl.program_id(0) % tiles_per_seq
    rows = x_ref.shape[0]
    xb = x_ref[...].astype(BF16)
    xc = _dot(xb, w_ref[:, :HALF_W])
    h_ref[:, :HALF_W] = xc
    for c in range(1, w_ref.shape[1] // HALF_W):
        sl = slice(c * HALF_W, (c + 1) * HALF_W)
        h_ref[:, sl] = _dot(xb, w_ref[:, sl])
    halo = _dot(xh_ref[...].astype(BF16), w_ref[:, :HALF_W])
    _pool_tile(jnp.where(t == 0, 0.0, halo), xc, t * rows, pw_ref, sc_ref, y_ref)


def _pool(halo_src, x_src, w_bf16, scale, *, rows, pos0):
    w = len(POOL_WINDOWS) * C_GD
    n_rows = x_src.shape[0]
    return pl.pallas_call(
        functools.partial(_pool_kernel, pos0=pos0),
        grid=(n_rows // rows,),
        in_specs=[pl.BlockSpec((POOL_HALO, w), lambda i: (i, 0)),
                  pl.BlockSpec((rows, w), lambda i: (i, 0)),
                  _resident((len(POOL_WINDOWS), C_GD, C_GD)), _resident((1, w))],
        out_specs=pl.BlockSpec((rows, w), lambda i: (i, 0)),
        out_shape=jax.ShapeDtypeStruct((n_rows, w), F32),
        compiler_params=_params(("parallel",), 24),
        name="pool_mix",
    )(halo_src, x_src, w_bf16, scale.reshape(1, w))


def _proj_pool(x, w_bf16, pool_w_bf16, scale):
    n, d = x.shape
    n_out = w_bf16.shape[1]
    w = len(POOL_WINDOWS) * C_GD
    tile = PROMPT_TILE
    halo_step = tile // POOL_HALO
    vmem = (2 * (tile * d * 4 + POOL_HALO * d * 4 + tile * n_out * 4 + tile * w * 2)
            + d * n_out * 2 + 6 * tile * w * 4)
    return pl.pallas_call(
        functools.partial(_proj_pool_kernel, tiles_per_seq=SEQ // tile),
        grid=(n // tile,),
        in_specs=[pl.BlockSpec((POOL_HALO, d), lambda i: (jnp.maximum(i * halo_step - 1, 0), 0)),
                  pl.BlockSpec((tile, d), lambda i: (i, 0)),
                  _resident((d, n_out)),
                  _resident((len(POOL_WINDOWS), C_GD, C_GD)), _resident((1, w))],
        out_specs=[pl.BlockSpec((tile, n_out), lambda i: (i, 0)),
                   pl.BlockSpec((tile, w), lambda i: (i, 0))],
        out_shape=[jax.ShapeDtypeStruct((n, n_out), F32), jax.ShapeDtypeStruct((n, w), BF16)],
        compiler_params=_params(("parallel",), vmem / MIB + 8),
        name="in_proj_pool",
    )(x, x, w_bf16, pool_w_bf16, scale.reshape(1, w))


def _hgrn_kernel(q_ref, f_ref, i_ref, g_ref, s0_ref, lb_ref, ng_ref, o_ref, sfin_ref, s_scr,
                 *, rows, in_rows, valid, seqs):
    t = pl.program_id(1)
    c_rows = HGRN_CHUNK

    @pl.when(t == 0)
    def _():
        s_scr[...] = s0_ref[...]

    r_io = lax.broadcasted_iota(jnp.int32, (c_rows, c_rows), 0)
    c_io = lax.broadcasted_iota(jnp.int32, (c_rows, c_rows), 1)
    causal = r_io >= c_io
    ltri = jnp.where(causal, 1.0, 0.0).astype(BF16)
    lgrp = jnp.where(c_io < (r_io // HGRN_SUB) * HGRN_SUB + HGRN_SUB // 2, 1.0, 0.0).astype(BF16)
    lsum = jnp.concatenate([ltri, lgrp], axis=0)
    eye = r_io == c_io
    row_id = lax.broadcasted_iota(jnp.int32, (c_rows, D_HK), 0)

    heads = [slice(hd * D_HK, (hd + 1) * D_HK) for hd in range(D_HEADS)]

    def load(ref, item, sl):
        r0, seq = item
        if in_rows == rows:
            return ref[pl.ds(r0, c_rows), sl]
        return jnp.concatenate([ref[seq * in_rows:(seq + 1) * in_rows, sl],
                                jnp.zeros((c_rows - in_rows, D_HK), F32)], axis=0)

    def decay_sums(item):
        gates = []
        for sl in heads:
            lb = lb_ref[:, sl]
            half_span = 0.5 * (1.0 - lb)
            f = (lb + half_span) + half_span * jnp.tanh(0.5 * load(f_ref, item, sl))
            logf = jnp.log2(f)
            kk = 1.0 - f
            if valid < rows:
                live = (t * rows + item[0] + row_id) < valid
                logf = jnp.where(live, logf, 0.0)
                kk = jnp.where(live, kk, 0.0)
            sums = _dot_exact_lhs(lsum, logf)
            gates.append((kk, sums[:c_rows, :], sums[c_rows:, :]))
        return gates

    def chunk_scores(item, gates):
        attns, queries = [], []
        for sl, (kk, cg, ref) in zip(heads, gates):
            q = _silu(load(q_ref, item, sl))
            qd = (q * jnp.exp2(cg - ref)).astype(BF16)
            blocks = []
            for i in range(c_rows // HGRN_SUB):
                ref_i = ref[i * HGRN_SUB:i * HGRN_SUB + 1, :]
                e = jnp.where(row_id < (i + 1) * HGRN_SUB, ref_i - cg, 0.0)
                k_i = (kk * jnp.exp2(e)).astype(BF16)
                blocks.append(_dot_nt(qd[i * HGRN_SUB:(i + 1) * HGRN_SUB, :], k_i))
            attns.append(jnp.where(causal, jnp.concatenate(blocks, axis=0), 0.0).astype(BF16))
            queries.append((q * jnp.exp2(cg)).astype(BF16))
        return attns, queries

    def advance_state(item, gates, attns, queries):
        seq = item[1]
        outs = []
        for hd, (sl, (kk, cg, _)) in enumerate(zip(heads, gates)):
            vb = load(i_ref, item, sl).astype(BF16)
            state = s_scr[seq, hd]
            outs.append(_dot(queries[hd], state.astype(BF16)) + _dot(attns[hd], vb))
            g_last = cg[c_rows - 1:c_rows, :]
            kd = kk * jnp.exp2(g_last - cg)
            decay_col = jnp.sum(
                jnp.where(eye, jnp.broadcast_to(jnp.exp2(g_last), (c_rows, D_HK)), 0.0),
                axis=1, keepdims=True)
            s_scr[seq, hd] = state * decay_col + _dot(kd.T.astype(BF16), vb)
        return outs

    def finish(item, outs):
        r0, seq = item
        for sl, o in zip(heads, outs):
            o = o * lax.rsqrt(jnp.mean(o * o, axis=-1, keepdims=True) + RMS_EPS) * ng_ref[...]
            o = (o * _silu(load(g_ref, item, sl))).astype(o_ref.dtype)
            if in_rows == rows:
                o_ref[pl.ds(r0, c_rows), sl] = o
            else:
                o_ref[seq * in_rows:(seq + 1) * in_rows, sl] = o[:in_rows, :]

    def run_items(items):
        gates = [decay_sums(item) for item in items]
        outs = []
        for item, gt in zip(items, gates):
            attns, queries = chunk_scores(item, gt)
            outs.append(advance_state(item, gt, attns, queries))
        for item, o in zip(items, outs):
            finish(item, o)

    if in_rows == rows:
        group = min(HGRN_GROUP, rows // c_rows)

        def chunk_group(c, carry):
            run_items([(pl.multiple_of((c * group + k) * c_rows, c_rows), 0) for k in range(group)])
            return carry

        lax.fori_loop(0, rows // (c_rows * group), chunk_group, 0)
    else:
        run_items([(0, seq) for seq in range(seqs)])

    @pl.when(t == pl.num_programs(1) - 1)
    def _():
        sfin_ref[...] = s_scr[...]


def _hgrn(src, s0, lb, norm_g, *, bsz, length, rows, in_rows, valid, seqs, out_dtype):
    w = D_HEADS * D_HK
    tiles = length // rows
    assert seqs == 1 or tiles == 1
    blk_rows = seqs * in_rows
    col = lambda k: pl.BlockSpec((blk_rows, w), lambda b, t: (b * tiles + t, 1 + k))
    state_spec = pl.BlockSpec((seqs, D_HEADS, D_HK, D_HK), lambda b, t: (b, 0, 0, 0))
    return pl.pallas_call(
        functools.partial(_hgrn_kernel, rows=rows, in_rows=in_rows, valid=valid, seqs=seqs),
        grid=(bsz // seqs, tiles),
        in_specs=[col(0), col(1), col(2), col(3), state_spec, _resident((1, w)),
                  _resident((1, D_HK))],
        out_specs=[pl.BlockSpec((blk_rows, w), lambda b, t: (b * tiles + t, 0)), state_spec],
        out_shape=[jax.ShapeDtypeStruct((bsz * tiles * in_rows, w), out_dtype),
                   jax.ShapeDtypeStruct((bsz, D_HEADS, D_HK, D_HK), F32)],
        scratch_shapes=[pltpu.VMEM((seqs, D_HEADS, D_HK, D_HK), F32)],
        compiler_params=_params(("parallel", "arbitrary"), 2 * 5 * blk_rows * w * 4 / MIB + 24),
        name="hgrn2",
    )(src, src, src, src, s0, lb.reshape(1, w), norm_g.reshape(1, D_HK))


def kernel(x_prompt, x_sample, cache_k, cache_v, state_pool, state_hgrn, page_table, w_in_even, w_out_even, gmlp_ws, gmlp_bs, gmlp_ln_g, gmlp_ln_b, w_in_odd, w_out_odd, pool_w, pool_scale, hgrn_lb_param, hgrn_norm_g, ln_mix_g, ln_mix_b, ln_ffn_g, ln_ffn_b, ffn_w1, ffn_w2):
    w = HALF_W
    xp = x_prompt.reshape(N_PROMPT, D_MODEL)
    xs = x_sample.reshape(N_SAMPLE, D_MODEL)
    pad_tokens = ((0, 0), (0, SUBLANES - DEC_SEQ), (0, 0))

    col_scale = jnp.where(jnp.arange(EVEN_IN) < w, SCORE_SCALE, 1.0)
    w_even = (w_in_even[0] * col_scale).astype(BF16)
    kb, ugv, qt, vt, kt32, vt32 = _proj_even_prompt(x_prompt, w_even, w_even[:, :3 * w].T)
    h0s = _proj(xs, w_even, SAMPLE_TILE)
    hs = h0s.reshape(DEC_BATCH, DEC_SEQ, EVEN_IN)
    oa_prompt, oa_sample = _moba(page_table, qt, kb, vt, hs[..., :w],
                                 jnp.pad(hs[..., w:2 * w], pad_tokens), jnp.pad(hs[..., 2 * w:3 * w], pad_tokens),
                                 cache_k[0].transpose(0, 2, 3, 1), cache_v[0].transpose(0, 2, 3, 1))

    tril = jnp.tril(jnp.ones((B_CHUNK, B_CHUNK), bool))
    ws_prompt = jnp.where(tril[None], gmlp_ws[0], 0.0)
    owner = jnp.arange(B_CHUNK) // DEC_SEQ
    ws_sample = jnp.where(owner[:, None] == owner[None, :],
                          jnp.tile(ws_prompt[:, :DEC_SEQ, :DEC_SEQ], (1, DEC_BATCH, DEC_BATCH)), 0.0)
    bias_prompt = jnp.repeat(gmlp_bs[0].T, B_GD, axis=1)
    bias_sample = jnp.tile(bias_prompt[:DEC_SEQ], (DEC_BATCH, 1))
    grp = jnp.arange(w) // B_GD
    avg = jnp.where(grp[:, None] == grp[None, :], 1.0 / B_GD, 0.0).astype(BF16)
    ob_prompt, = _gmlp(ugv, 0, 1, ws_prompt.astype(BF16), bias_prompt, avg, gmlp_ln_g[0], gmlp_ln_b[0],
                       out_dtype=BF16, emit_vn=False)
    ob_sample, vn_sample = _gmlp(h0s, 3, 4, ws_sample.astype(BF16), bias_sample, avg, gmlp_ln_g[0],
                                 gmlp_ln_b[0], out_dtype=F32, emit_vn=True)

    layer0 = (w_out_even[0].astype(BF16), ln_mix_g[0], ln_mix_b[0],
              ffn_w1[0].astype(BF16), ffn_w2[0].astype(BF16), ln_ffn_g[0], ln_ffn_b[0])
    xp = _mix_ffn(oa_prompt, ob_prompt, xp, *layer0, FFN_TILE)
    xs = _mix_ffn(oa_sample.reshape(N_SAMPLE, w), ob_sample, xs, *layer0, SAMPLE_TILE)

    per_head = lambda t: t.reshape(BATCH, A_HEADS, A_HD, SEQ).transpose(0, 3, 1, 2)[None]
    new_k_prompt = per_head(kt32)
    new_v_prompt = per_head(vt32)
    new_k_sample = hs[..., w:2 * w].reshape(1, DEC_BATCH, DEC_SEQ, A_HEADS, A_HD)
    new_v_sample = hs[..., 2 * w:3 * w].reshape(1, DEC_BATCH, DEC_SEQ, A_HEADS, A_HD)
    new_gmlp_v_sample = vn_sample.reshape(1, DEC_BATCH, DEC_SEQ, w)

    w_odd = w_in_odd[0].astype(BF16)
    lb = jax.nn.softmax(hgrn_lb_param.astype(F32), axis=0)[0]
    pool_wb = pool_w[0].astype(BF16)

    h1p, oc_prompt = _proj_pool(xp, w_odd, pool_wb, pool_scale[0])
    od_prompt, s_prompt = _hgrn(h1p, jnp.zeros((BATCH, D_HEADS, D_HK, D_HK), F32), lb, hgrn_norm_g[0],
                                bsz=BATCH, length=SEQ, rows=HGRN_ROWS, in_rows=HGRN_ROWS, valid=SEQ,
                                seqs=1, out_dtype=BF16)

    h1s = _proj(xs, w_odd, SAMPLE_TILE).reshape(DEC_BATCH, DEC_SEQ, ODD_IN)
    h1s_pad = jnp.pad(h1s, pad_tokens).reshape(DEC_BATCH * SUBLANES, ODD_IN)
    halo_s = jnp.pad(state_pool[0], ((0, 0), (POOL_HALO - POOL_BUF, 0), (0, 0)))
    oc_sample = _pool(halo_s.reshape(DEC_BATCH * POOL_HALO, w), h1s_pad, pool_wb, pool_scale[0],
                      rows=SUBLANES, pos0=PAST_LEN)
    od_sample, s_sample = _hgrn(h1s_pad, state_hgrn[0], lb, hgrn_norm_g[0], bsz=DEC_BATCH,
                                length=HGRN_CHUNK, rows=HGRN_CHUNK, in_rows=SUBLANES, valid=DEC_SEQ,
                                seqs=HGRN_SAMPLE_SEQS, out_dtype=F32)
    real_rows = lambda t: t.reshape(DEC_BATCH, SUBLANES, w)[:, :DEC_SEQ].reshape(N_SAMPLE, w)

    layer1 = (w_out_odd[0].astype(BF16), ln_mix_g[1], ln_mix_b[1],
              ffn_w1[1].astype(BF16), ffn_w2[1].astype(BF16), ln_ffn_g[1], ln_ffn_b[1])
    xp = _mix_ffn(oc_prompt, od_prompt, xp, *layer1, FFN_TILE)
    xs = _mix_ffn(real_rows(oc_sample), real_rows(od_sample), xs, *layer1, SAMPLE_TILE)

    new_pool_prompt = h1p.reshape(BATCH, SEQ, ODD_IN)[:, SEQ - POOL_BUF:, :w][None]
    new_pool_sample = jnp.concatenate([state_pool[0][:, DEC_SEQ:], h1s[..., :w]], axis=1)[None]
    return (xp.reshape(BATCH, SEQ, D_MODEL), xs.reshape(DEC_BATCH, DEC_SEQ, D_MODEL),
            new_k_prompt, new_v_prompt, new_k_sample, new_v_sample, new_gmlp_v_sample,
            new_pool_prompt, new_pool_sample, s_prompt[None], s_sample[None])
```

```python
import functools
import math

import jax
import jax.numpy as jnp
from jax import lax
from jax.experimental import pallas as pl
from jax.experimental.pallas import tpu as pltpu

F32 = jnp.float32
BF16 = jnp.bfloat16

D_MODEL = 1024
BATCH = 4
SEQ = 4096
DEPTH = 2
DEC_BATCH = 32
DEC_SEQ = 4
PAST_LEN = 8192
PAGE_SIZE = 128
HALF_W = D_MODEL // 2
A_HD = 64
A_HEADS = HALF_W // A_HD
MOBA_BLOCK = 256
MOBA_TOPK = 3
B_GROUPS = 8
B_GD = HALF_W // B_GROUPS
B_CHUNK = 128
POOL_WINDOWS = (2, 4, 8, 16)
C_GD = HALF_W // len(POOL_WINDOWS)
POOL_BUF = max(POOL_WINDOWS) - 1
D_HK = 128
D_HEADS = HALF_W // D_HK
D_FF = 4 * D_MODEL
EVEN_IN = 5 * HALF_W
ODD_IN = 5 * HALF_W
ALPHA = (2 * DEPTH) ** 0.25
LN_EPS = 1e-5
RMS_EPS = 1e-6
NEG = -1e30

N_PROMPT = BATCH * SEQ
N_SAMPLE = DEC_BATCH * DEC_SEQ
N_PAGES = PAST_LEN // PAGE_SIZE
N_PAST_BLOCKS = PAST_LEN // MOBA_BLOCK
PAGES_PER_BLOCK = MOBA_BLOCK // PAGE_SIZE
N_KEY_BLOCKS = SEQ // MOBA_BLOCK

LANES = 128
SUBLANES = 8
PROMPT_TILE = 1024
SAMPLE_TILE = N_SAMPLE
FF_CHUNK = 1024
FFN_TILE = 1024
FFN_GROUP_ROWS = 256
MOBA_LOOKAHEAD = 7
SAMPLE_PAGES_PER_TILE = DEC_BATCH * N_PAGES // (BATCH * N_KEY_BLOCKS)
TILES_PER_SAMPLE = N_PAGES // SAMPLE_PAGES_PER_TILE
assert SAMPLE_PAGES_PER_TILE * BATCH * N_KEY_BLOCKS == DEC_BATCH * N_PAGES
assert TILES_PER_SAMPLE * SAMPLE_PAGES_PER_TILE == N_PAGES and N_KEY_BLOCKS % TILES_PER_SAMPLE == 0
assert SAMPLE_PAGES_PER_TILE % PAGES_PER_BLOCK == 0
V_TAIL = 16
V_SLAB = A_HD + V_TAIL
V_ROWS = A_HEADS * V_SLAB
assert A_HEADS % (MOBA_LOOKAHEAD + 1) == 0
SCORE_SCALE = (A_HD ** -0.5) * math.log2(math.e)
HGRN_CHUNK = 128
HGRN_SUB = 32
HGRN_ROWS = 1024
HGRN_GROUP = 2
HGRN_SAMPLE_SEQS = 8
GMLP_GROUP = 8
POOL_HALO = 16
MIB = 1024 * 1024


def _dot(a, b):
    return jnp.dot(a, b, preferred_element_type=F32)


def _dot_nt(a, b):
    return lax.dot_general(a, b, (((1,), (1,)), ((), ())), preferred_element_type=F32)


def _split3(x):
    p0 = x.astype(BF16)
    r1 = x - p0.astype(F32)
    p1 = r1.astype(BF16)
    p2 = (r1 - p1.astype(F32)).astype(BF16)
    return p0, p1, p2


def _dot_exact_lhs(m_bf16, x):
    p0, p1, p2 = _split3(x)
    return _dot(m_bf16, p0) + _dot(m_bf16, p1) + _dot(m_bf16, p2)


def _dot_exact_rhs(x, m_bf16, pieces=3):
    parts = _split3(x)[:pieces]
    out = _dot(parts[0], m_bf16)
    for part in parts[1:]:
        out = out + _dot(part, m_bf16)
    return out


def _gelu_tanh(x):
    return 0.5 * x * (1.0 + jnp.tanh(0.7978845608028654 * (x + 0.044715 * (x * x * x))))


def _silu(x):
    h = 0.5 * x
    return h + h * jnp.tanh(h)


def _layer_norm(z, g, b):
    mu = jnp.mean(z, axis=-1, keepdims=True)
    zc = z - mu
    var = jnp.mean(zc * zc, axis=-1, keepdims=True)
    return zc * lax.rsqrt(var + LN_EPS) * g + b


def _params(sem, vmem_mib):
    return pltpu.CompilerParams(dimension_semantics=sem, vmem_limit_bytes=int(vmem_mib * MIB))


def _resident(shape):
    return pl.BlockSpec(shape, lambda *_: (0,) * len(shape), pipeline_mode=pl.Buffered(1))


def _proj_kernel(x_ref, w_ref, h_ref):
    xb = x_ref[...].astype(BF16)
    for c in range(w_ref.shape[1] // HALF_W):
        sl = slice(c * HALF_W, (c + 1) * HALF_W)
        h_ref[:, sl] = _dot(xb, w_ref[:, sl])


def _proj(x, w_bf16, tile):
    n, d = x.shape
    n_out = w_bf16.shape[1]
    vmem = 2 * (tile * d * 4 + tile * n_out * 4) + d * n_out * 2
    return pl.pallas_call(
        _proj_kernel,
        grid=(n // tile,),
        in_specs=[pl.BlockSpec((tile, d), lambda i: (i, 0)), _resident((d, n_out))],
        out_specs=pl.BlockSpec((tile, n_out), lambda i: (i, 0)),
        out_shape=jax.ShapeDtypeStruct((n, n_out), F32),
        compiler_params=_params(("parallel",), vmem / MIB + 8),
        name="in_proj",
    )(x, w_bf16)


def _proj_even_kernel(x_ref, wn_ref, wt_ref, kb_ref, ugv_ref, qt_ref, vt_ref, kt32_ref, vt32_ref):
    w = HALF_W
    xb = x_ref[...].astype(BF16)
    k = _dot(xb, wn_ref[:, w:2 * w])
    for half in range(PROMPT_TILE // MOBA_BLOCK):
        kb_ref[half] = k[half * MOBA_BLOCK:(half + 1) * MOBA_BLOCK, :].astype(BF16)
    ugv_ref[:, :w] = _dot(xb, wn_ref[:, 3 * w:4 * w])
    ugv_ref[:, w:] = _dot(xb, wn_ref[:, 4 * w:])
    qt = _dot_nt(wt_ref[:w, :], xb)
    kt32_ref[...] = k.T
    vt = _dot_nt(wt_ref[2 * w:, :], xb)
    vt32_ref[...] = vt
    ones_row = jnp.where(lax.broadcasted_iota(jnp.int32, (V_TAIL, MOBA_BLOCK), 0) == 0,
                         1.0, 0.0).astype(BF16)
    for half in range(PROMPT_TILE // MOBA_BLOCK):
        cols = slice(half * MOBA_BLOCK, (half + 1) * MOBA_BLOCK)
        qt_ref[half] = qt[:, cols].astype(BF16)
        for h in range(A_HEADS):
            vt_ref[half, h * V_SLAB:h * V_SLAB + A_HD, :] = vt[h * A_HD:(h + 1) * A_HD, cols].astype(BF16)
            vt_ref[half, h * V_SLAB + A_HD:(h + 1) * V_SLAB, :] = ones_row


def _proj_even_prompt(x_prompt, w_nat, w_tr):
    w = HALF_W
    tiles = SEQ // PROMPT_TILE
    per_tile = PROMPT_TILE // MOBA_BLOCK
    blocked = lambda shape: pl.BlockSpec((None, per_tile) + shape, lambda b, t: (b, t, 0, 0))
    vmem = (2 * (PROMPT_TILE * D_MODEL * 4 + PROMPT_TILE * 2 * w * 4 + 3 * PROMPT_TILE * w * 2
                 + 2 * PROMPT_TILE * w * 4) + 2 * D_MODEL * 3 * w * 2 + 4 * PROMPT_TILE * w * 4)
    return pl.pallas_call(
        _proj_even_kernel,
        grid=(BATCH, tiles),
        in_specs=[pl.BlockSpec((None, PROMPT_TILE, D_MODEL), lambda b, t: (b, t, 0)),
                  _resident((D_MODEL, EVEN_IN)), _resident((3 * w, D_MODEL))],
        out_specs=[blocked((MOBA_BLOCK, w)),
                   pl.BlockSpec((PROMPT_TILE, 2 * w), lambda b, t: (b * tiles + t, 0)),
                   blocked((w, MOBA_BLOCK)), blocked((V_ROWS, MOBA_BLOCK)),
                   pl.BlockSpec((None, w, PROMPT_TILE), lambda b, t: (b, 0, t)),
                   pl.BlockSpec((None, w, PROMPT_TILE), lambda b, t: (b, 0, t))],
        out_shape=[jax.ShapeDtypeStruct((BATCH, N_KEY_BLOCKS, MOBA_BLOCK, w), BF16),
                   jax.ShapeDtypeStruct((N_PROMPT, 2 * w), F32),
                   jax.ShapeDtypeStruct((BATCH, N_KEY_BLOCKS, w, MOBA_BLOCK), BF16),
                   jax.ShapeDtypeStruct((BATCH, N_KEY_BLOCKS, V_ROWS, MOBA_BLOCK), BF16),
                   jax.ShapeDtypeStruct((BATCH, w, SEQ), F32),
                   jax.ShapeDtypeStruct((BATCH, w, SEQ), F32)],
        compiler_params=_params(("parallel", "parallel"), vmem / MIB + 8),
        name="in_proj_even",
    )(x_prompt, w_nat, w_tr)


def _mix_ffn_kernel(oa_ref, ob_ref, x_ref, wo_ref, g1_ref, b1_ref, w1_ref, w2_ref, g2_ref, b2_ref,
                    y_ref):
    tile = x_ref.shape[0]
    group_rows = min(FFN_GROUP_ROWS, tile)
    groups = [slice(r, r + group_rows) for r in range(0, tile, group_rows)]
    mixed = [_dot(oa_ref[rows, :].astype(BF16), wo_ref[:HALF_W, :])
             + _dot(ob_ref[rows, :].astype(BF16), wo_ref[HALF_W:, :]) for rows in groups]
    for rows, mix in zip(groups, mixed):
        x = _layer_norm(ALPHA * x_ref[rows, :] + mix, g1_ref[...], b1_ref[...])
        xb = x.astype(BF16)
        acc = jnp.zeros(x.shape, F32)
        for c in range(D_FF // FF_CHUNK):
            sl = slice(c * FF_CHUNK, (c + 1) * FF_CHUNK)
            hid = jnp.maximum(_dot(xb, w1_ref[:, sl]), 0.0)
            acc = acc + _dot((hid * hid).astype(BF16), w2_ref[sl, :])
        y_ref[rows, :] = _layer_norm(ALPHA * x + acc, g2_ref[...], b2_ref[...])


def _mix_ffn(oa, ob, x, wo_bf16, g1, b1, w1_layers, w2_layers, layer, g2, b2, tile):
    n, d = x.shape
    row = lambda i: (i, 0)
    vec = lambda v: v.reshape(1, d)
    of_layer = lambda shape: pl.BlockSpec((None,) + shape, lambda i: (layer, 0, 0),
                                          pipeline_mode=pl.Buffered(1))
    vmem = (2 * (2 * tile * HALF_W * oa.dtype.itemsize + 2 * tile * d * 4)
            + (d * d + 2 * d * D_FF) * 2
            + 8 * min(tile, FFN_GROUP_ROWS) * FF_CHUNK * 4)
    return pl.pallas_call(
        _mix_ffn_kernel,
        grid=(n // tile,),
        in_specs=[pl.BlockSpec((tile, HALF_W), row), pl.BlockSpec((tile, HALF_W), row),
                  pl.BlockSpec((tile, d), row), _resident((d, d)),
                  _resident((1, d)), _resident((1, d)),
                  of_layer((d, D_FF)), of_layer((D_FF, d)),
                  _resident((1, d)), _resident((1, d))],
        out_specs=pl.BlockSpec((tile, d), row),
        out_shape=jax.ShapeDtypeStruct((n, d), F32),
        compiler_params=_params(("parallel",), vmem / MIB + 8),
        name="mix_ffn_ln",
    )(oa, ob, x, wo_bf16, vec(g1), vec(b1), w1_layers, w2_layers, vec(g2), vec(b2))


def _moba_kernel(pt_ref, qt_ref, k_ref, vt_ref, qs_ref, kn_ref, vn_ref, *refs):
    del pt_ref
    kp_refs = refs[:SAMPLE_PAGES_PER_TILE]
    vp_refs = refs[SAMPLE_PAGES_PER_TILE:2 * SAMPLE_PAGES_PER_TILE]
    (o_ref, os_ref, kmt_scr, qtz_scr, selb_scr, m_scr, acc_scr, s_scr, ref_scr, a_scr,
     sm_scr, sl_scr, sgate_scr, sacc_scr) = refs[2 * SAMPLE_PAGES_PER_TILE:]
    j = pl.program_id(1)
    nb = k_ref.shape[0]
    blk = MOBA_BLOCK
    pair_w = 2 * A_HD
    w = A_HEADS * A_HD
    nrow = DEC_SEQ * A_HEADS
    part = (pl.program_id(0) * nb + j) % TILES_PER_SAMPLE
    wide = lambda c: jnp.broadcast_to(c, (nrow, LANES))

    @pl.when(j == 0)
    def _():
        rows = [jnp.sum(k_ref[n].astype(F32), axis=0, keepdims=True) * (1.0 / blk)
                for n in range(nb)]
        kmean = jnp.concatenate(rows, axis=0)
        tiled = jnp.concatenate([kmean] * A_HEADS, axis=0)
        rh = lax.broadcasted_iota(jnp.int32, tiled.shape, 0) // nb
        ch = lax.broadcasted_iota(jnp.int32, tiled.shape, 1) // A_HD
        kmt_scr[...] = jnp.where(rh == ch, tiled, 0.0)

    head_mask = (lax.broadcasted_iota(jnp.int32, (A_HEADS, w), 1) // A_HD
                 == lax.broadcasted_iota(jnp.int32, (A_HEADS, w), 0))
    qs = qs_ref[...]
    qbd = jnp.concatenate(
        [jnp.where(head_mask, jnp.broadcast_to(qs[i:i + 1, :], (A_HEADS, w)), 0.0)
         for i in range(DEC_SEQ)], axis=0).astype(BF16)
    page_scores = [_dot(qbd, kp_refs[i][...].reshape(w, PAGE_SIZE).astype(BF16))
                   for i in range(SAMPLE_PAGES_PER_TILE)]

    def sample_blocks():
        probs = []
        for bl in range(SAMPLE_PAGES_PER_TILE // PAGES_PER_BLOCK):
            s = jnp.concatenate(page_scores[bl * PAGES_PER_BLOCK:(bl + 1) * PAGES_PER_BLOCK], axis=1)
            blk_id = part * (SAMPLE_PAGES_PER_TILE // PAGES_PER_BLOCK) + bl
            m = jnp.max(s, axis=-1, keepdims=True)
            p = jnp.exp2(s - m)
            sm_scr[blk_id] = wide(m)
            sl_scr[blk_id] = wide(jnp.sum(p, axis=-1, keepdims=True))
            sgate_scr[blk_id] = wide(jnp.sum(s, axis=-1, keepdims=True) * (1.0 / MOBA_BLOCK))
            probs.append(p.astype(BF16))
        for bl, p in enumerate(probs):
            blk_id = part * (SAMPLE_PAGES_PER_TILE // PAGES_PER_BLOCK) + bl
            acc = None
            for r in range(PAGES_PER_BLOCK):
                vpage = vp_refs[bl * PAGES_PER_BLOCK + r][...].reshape(w, PAGE_SIZE).astype(BF16)
                pv = _dot_nt(p[:, r * PAGE_SIZE:(r + 1) * PAGE_SIZE], vpage)
                acc = pv if acc is None else acc + pv
            sacc_scr[blk_id] = acc

    qts = qt_ref[...]
    k0, k1, k2 = _split3(kmt_scr[...])
    gate = _dot(k0, qts) + _dot(k1, qts) + _dot(k2, qts)

    n_io = lax.broadcasted_iota(jnp.int32, (nb, blk), 0)
    half = lax.broadcasted_iota(jnp.int32, (pair_w, blk), 0) // A_HD
    for h in range(A_HEADS):
        g = jnp.where(n_io < j, gate[h * nb:(h + 1) * nb, :], NEG)
        bias = jnp.full((nb, blk), NEG, F32)
        for _ in range(MOBA_TOPK):
            mx = jnp.max(g, axis=0, keepdims=True)
            cand = jnp.where((g == mx) & (mx > 0.5 * NEG), n_io, nb)
            pick = n_io == jnp.min(cand, axis=0, keepdims=True)
            bias = jnp.where(pick, 0.0, bias)
            g = jnp.where(pick, NEG, g)
        selb_scr[h * nb:(h + 1) * nb, :] = bias
        pr = h // 2
        qpair = qts[pr * pair_w:(pr + 1) * pair_w, :]
        qtz_scr[h] = jnp.where(half == (h % 2), qpair, jnp.zeros_like(qpair))

    causal = (lax.broadcasted_iota(jnp.int32, (blk, blk), 0)
              <= lax.broadcasted_iota(jnp.int32, (blk, blk), 1))

    def fold_rows(x, op):
        return op(x.reshape(blk // SUBLANES, SUBLANES, blk), axis=0)

    def stage_scores(n, h, slot, own):
        pr = h // 2
        s = _dot(k_ref[n, :, pr * pair_w:(pr + 1) * pair_w], qtz_scr[h])
        if own:
            s = jnp.where(causal, s, NEG)
        s_scr[slot] = s
        col_max = jnp.max(fold_rows(s, jnp.max), axis=0, keepdims=True)
        if own:
            m_scr[h] = col_max
            ref_scr[slot] = col_max
        else:
            bias = selb_scr[pl.ds(h * nb + n, 1), :]
            m_old = m_scr[h]
            m_new = jnp.maximum(m_old, col_max + bias)
            m_scr[h] = m_new
            a_scr[slot] = jnp.exp2(m_old - m_new)
            ref_scr[slot] = m_new - bias

    def accumulate(n, h, slot, own):
        vs = slice(h * V_SLAB, (h + 1) * V_SLAB)
        p = jnp.exp2(s_scr[slot] - ref_scr[slot])
        pv = _dot(vt_ref[n, vs, :], p.astype(BF16))
        if own:
            acc_scr[vs, :] = pv
        else:
            acc_scr[vs, :] = a_scr[slot] * acc_scr[vs, :] + pv

    n_slots = MOBA_LOOKAHEAD + 1

    def stage_ahead(n, n_next, h, own):
        ha = h + MOBA_LOOKAHEAD
        if ha < A_HEADS:
            stage_scores(n, ha, ha % n_slots, own=own)
        else:
            stage_scores(n_next, ha - A_HEADS, ha % n_slots, own=False)

    for h in range(MOBA_LOOKAHEAD):
        stage_scores(j, h, h % n_slots, own=True)
    sample_blocks()
    for h in range(A_HEADS):
        stage_ahead(j, 0, h, own=True)
        accumulate(j, h, h % n_slots, own=True)

    def past_block(n, carry):
        for h in range(A_HEADS):
            stage_ahead(n, n + 1, h, own=False)
            accumulate(n, h, h % n_slots, own=False)
        return carry

    lax.fori_loop(0, j, past_block, 0)

    heads_out = []
    for h in range(A_HEADS):
        slab = acc_scr[h * V_SLAB:(h + 1) * V_SLAB, :]
        heads_out.append(slab[:A_HD, :] / slab[A_HD:A_HD + 1, :])
    o_ref[...] = jnp.concatenate(heads_out, axis=0).T.astype(o_ref.dtype)

    @pl.when(part == TILES_PER_SAMPLE - 1)
    def _():
        tile4 = lambda c: jnp.concatenate([c] * (w // LANES), axis=1)
        s = _dot_nt(qbd, kn_ref[...].astype(BF16))
        col = lax.broadcasted_iota(jnp.int32, s.shape, 1)
        qi = lax.broadcasted_iota(jnp.int32, s.shape, 0) // A_HEADS
        s = jnp.where(col <= qi, s, NEG)
        m_col = jnp.max(s, axis=-1, keepdims=True)
        p = jnp.exp2(s - m_col)
        m_own = wide(m_col)
        l_own = wide(jnp.sum(p, axis=-1, keepdims=True))
        o_own = _dot(p, vn_ref[...])

        for _ in range(MOBA_TOPK):
            best = jnp.full((nrow, LANES), NEG, F32)
            bidx = jnp.zeros((nrow, LANES), jnp.int32)
            for n in range(N_PAST_BLOCKS):
                gn = sgate_scr[n]
                upd = gn > best
                best = jnp.where(upd, gn, best)
                bidx = jnp.where(upd, n, bidx)
            for n in range(N_PAST_BLOCKS):
                sgate_scr[n] = jnp.where(bidx == n, -jnp.inf, sgate_scr[n])

        m_all = m_own
        for n in range(N_PAST_BLOCKS):
            m_all = jnp.maximum(m_all, jnp.where(sgate_scr[n] == -jnp.inf, sm_scr[n], NEG))
        w_own = jnp.exp2(m_own - m_all)
        l_all = w_own * l_own
        o_all = tile4(w_own) * o_own
        for n in range(N_PAST_BLOCKS):
            wn = jnp.where(sgate_scr[n] == -jnp.inf, jnp.exp2(sm_scr[n] - m_all), 0.0)
            l_all = l_all + wn * sl_scr[n]
            o_all = o_all + tile4(wn) * sacc_scr[n]
        out = o_all / tile4(l_all)
        for i in range(DEC_SEQ):
            rows = out[i * A_HEADS:(i + 1) * A_HEADS, :]
            os_ref[i:i + 1, :] = jnp.sum(jnp.where(head_mask, rows, 0.0), axis=0, keepdims=True)


def _moba(page_table, qt, kb, vt, q_s, k_new, v_new, cache_kt, cache_vt):
    bsz, nb = qt.shape[:2]
    w = A_HEADS * A_HD
    nrow = DEC_SEQ * A_HEADS
    n_slots = MOBA_LOOKAHEAD + 1
    step = lambda b, j: b * nb + j
    sample_of = lambda b, j: step(b, j) // TILES_PER_SAMPLE
    first_page = lambda b, j: (step(b, j) % TILES_PER_SAMPLE) * SAMPLE_PAGES_PER_TILE
    page_spec = lambda i: pl.BlockSpec(
        (None, A_HEADS, A_HD, PAGE_SIZE),
        lambda b, j, pt: (pt[sample_of(b, j), first_page(b, j) + i], 0, 0, 0))
    per_sample = lambda rows: pl.BlockSpec((None, rows, w), lambda b, j, pt: (sample_of(b, j), 0, 0))
    whole_row = lambda shape: pl.BlockSpec((None,) + shape, lambda b, j, pt: (b, 0, 0, 0),
                                           pipeline_mode=pl.Buffered(1))
    grid_spec = pltpu.PrefetchScalarGridSpec(
        num_scalar_prefetch=1,
        grid=(bsz, nb),
        in_specs=([pl.BlockSpec((None, None, w, MOBA_BLOCK), lambda b, j, pt: (b, j, 0, 0)),
                   whole_row((nb, MOBA_BLOCK, w)), whole_row((nb, V_ROWS, MOBA_BLOCK)),
                   per_sample(DEC_SEQ), per_sample(SUBLANES), per_sample(SUBLANES)]
                  + [page_spec(i) for i in range(SAMPLE_PAGES_PER_TILE)]
                  + [page_spec(i) for i in range(SAMPLE_PAGES_PER_TILE)]),
        out_specs=[pl.BlockSpec((MOBA_BLOCK, w), lambda b, j, pt: (step(b, j), 0)),
                   per_sample(DEC_SEQ)],
        scratch_shapes=[pltpu.VMEM((A_HEADS * nb, w), F32),
                        pltpu.VMEM((A_HEADS, 2 * A_HD, MOBA_BLOCK), BF16),
                        pltpu.VMEM((A_HEADS * nb, MOBA_BLOCK), F32),
                        pltpu.VMEM((A_HEADS, 1, MOBA_BLOCK), F32),
                        pltpu.VMEM((V_ROWS, MOBA_BLOCK), F32),
                        pltpu.VMEM((n_slots, MOBA_BLOCK, MOBA_BLOCK), F32),
                        pltpu.VMEM((n_slots, 1, MOBA_BLOCK), F32),
                        pltpu.VMEM((n_slots, 1, MOBA_BLOCK), F32),
                        pltpu.VMEM((N_PAST_BLOCKS, nrow, LANES), F32),
                        pltpu.VMEM((N_PAST_BLOCKS, nrow, LANES), F32),
                        pltpu.VMEM((N_PAST_BLOCKS, nrow, LANES), F32),
                        pltpu.VMEM((N_PAST_BLOCKS, nrow, w), F32)],
    )
    vmem = (nb * MOBA_BLOCK * (w + V_ROWS) * 2 + 2 * (w * MOBA_BLOCK * 2 + MOBA_BLOCK * w * 2)
            + n_slots * MOBA_BLOCK * MOBA_BLOCK * 4
            + 2 * 2 * SAMPLE_PAGES_PER_TILE * PAGE_SIZE * w * 4
            + N_PAST_BLOCKS * nrow * (3 * LANES + w) * 4 + 4 * MIB)
    return pl.pallas_call(
        _moba_kernel,
        grid_spec=grid_spec,
        out_shape=[jax.ShapeDtypeStruct((bsz * nb * MOBA_BLOCK, w), BF16),
                   jax.ShapeDtypeStruct((DEC_BATCH, DEC_SEQ, w), F32)],
        compiler_params=_params(("arbitrary", "arbitrary"), vmem / MIB + 6),
        name="moba",
    )(page_table, qt, kb, vt, q_s, k_new, v_new,
      *([cache_kt] * SAMPLE_PAGES_PER_TILE), *([cache_vt] * SAMPLE_PAGES_PER_TILE))


def _gmlp_kernel(u_ref, gv_ref, w_ref, bias_ref, avg_ref, lng_ref, lnb_ref, ob_ref, *vn_refs):
    chunks = [slice(c * B_CHUNK, (c + 1) * B_CHUNK) for c in range(u_ref.shape[0] // B_CHUNK)]
    avg = avg_ref[...]
    gvs = [_gelu_tanh(gv_ref[rows, :]) for rows in chunks]
    centred = [gv - _dot_exact_rhs(gv, avg) for gv in gvs]
    variances = [_dot_exact_rhs(gc * gc, avg, pieces=2) for gc in centred]
    pair_w = 2 * B_GD
    lane = lax.broadcasted_iota(jnp.int32, (B_CHUNK, pair_w), 1)
    for rows, gc, var in zip(chunks, centred, variances):
        vn = gc * lax.rsqrt(var + LN_EPS) * lng_ref[...] + lnb_ref[...]
        if vn_refs:
            vn_refs[0][rows, :] = vn
        vb = vn.astype(BF16)
        u = _gelu_tanh(u_ref[rows, :])
        for pr in range(B_GROUPS // 2):
            sl = slice(pr * pair_w, (pr + 1) * pair_w)
            vp = vb[:, sl]
            zero = jnp.zeros_like(vp)
            mixed = (_dot(w_ref[2 * pr], jnp.where(lane < B_GD, vp, zero))
                     + _dot(w_ref[2 * pr + 1], jnp.where(lane >= B_GD, vp, zero)))
            ob_ref[rows, sl] = (u[:, sl] * (mixed + bias_ref[:, sl])).astype(ob_ref.dtype)


def _gmlp(src, u_col, gv_col, w_masked, bias, avg, ln_g, ln_b, *, out_dtype, emit_vn):
    rows = src.shape[0]
    w = B_GROUPS * B_GD
    tile = B_CHUNK * min(GMLP_GROUP, rows // B_CHUNK)
    out_shape = [jax.ShapeDtypeStruct((rows, w), out_dtype)]
    out_specs = [pl.BlockSpec((tile, w), lambda c: (c, 0))]
    if emit_vn:
        out_shape.append(jax.ShapeDtypeStruct((rows, w), F32))
        out_specs.append(pl.BlockSpec((tile, w), lambda c: (c, 0)))
    return pl.pallas_call(
        _gmlp_kernel,
        grid=(rows // tile,),
        in_specs=[pl.BlockSpec((tile, w), lambda c: (c, u_col)),
                  pl.BlockSpec((tile, w), lambda c: (c, gv_col)),
                  _resident((B_GROUPS, B_CHUNK, B_CHUNK)), _resident((B_CHUNK, w)),
                  _resident((w, w)), _resident((1, w)), _resident((1, w))],
        out_specs=out_specs,
        out_shape=out_shape,
        compiler_params=_params(("parallel",), 2 * 4 * tile * w * 4 / MIB + 16),
        name="gmlp_gate",
    )(src, src, w_masked, bias, avg, ln_g.reshape(1, w), ln_b.reshape(1, w))


def _pool_tile(halo, x, pos_first, w_ref, sc_ref, y_ref):
    rows = x.shape[0]
    ext = jnp.concatenate([halo, x], axis=0)
    pos = pos_first + lax.broadcasted_iota(jnp.int32, (rows, C_GD), 0)
    for gi, win in enumerate(POOL_WINDOWS):
        sl = slice(gi * C_GD, (gi + 1) * C_GD)
        s = ext[:, sl]
        sh = 1
        while sh < win:
            s = s + pltpu.roll(s, sh, 0)
            sh *= 2
        cnt = jnp.minimum(win, pos + 1).astype(F32)
        pooled = s[POOL_HALO:, :] / cnt - x[:, sl]
        y_ref[:, sl] = (_dot(pooled.astype(BF16), w_ref[gi]) * sc_ref[:, sl]).astype(y_ref.dtype)


def _pool_kernel(halo_ref, x_ref, w_ref, sc_ref, y_ref, *, pos0):
    _pool_tile(halo_ref[...], x_ref[...], pos0, w_ref, sc_ref, y_ref)


def _proj_pool_kernel(xh_ref, x_ref, w_ref, pw_ref, sc_ref, h_ref, y_ref, *, tiles_per_seq):
    t = pl.program_id(0) % tiles_per_seq
    rows = x_ref.shape[0]
    xb = x_ref[...].astype(BF16)
    xc = _dot(xb, w_ref[:, :HALF_W])
    h_ref[:, :HALF_W] = xc
    for c in range(1, w_ref.shape[1] // HALF_W):
        sl = slice(c * HALF_W, (c + 1) * HALF_W)
        h_ref[:, sl] = _dot(xb, w_ref[:, sl])
    halo = _dot(xh_ref[...].astype(BF16), w_ref[:, :HALF_W])
    _pool_tile(jnp.where(t == 0, 0.0, halo), xc, t * rows, pw_ref, sc_ref, y_ref)


def _pool(halo_src, x_src, w_bf16, scale, *, rows, pos0):
    w = len(POOL_WINDOWS) * C_GD
    n_rows = x_src.shape[0]
    return pl.pallas_call(
        functools.partial(_pool_kernel, pos0=pos0),
        grid=(n_rows // rows,),
        in_specs=[pl.BlockSpec((POOL_HALO, w), lambda i: (i, 0)),
                  pl.BlockSpec((rows, w), lambda i: (i, 0)),
                  _resident((len(POOL_WINDOWS), C_GD, C_GD)), _resident((1, w))],
        out_specs=pl.BlockSpec((rows, w), lambda i: (i, 0)),
        out_shape=jax.ShapeDtypeStruct((n_rows, w), F32),
        compiler_params=_params(("parallel",), 24),
        name="pool_mix",
    )(halo_src, x_src, w_bf16, scale.reshape(1, w))


def _proj_pool(x, w_bf16, pool_w_bf16, scale):
    n, d = x.shape
    n_out = w_bf16.shape[1]
    w = len(POOL_WINDOWS) * C_GD
    tile = PROMPT_TILE
    halo_step = tile // POOL_HALO
    vmem = (2 * (tile * d * 4 + POOL_HALO * d * 4 + tile * n_out * 4 + tile * w * 2)
            + d * n_out * 2 + 6 * tile * w * 4)
    return pl.pallas_call(
        functools.partial(_proj_pool_kernel, tiles_per_seq=SEQ // tile),
        grid=(n // tile,),
        in_specs=[pl.BlockSpec((POOL_HALO, d), lambda i: (jnp.maximum(i * halo_step - 1, 0), 0)),
                  pl.BlockSpec((tile, d), lambda i: (i, 0)),
                  _resident((d, n_out)),
                  _resident((len(POOL_WINDOWS), C_GD, C_GD)), _resident((1, w))],
        out_specs=[pl.BlockSpec((tile, n_out), lambda i: (i, 0)),
                   pl.BlockSpec((tile, w), lambda i: (i, 0))],
        out_shape=[jax.ShapeDtypeStruct((n, n_out), F32), jax.ShapeDtypeStruct((n, w), BF16)],
        compiler_params=_params(("parallel",), vmem / MIB + 8),
        name="in_proj_pool",
    )(x, x, w_bf16, pool_w_bf16, scale.reshape(1, w))


def _hgrn_kernel(q_ref, f_ref, i_ref, g_ref, s0_ref, lb_ref, ng_ref, o_ref, sfin_ref, s_scr,
                 *, rows, in_rows, valid, seqs):
    t = pl.program_id(1)
    c_rows = HGRN_CHUNK

    @pl.when(t == 0)
    def _():
        s_scr[...] = s0_ref[...]

    r_io = lax.broadcasted_iota(jnp.int32, (c_rows, c_rows), 0)
    c_io = lax.broadcasted_iota(jnp.int32, (c_rows, c_rows), 1)
    causal = r_io >= c_io
    ltri = jnp.where(causal, 1.0, 0.0).astype(BF16)
    lgrp = jnp.where(c_io < (r_io // HGRN_SUB) * HGRN_SUB + HGRN_SUB // 2, 1.0, 0.0).astype(BF16)
    lsum = jnp.concatenate([ltri, lgrp], axis=0)
    eye = r_io == c_io
    row_id = lax.broadcasted_iota(jnp.int32, (c_rows, D_HK), 0)

    heads = [slice(hd * D_HK, (hd + 1) * D_HK) for hd in range(D_HEADS)]

    def load(ref, item, sl):
        r0, seq = item
        if in_rows == rows:
            return ref[pl.ds(r0, c_rows), sl]
        return jnp.concatenate([ref[seq * in_rows:(seq + 1) * in_rows, sl],
                                jnp.zeros((c_rows - in_rows, D_HK), F32)], axis=0)

    def decay_sums(item):
        gates = []
        for sl in heads:
            lb = lb_ref[:, sl]
            half_span = 0.5 * (1.0 - lb)
            f = (lb + half_span) + half_span * jnp.tanh(0.5 * load(f_ref, item, sl))
            logf = jnp.log2(f)
            kk = 1.0 - f
            if valid < rows:
                live = (t * rows + item[0] + row_id) < valid
                logf = jnp.where(live, logf, 0.0)
                kk = jnp.where(live, kk, 0.0)
            sums = _dot_exact_lhs(lsum, logf)
            gates.append((kk, sums[:c_rows, :], sums[c_rows:, :]))
        return gates

    def chunk_scores(item, gates):
        attns, queries = [], []
        for sl, (kk, cg, ref) in zip(heads, gates):
            q = _silu(load(q_ref, item, sl))
            qd = (q * jnp.exp2(cg - ref)).astype(BF16)
            blocks = []
            for i in range(c_rows // HGRN_SUB):
                ref_i = ref[i * HGRN_SUB:i * HGRN_SUB + 1, :]
                e = jnp.where(row_id < (i + 1) * HGRN_SUB, ref_i - cg, 0.0)
                k_i = (kk * jnp.exp2(e)).astype(BF16)
                blocks.append(_dot_nt(qd[i * HGRN_SUB:(i + 1) * HGRN_SUB, :], k_i))
            attns.append(jnp.where(causal, jnp.concatenate(blocks, axis=0), 0.0).astype(BF16))
            queries.append((q * jnp.exp2(cg)).astype(BF16))
        return attns, queries

    def advance_state(item, gates, attns, queries):
        seq = item[1]
        outs = []
        for hd, (sl, (kk, cg, _)) in enumerate(zip(heads, gates)):
            vb = load(i_ref, item, sl).astype(BF16)
            state = s_scr[seq, hd]
            outs.append(_dot(queries[hd], state.astype(BF16)) + _dot(attns[hd], vb))
            g_last = cg[c_rows - 1:c_rows, :]
            kd = kk * jnp.exp2(g_last - cg)
            decay_col = jnp.sum(
                jnp.where(eye, jnp.broadcast_to(jnp.exp2(g_last), (c_rows, D_HK)), 0.0),
                axis=1, keepdims=True)
            s_scr[seq, hd] = state * decay_col + _dot(kd.T.astype(BF16), vb)
        return outs

    def finish(item, outs):
        r0, seq = item
        for sl, o in zip(heads, outs):
            o = o * lax.rsqrt(jnp.mean(o * o, axis=-1, keepdims=True) + RMS_EPS) * ng_ref[...]
            o = (o * _silu(load(g_ref, item, sl))).astype(o_ref.dtype)
            if in_rows == rows:
                o_ref[pl.ds(r0, c_rows), sl] = o
            else:
                o_ref[seq * in_rows:(seq + 1) * in_rows, sl] = o[:in_rows, :]

    def run_items(items):
        gates = [decay_sums(item) for item in items]
        outs = []
        for item, gt in zip(items, gates):
            attns, queries = chunk_scores(item, gt)
            outs.append(advance_state(item, gt, attns, queries))
        for item, o in zip(items, outs):
            finish(item, o)

    if in_rows == rows:
        group = min(HGRN_GROUP, rows // c_rows)

        def chunk_group(c, carry):
            run_items([(pl.multiple_of((c * group + k) * c_rows, c_rows), 0) for k in range(group)])
            return carry

        lax.fori_loop(0, rows // (c_rows * group), chunk_group, 0)
    else:
        run_items([(0, seq) for seq in range(seqs)])

    @pl.when(t == pl.num_programs(1) - 1)
    def _():
        sfin_ref[...] = s_scr[...]


def _hgrn(src, s0, lb, norm_g, *, bsz, length, rows, in_rows, valid, seqs, out_dtype):
    w = D_HEADS * D_HK
    tiles = length // rows
    assert seqs == 1 or tiles == 1
    blk_rows = seqs * in_rows
    col = lambda k: pl.BlockSpec((blk_rows, w), lambda b, t: (b * tiles + t, 1 + k))
    state_spec = pl.BlockSpec((seqs, D_HEADS, D_HK, D_HK), lambda b, t: (b, 0, 0, 0))
    return pl.pallas_call(
        functools.partial(_hgrn_kernel, rows=rows, in_rows=in_rows, valid=valid, seqs=seqs),
        grid=(bsz // seqs, tiles),
        in_specs=[col(0), col(1), col(2), col(3), state_spec, _resident((1, w)),
                  _resident((1, D_HK))],
        out_specs=[pl.BlockSpec((blk_rows, w), lambda b, t: (b * tiles + t, 0)), state_spec],
        out_shape=[jax.ShapeDtypeStruct((bsz * tiles * in_rows, w), out_dtype),
                   jax.ShapeDtypeStruct((bsz, D_HEADS, D_HK, D_HK), F32)],
        scratch_shapes=[pltpu.VMEM((seqs, D_HEADS, D_HK, D_HK), F32)],
        compiler_params=_params(("parallel", "arbitrary"), 2 * 5 * blk_rows * w * 4 / MIB + 24),
        name="hgrn2",
    )(src, src, src, src, s0, lb.reshape(1, w), norm_g.reshape(1, D_HK))


def kernel(x_prompt, x_sample, cache_k, cache_v, state_pool, state_hgrn, page_table, w_in_even, w_out_even, gmlp_ws, gmlp_bs, gmlp_ln_g, gmlp_ln_b, w_in_odd, w_out_odd, pool_w, pool_scale, hgrn_lb_param, hgrn_norm_g, ln_mix_g, ln_mix_b, ln_ffn_g, ln_ffn_b, ffn_w1, ffn_w2):
    w = HALF_W
    xp = x_prompt.reshape(N_PROMPT, D_MODEL)
    xs = x_sample.reshape(N_SAMPLE, D_MODEL)
    pad_tokens = ((0, 0), (0, SUBLANES - DEC_SEQ), (0, 0))

    col_scale = jnp.where(jnp.arange(EVEN_IN) < w, SCORE_SCALE, 1.0)
    w_even = (w_in_even[0] * col_scale).astype(BF16)
    kb, ugv, qt, vt, kt32, vt32 = _proj_even_prompt(x_prompt, w_even, w_even[:, :3 * w].T)
    h0s = _proj(xs, w_even, SAMPLE_TILE)
    hs = h0s.reshape(DEC_BATCH, DEC_SEQ, EVEN_IN)
    oa_prompt, oa_sample = _moba(page_table, qt, kb, vt, hs[..., :w],
                                 jnp.pad(hs[..., w:2 * w], pad_tokens), jnp.pad(hs[..., 2 * w:3 * w], pad_tokens),
                                 cache_k[0].transpose(0, 2, 3, 1), cache_v[0].transpose(0, 2, 3, 1))

    tril = jnp.tril(jnp.ones((B_CHUNK, B_CHUNK), bool))
    ws_prompt = jnp.where(tril[None], gmlp_ws[0], 0.0)
    owner = jnp.arange(B_CHUNK) // DEC_SEQ
    ws_sample = jnp.where(owner[:, None] == owner[None, :],
                          jnp.tile(ws_prompt[:, :DEC_SEQ, :DEC_SEQ], (1, DEC_BATCH, DEC_BATCH)), 0.0)
    bias_prompt = jnp.repeat(gmlp_bs[0].T, B_GD, axis=1)
    bias_sample = jnp.tile(bias_prompt[:DEC_SEQ], (DEC_BATCH, 1))
    grp = jnp.arange(w) // B_GD
    avg = jnp.where(grp[:, None] == grp[None, :], 1.0 / B_GD, 0.0).astype(BF16)
    ob_prompt, = _gmlp(ugv, 0, 1, ws_prompt.astype(BF16), bias_prompt, avg, gmlp_ln_g[0], gmlp_ln_b[0],
                       out_dtype=BF16, emit_vn=False)
    ob_sample, vn_sample = _gmlp(h0s, 3, 4, ws_sample.astype(BF16), bias_sample, avg, gmlp_ln_g[0],
                                 gmlp_ln_b[0], out_dtype=F32, emit_vn=True)

    ffn_w1b, ffn_w2b = ffn_w1.astype(BF16), ffn_w2.astype(BF16)
    layer0 = (w_out_even[0].astype(BF16), ln_mix_g[0], ln_mix_b[0],
              ffn_w1b, ffn_w2b, 0, ln_ffn_g[0], ln_ffn_b[0])
    xp = _mix_ffn(oa_prompt, ob_prompt, xp, *layer0, FFN_TILE)
    xs = _mix_ffn(oa_sample.reshape(N_SAMPLE, w), ob_sample, xs, *layer0, SAMPLE_TILE)

    per_head = lambda t: t.reshape(BATCH, A_HEADS, A_HD, SEQ).transpose(0, 3, 1, 2)[None]
    new_k_prompt = per_head(kt32)
    new_v_prompt = per_head(vt32)
    new_k_sample = hs[..., w:2 * w].reshape(1, DEC_BATCH, DEC_SEQ, A_HEADS, A_HD)
    new_v_sample = hs[..., 2 * w:3 * w].reshape(1, DEC_BATCH, DEC_SEQ, A_HEADS, A_HD)
    new_gmlp_v_sample = vn_sample.reshape(1, DEC_BATCH, DEC_SEQ, w)

    w_odd = w_in_odd[0].astype(BF16)
    lb = jax.nn.softmax(hgrn_lb_param.astype(F32), axis=0)[0]
    pool_wb = pool_w[0].astype(BF16)

    h1p, oc_prompt = _proj_pool(xp, w_odd, pool_wb, pool_scale[0])
    od_prompt, s_prompt = _hgrn(h1p, jnp.zeros((BATCH, D_HEADS, D_HK, D_HK), F32), lb, hgrn_norm_g[0],
                                bsz=BATCH, length=SEQ, rows=HGRN_ROWS, in_rows=HGRN_ROWS, valid=SEQ,
                                seqs=1, out_dtype=BF16)

    h1s = _proj(xs, w_odd, SAMPLE_TILE).reshape(DEC_BATCH, DEC_SEQ, ODD_IN)
    h1s_pad = jnp.pad(h1s, pad_tokens).reshape(DEC_BATCH * SUBLANES, ODD_IN)
    halo_s = jnp.pad(state_pool[0], ((0, 0), (POOL_HALO - POOL_BUF, 0), (0, 0)))
    oc_sample = _pool(halo_s.reshape(DEC_BATCH * POOL_HALO, w), h1s_pad, pool_wb, pool_scale[0],
                      rows=SUBLANES, pos0=PAST_LEN)
    od_sample, s_sample = _hgrn(h1s_pad, state_hgrn[0], lb, hgrn_norm_g[0], bsz=DEC_BATCH,
                                length=HGRN_CHUNK, rows=HGRN_CHUNK, in_rows=SUBLANES, valid=DEC_SEQ,
                                seqs=HGRN_SAMPLE_SEQS, out_dtype=F32)
    real_rows = lambda t: t.reshape(DEC_BATCH, SUBLANES, w)[:, :DEC_SEQ].reshape(N_SAMPLE, w)

    layer1 = (w_out_odd[0].astype(BF16), ln_mix_g[1], ln_mix_b[1],
              ffn_w1b, ffn_w2b, 1, ln_ffn_g[1], ln_ffn_b[1])
    xp = _mix_ffn(oc_prompt, od_prompt, xp, *layer1, FFN_TILE)
    xs = _mix_ffn(real_rows(oc_sample), real_rows(od_sample), xs, *layer1, SAMPLE_TILE)

    new_pool_prompt = h1p.reshape(BATCH, SEQ, ODD_IN)[:, SEQ - POOL_BUF:, :w][None]
    new_pool_sample = jnp.concatenate([state_pool[0][:, DEC_SEQ:], h1s[..., :w]], axis=1)[None]
    return (xp.reshape(BATCH, SEQ, D_MODEL), xs.reshape(DEC_BATCH, DEC_SEQ, D_MODEL),
            new_k_prompt, new_v_prompt, new_k_sample, new_v_sample, new_gmlp_v_sample,
            new_pool_prompt, new_pool_sample, s_prompt[None], s_sample[None])
```

```python
import functools
import math

import jax
import jax.numpy as jnp
from jax import lax
from jax.experimental import pallas as pl
from jax.experimental.pallas import tpu as pltpu

F32 = jnp.float32
BF16 = jnp.bfloat16

D_MODEL = 1024
BATCH = 4
SEQ = 4096
DEPTH = 2
DEC_BATCH = 32
DEC_SEQ = 4
PAST_LEN = 8192
PAGE_SIZE = 128
HALF_W = D_MODEL // 2
A_HD = 64
A_HEADS = HALF_W // A_HD
MOBA_BLOCK = 256
MOBA_TOPK = 3
B_GROUPS = 8
B_GD = HALF_W // B_GROUPS
B_CHUNK = 128
POOL_WINDOWS = (2, 4, 8, 16)
C_GD = HALF_W // len(POOL_WINDOWS)
POOL_BUF = max(POOL_WINDOWS) - 1
D_HK = 128
D_HEADS = HALF_W // D_HK
D_FF = 4 * D_MODEL
EVEN_IN = 5 * HALF_W
ODD_IN = 5 * HALF_W
ALPHA = (2 * DEPTH) ** 0.25
LN_EPS = 1e-5
RMS_EPS = 1e-6
NEG = -1e30

N_PROMPT = BATCH * SEQ
N_SAMPLE = DEC_BATCH * DEC_SEQ
N_PAGES = PAST_LEN // PAGE_SIZE
N_PAST_BLOCKS = PAST_LEN // MOBA_BLOCK
PAGES_PER_BLOCK = MOBA_BLOCK // PAGE_SIZE
N_KEY_BLOCKS = SEQ // MOBA_BLOCK

LANES = 128
SUBLANES = 8
PROMPT_TILE = 1024
SAMPLE_TILE = N_SAMPLE
FF_CHUNK = 1024
FFN_TILE = 1024
FFN_GROUP_ROWS = 256
MOBA_LOOKAHEAD = 7
SAMPLE_PAGES_PER_TILE = DEC_BATCH * N_PAGES // (BATCH * N_KEY_BLOCKS)
TILES_PER_SAMPLE = N_PAGES // SAMPLE_PAGES_PER_TILE
assert SAMPLE_PAGES_PER_TILE * BATCH * N_KEY_BLOCKS == DEC_BATCH * N_PAGES
assert TILES_PER_SAMPLE * SAMPLE_PAGES_PER_TILE == N_PAGES and N_KEY_BLOCKS % TILES_PER_SAMPLE == 0
assert SAMPLE_PAGES_PER_TILE % PAGES_PER_BLOCK == 0
V_TAIL = 16
V_SLAB = A_HD + V_TAIL
V_ROWS = A_HEADS * V_SLAB
assert A_HEADS % (MOBA_LOOKAHEAD + 1) == 0
SCORE_SCALE = (A_HD ** -0.5) * math.log2(math.e)
HGRN_CHUNK = 128
HGRN_SUB = 32
HGRN_ROWS = 1024
HGRN_GROUP = 2
HGRN_SAMPLE_SEQS = 8
GMLP_GROUP = 8
POOL_HALO = 16
MIB = 1024 * 1024


def _dot(a, b):
    return jnp.dot(a, b, preferred_element_type=F32)


def _dot_nt(a, b):
    return lax.dot_general(a, b, (((1,), (1,)), ((), ())), preferred_element_type=F32)


def _split3(x):
    p0 = x.astype(BF16)
    r1 = x - p0.astype(F32)
    p1 = r1.astype(BF16)
    p2 = (r1 - p1.astype(F32)).astype(BF16)
    return p0, p1, p2


def _dot_exact_lhs(m_bf16, x):
    p0, p1, p2 = _split3(x)
    return _dot(m_bf16, p0) + _dot(m_bf16, p1) + _dot(m_bf16, p2)


def _dot_exact_rhs(x, m_bf16, pieces=3):
    parts = _split3(x)[:pieces]
    out = _dot(parts[0], m_bf16)
    for part in parts[1:]:
        out = out + _dot(part, m_bf16)
    return out


def _gelu_tanh(x):
    return 0.5 * x * (1.0 + jnp.tanh(0.7978845608028654 * (x + 0.044715 * (x * x * x))))


def _silu(x):
    h = 0.5 * x
    return h + h * jnp.tanh(h)


def _layer_norm(z, g, b):
    mu = jnp.mean(z, axis=-1, keepdims=True)
    zc = z - mu
    var = jnp.mean(zc * zc, axis=-1, keepdims=True)
    return zc * lax.rsqrt(var + LN_EPS) * g + b


def _params(sem, vmem_mib):
    return pltpu.CompilerParams(dimension_semantics=sem, vmem_limit_bytes=int(vmem_mib * MIB))


def _resident(shape):
    return pl.BlockSpec(shape, lambda *_: (0,) * len(shape), pipeline_mode=pl.Buffered(1))


def _proj_kernel(x_ref, w_ref, h_ref):
    xb = x_ref[...].astype(BF16)
    for c in range(w_ref.shape[1] // HALF_W):
        sl = slice(c * HALF_W, (c + 1) * HALF_W)
        h_ref[:, sl] = _dot(xb, w_ref[:, sl])


def _proj(x, w_bf16, tile):
    n, d = x.shape
    n_out = w_bf16.shape[1]
    vmem = 2 * (tile * d * 4 + tile * n_out * 4) + d * n_out * 2
    return pl.pallas_call(
        _proj_kernel,
        grid=(n // tile,),
        in_specs=[pl.BlockSpec((tile, d), lambda i: (i, 0)), _resident((d, n_out))],
        out_specs=pl.BlockSpec((tile, n_out), lambda i: (i, 0)),
        out_shape=jax.ShapeDtypeStruct((n, n_out), F32),
        compiler_params=_params(("parallel",), vmem / MIB + 8),
        name="in_proj",
    )(x, w_bf16)


def _proj_even_kernel(x_ref, wn_ref, wt_ref, kb_ref, ugv_ref, qt_ref, vt_ref, kt32_ref, vt32_ref):
    w = HALF_W
    xb = x_ref[...].astype(BF16)
    k = _dot(xb, wn_ref[:, w:2 * w])
    for half in range(PROMPT_TILE // MOBA_BLOCK):
        kb_ref[half] = k[half * MOBA_BLOCK:(half + 1) * MOBA_BLOCK, :].astype(BF16)
    ugv_ref[:, :w] = _dot(xb, wn_ref[:, 3 * w:4 * w])
    ugv_ref[:, w:] = _dot(xb, wn_ref[:, 4 * w:])
    qt = _dot_nt(wt_ref[:w, :], xb)
    kt32_ref[...] = k.T
    vt = _dot_nt(wt_ref[2 * w:, :], xb)
    vt32_ref[...] = vt
    ones_row = jnp.where(lax.broadcasted_iota(jnp.int32, (V_TAIL, MOBA_BLOCK), 0) == 0,
                         1.0, 0.0).astype(BF16)
    for half in range(PROMPT_TILE // MOBA_BLOCK):
        cols = slice(half * MOBA_BLOCK, (half + 1) * MOBA_BLOCK)
        qt_ref[half] = qt[:, cols].astype(BF16)
        for h in range(A_HEADS):
            vt_ref[half, h * V_SLAB:h * V_SLAB + A_HD, :] = vt[h * A_HD:(h + 1) * A_HD, cols].astype(BF16)
            vt_ref[half, h * V_SLAB + A_HD:(h + 1) * V_SLAB, :] = ones_row


def _proj_even_prompt(x_prompt, w_nat, w_tr):
    w = HALF_W
    tiles = SEQ // PROMPT_TILE
    per_tile = PROMPT_TILE // MOBA_BLOCK
    blocked = lambda shape: pl.BlockSpec((None, per_tile) + shape, lambda b, t: (b, t, 0, 0))
    vmem = (2 * (PROMPT_TILE * D_MODEL * 4 + PROMPT_TILE * 2 * w * 4 + 3 * PROMPT_TILE * w * 2
                 + 2 * PROMPT_TILE * w * 4) + 2 * D_MODEL * 3 * w * 2 + 4 * PROMPT_TILE * w * 4)
    return pl.pallas_call(
        _proj_even_kernel,
        grid=(BATCH, tiles),
        in_specs=[pl.BlockSpec((None, PROMPT_TILE, D_MODEL), lambda b, t: (b, t, 0)),
                  _resident((D_MODEL, EVEN_IN)), _resident((3 * w, D_MODEL))],
        out_specs=[blocked((MOBA_BLOCK, w)),
                   pl.BlockSpec((PROMPT_TILE, 2 * w), lambda b, t: (b * tiles + t, 0)),
                   blocked((w, MOBA_BLOCK)), blocked((V_ROWS, MOBA_BLOCK)),
                   pl.BlockSpec((None, w, PROMPT_TILE), lambda b, t: (b, 0, t)),
                   pl.BlockSpec((None, w, PROMPT_TILE), lambda b, t: (b, 0, t))],
        out_shape=[jax.ShapeDtypeStruct((BATCH, N_KEY_BLOCKS, MOBA_BLOCK, w), BF16),
                   jax.ShapeDtypeStruct((N_PROMPT, 2 * w), F32),
                   jax.ShapeDtypeStruct((BATCH, N_KEY_BLOCKS, w, MOBA_BLOCK), BF16),
                   jax.ShapeDtypeStruct((BATCH, N_KEY_BLOCKS, V_ROWS, MOBA_BLOCK), BF16),
                   jax.ShapeDtypeStruct((BATCH, w, SEQ), F32),
                   jax.ShapeDtypeStruct((BATCH, w, SEQ), F32)],
        compiler_params=_params(("parallel", "parallel"), vmem / MIB + 8),
        name="in_proj_even",
    )(x_prompt, w_nat, w_tr)


def _mix_ffn_kernel(oa_ref, ob_ref, x_ref, wo_ref, g1_ref, b1_ref, w1_ref, w2_ref, g2_ref, b2_ref,
                    y_ref):
    tile = x_ref.shape[0]
    group_rows = min(FFN_GROUP_ROWS, tile // 2)
    groups = [slice(r, r + group_rows) for r in range(0, tile, group_rows)]
    mixed = [_dot(oa_ref[rows, :].astype(BF16), wo_ref[:HALF_W, :])
             + _dot(ob_ref[rows, :].astype(BF16), wo_ref[HALF_W:, :]) for rows in groups]
    for rows, mix in zip(groups, mixed):
        x = _layer_norm(ALPHA * x_ref[rows, :] + mix, g1_ref[...], b1_ref[...])
        xb = x.astype(BF16)
        acc = jnp.zeros(x.shape, F32)
        for c in range(D_FF // FF_CHUNK):
            sl = slice(c * FF_CHUNK, (c + 1) * FF_CHUNK)
            hid = jnp.maximum(_dot(xb, w1_ref[:, sl]), 0.0)
            acc = acc + _dot((hid * hid).astype(BF16), w2_ref[sl, :])
        y_ref[rows, :] = _layer_norm(ALPHA * x + acc, g2_ref[...], b2_ref[...])


def _mix_ffn(oa, ob, x, wo_bf16, g1, b1, w1_layers, w2_layers, layer, g2, b2, tile):
    n, d = x.shape
    row = lambda i: (i, 0)
    vec = lambda v: v.reshape(1, d)
    of_layer = lambda shape: pl.BlockSpec((None,) + shape, lambda i: (layer, 0, 0),
                                          pipeline_mode=pl.Buffered(1))
    vmem = (2 * (2 * tile * HALF_W * oa.dtype.itemsize + 2 * tile * d * 4)
            + (d * d + 2 * d * D_FF) * 2
            + 8 * min(tile, FFN_GROUP_ROWS) * FF_CHUNK * 4)
    return pl.pallas_call(
        _mix_ffn_kernel,
        grid=(n // tile,),
        in_specs=[pl.BlockSpec((tile, HALF_W), row), pl.BlockSpec((tile, HALF_W), row),
                  pl.BlockSpec((tile, d), row), _resident((d, d)),
                  _resident((1, d)), _resident((1, d)),
                  of_layer((d, D_FF)), of_layer((D_FF, d)),
                  _resident((1, d)), _resident((1, d))],
        out_specs=pl.BlockSpec((tile, d), row),
        out_shape=jax.ShapeDtypeStruct((n, d), F32),
        compiler_params=_params(("parallel",), vmem / MIB + 8),
        name="mix_ffn_ln",
    )(oa, ob, x, wo_bf16, vec(g1), vec(b1), w1_layers, w2_layers, vec(g2), vec(b2))


def _moba_kernel(pt_ref, qt_ref, k_ref, vt_ref, qs_ref, kn_ref, vn_ref, *refs):
    del pt_ref
    kp_refs = refs[:SAMPLE_PAGES_PER_TILE]
    vp_refs = refs[SAMPLE_PAGES_PER_TILE:2 * SAMPLE_PAGES_PER_TILE]
    (o_ref, os_ref, kmt_scr, qtz_scr, selb_scr, m_scr, acc_scr, s_scr, ref_scr, a_scr,
     sm_scr, sl_scr, sgate_scr, sacc_scr) = refs[2 * SAMPLE_PAGES_PER_TILE:]
    j = pl.program_id(1)
    nb = k_ref.shape[0]
    blk = MOBA_BLOCK
    pair_w = 2 * A_HD
    w = A_HEADS * A_HD
    nrow = DEC_SEQ * A_HEADS
    part = (pl.program_id(0) * nb + j) % TILES_PER_SAMPLE
    wide = lambda c: jnp.broadcast_to(c, (nrow, LANES))

    @pl.when(j == 0)
    def _():
        rows = [jnp.sum(k_ref[n].astype(F32), axis=0, keepdims=True) * (1.0 / blk)
                for n in range(nb)]
        kmean = jnp.concatenate(rows, axis=0)
        tiled = jnp.concatenate([kmean] * A_HEADS, axis=0)
        rh = lax.broadcasted_iota(jnp.int32, tiled.shape, 0) // nb
        ch = lax.broadcasted_iota(jnp.int32, tiled.shape, 1) // A_HD
        kmt_scr[...] = jnp.where(rh == ch, tiled, 0.0)

    head_mask = (lax.broadcasted_iota(jnp.int32, (A_HEADS, w), 1) // A_HD
                 == lax.broadcasted_iota(jnp.int32, (A_HEADS, w), 0))
    qs = qs_ref[...]
    qbd = jnp.concatenate(
        [jnp.where(head_mask, jnp.broadcast_to(qs[i:i + 1, :], (A_HEADS, w)), 0.0)
         for i in range(DEC_SEQ)], axis=0).astype(BF16)
    page_scores = [_dot(qbd, kp_refs[i][...].reshape(w, PAGE_SIZE).astype(BF16))
                   for i in range(SAMPLE_PAGES_PER_TILE)]

    def sample_blocks():
        probs = []
        for bl in range(SAMPLE_PAGES_PER_TILE // PAGES_PER_BLOCK):
            s = jnp.concatenate(page_scores[bl * PAGES_PER_BLOCK:(bl + 1) * PAGES_PER_BLOCK], axis=1)
            blk_id = part * (SAMPLE_PAGES_PER_TILE // PAGES_PER_BLOCK) + bl
            m = jnp.max(s, axis=-1, keepdims=True)
            p = jnp.exp2(s - m)
            sm_scr[blk_id] = wide(m)
            sl_scr[blk_id] = wide(jnp.sum(p, axis=-1, keepdims=True))
            sgate_scr[blk_id] = wide(jnp.sum(s, axis=-1, keepdims=True) * (1.0 / MOBA_BLOCK))
            probs.append(p.astype(BF16))
        for bl, p in enumerate(probs):
            blk_id = part * (SAMPLE_PAGES_PER_TILE // PAGES_PER_BLOCK) + bl
            acc = None
            for r in range(PAGES_PER_BLOCK):
                vpage = vp_refs[bl * PAGES_PER_BLOCK + r][...].reshape(w, PAGE_SIZE).astype(BF16)
                pv = _dot_nt(p[:, r * PAGE_SIZE:(r + 1) * PAGE_SIZE], vpage)
                acc = pv if acc is None else acc + pv
            sacc_scr[blk_id] = acc

    qts = qt_ref[...]
    k0, k1, k2 = _split3(kmt_scr[...])
    gate = _dot(k0, qts) + _dot(k1, qts) + _dot(k2, qts)

    n_io = lax.broadcasted_iota(jnp.int32, (nb, blk), 0)
    half = lax.broadcasted_iota(jnp.int32, (pair_w, blk), 0) // A_HD
    for h in range(A_HEADS):
        g = jnp.where(n_io < j, gate[h * nb:(h + 1) * nb, :], NEG)
        bias = jnp.full((nb, blk), NEG, F32)
        for _ in range(MOBA_TOPK):
            mx = jnp.max(g, axis=0, keepdims=True)
            cand = jnp.where((g == mx) & (mx > 0.5 * NEG), n_io, nb)
            pick = n_io == jnp.min(cand, axis=0, keepdims=True)
            bias = jnp.where(pick, 0.0, bias)
            g = jnp.where(pick, NEG, g)
        selb_scr[h * nb:(h + 1) * nb, :] = bias
        pr = h // 2
        qpair = qts[pr * pair_w:(pr + 1) * pair_w, :]
        qtz_scr[h] = jnp.where(half == (h % 2), qpair, jnp.zeros_like(qpair))

    causal = (lax.broadcasted_iota(jnp.int32, (blk, blk), 0)
              <= lax.broadcasted_iota(jnp.int32, (blk, blk), 1))

    def fold_rows(x, op):
        return op(x.reshape(blk // SUBLANES, SUBLANES, blk), axis=0)

    def stage_scores(n, h, slot, own):
        pr = h // 2
        s = _dot(k_ref[n, :, pr * pair_w:(pr + 1) * pair_w], qtz_scr[h])
        if own:
            s = jnp.where(causal, s, NEG)
        s_scr[slot] = s
        col_max = jnp.max(fold_rows(s, jnp.max), axis=0, keepdims=True)
        if own:
            m_scr[h] = col_max
            ref_scr[slot] = col_max
        else:
            bias = selb_scr[pl.ds(h * nb + n, 1), :]
            m_old = m_scr[h]
            m_new = jnp.maximum(m_old, col_max + bias)
            m_scr[h] = m_new
            a_scr[slot] = jnp.exp2(m_old - m_new)
            ref_scr[slot] = m_new - bias

    def accumulate(n, h, slot, own):
        vs = slice(h * V_SLAB, (h + 1) * V_SLAB)
        p = jnp.exp2(s_scr[slot] - ref_scr[slot])
        pv = _dot(vt_ref[n, vs, :], p.astype(BF16))
        if own:
            acc_scr[vs, :] = pv
        else:
            acc_scr[vs, :] = a_scr[slot] * acc_scr[vs, :] + pv

    n_slots = MOBA_LOOKAHEAD + 1

    def stage_ahead(n, n_next, h, own):
        ha = h + MOBA_LOOKAHEAD
        if ha < A_HEADS:
            stage_scores(n, ha, ha % n_slots, own=own)
        else:
            stage_scores(n_next, ha - A_HEADS, ha % n_slots, own=False)

    for h in range(MOBA_LOOKAHEAD):
        stage_scores(j, h, h % n_slots, own=True)
    sample_blocks()
    for h in range(A_HEADS):
        stage_ahead(j, 0, h, own=True)
        accumulate(j, h, h % n_slots, own=True)

    def past_block(n, carry):
        for h in range(A_HEADS):
            stage_ahead(n, n + 1, h, own=False)
            accumulate(n, h, h % n_slots, own=False)
        return carry

    lax.fori_loop(0, j, past_block, 0)

    heads_out = []
    for h in range(A_HEADS):
        slab = acc_scr[h * V_SLAB:(h + 1) * V_SLAB, :]
        heads_out.append(slab[:A_HD, :] / slab[A_HD:A_HD + 1, :])
    o_ref[...] = jnp.concatenate(heads_out, axis=0).T.astype(o_ref.dtype)

    @pl.when(part == TILES_PER_SAMPLE - 1)
    def _():
        tile4 = lambda c: jnp.concatenate([c] * (w // LANES), axis=1)
        s = _dot_nt(qbd, kn_ref[...].astype(BF16))
        col = lax.broadcasted_iota(jnp.int32, s.shape, 1)
        qi = lax.broadcasted_iota(jnp.int32, s.shape, 0) // A_HEADS
        s = jnp.where(col <= qi, s, NEG)
        m_col = jnp.max(s, axis=-1, keepdims=True)
        p = jnp.exp2(s - m_col)
        m_own = wide(m_col)
        l_own = wide(jnp.sum(p, axis=-1, keepdims=True))
        o_own = _dot(p, vn_ref[...])

        for _ in range(MOBA_TOPK):
            best = jnp.full((nrow, LANES), NEG, F32)
            bidx = jnp.zeros((nrow, LANES), jnp.int32)
            for n in range(N_PAST_BLOCKS):
                gn = sgate_scr[n]
                upd = gn > best
                best = jnp.where(upd, gn, best)
                bidx = jnp.where(upd, n, bidx)
            for n in range(N_PAST_BLOCKS):
                sgate_scr[n] = jnp.where(bidx == n, -jnp.inf, sgate_scr[n])

        m_all = m_own
        for n in range(N_PAST_BLOCKS):
            m_all = jnp.maximum(m_all, jnp.where(sgate_scr[n] == -jnp.inf, sm_scr[n], NEG))
        w_own = jnp.exp2(m_own - m_all)
        l_all = w_own * l_own
        o_all = tile4(w_own) * o_own
        for n in range(N_PAST_BLOCKS):
            wn = jnp.where(sgate_scr[n] == -jnp.inf, jnp.exp2(sm_scr[n] - m_all), 0.0)
            l_all = l_all + wn * sl_scr[n]
            o_all = o_all + tile4(wn) * sacc_scr[n]
        out = o_all / tile4(l_all)
        for i in range(DEC_SEQ):
            rows = out[i * A_HEADS:(i + 1) * A_HEADS, :]
            os_ref[i:i + 1, :] = jnp.sum(jnp.where(head_mask, rows, 0.0), axis=0, keepdims=True)


def _moba(page_table, qt, kb, vt, q_s, k_new, v_new, cache_kt, cache_vt):
    bsz, nb = qt.shape[:2]
    w = A_HEADS * A_HD
    nrow = DEC_SEQ * A_HEADS
    n_slots = MOBA_LOOKAHEAD + 1
    step = lambda b, j: b * nb + j
    sample_of = lambda b, j: step(b, j) // TILES_PER_SAMPLE
    first_page = lambda b, j: (step(b, j) % TILES_PER_SAMPLE) * SAMPLE_PAGES_PER_TILE
    page_spec = lambda i: pl.BlockSpec(
        (None, A_HEADS, A_HD, PAGE_SIZE),
        lambda b, j, pt: (pt[sample_of(b, j), first_page(b, j) + i], 0, 0, 0))
    per_sample = lambda rows: pl.BlockSpec((None, rows, w), lambda b, j, pt: (sample_of(b, j), 0, 0))
    whole_row = lambda shape: pl.BlockSpec((None,) + shape, lambda b, j, pt: (b, 0, 0, 0),
                                           pipeline_mode=pl.Buffered(1))
    grid_spec = pltpu.PrefetchScalarGridSpec(
        num_scalar_prefetch=1,
        grid=(bsz, nb),
        in_specs=([pl.BlockSpec((None, None, w, MOBA_BLOCK), lambda b, j, pt: (b, j, 0, 0)),
                   whole_row((nb, MOBA_BLOCK, w)), whole_row((nb, V_ROWS, MOBA_BLOCK)),
                   per_sample(DEC_SEQ), per_sample(SUBLANES), per_sample(SUBLANES)]
                  + [page_spec(i) for i in range(SAMPLE_PAGES_PER_TILE)]
                  + [page_spec(i) for i in range(SAMPLE_PAGES_PER_TILE)]),
        out_specs=[pl.BlockSpec((MOBA_BLOCK, w), lambda b, j, pt: (step(b, j), 0)),
                   per_sample(DEC_SEQ)],
        scratch_shapes=[pltpu.VMEM((A_HEADS * nb, w), F32),
                        pltpu.VMEM((A_HEADS, 2 * A_HD, MOBA_BLOCK), BF16),
                        pltpu.VMEM((A_HEADS * nb, MOBA_BLOCK), F32),
                        pltpu.VMEM((A_HEADS, 1, MOBA_BLOCK), F32),
                        pltpu.VMEM((V_ROWS, MOBA_BLOCK), F32),
                        pltpu.VMEM((n_slots, MOBA_BLOCK, MOBA_BLOCK), F32),
                        pltpu.VMEM((n_slots, 1, MOBA_BLOCK), F32),
                        pltpu.VMEM((n_slots, 1, MOBA_BLOCK), F32),
                        pltpu.VMEM((N_PAST_BLOCKS, nrow, LANES), F32),
                        pltpu.VMEM((N_PAST_BLOCKS, nrow, LANES), F32),
                        pltpu.VMEM((N_PAST_BLOCKS, nrow, LANES), F32),
                        pltpu.VMEM((N_PAST_BLOCKS, nrow, w), F32)],
    )
    vmem = (nb * MOBA_BLOCK * (w + V_ROWS) * 2 + 2 * (w * MOBA_BLOCK * 2 + MOBA_BLOCK * w * 2)
            + n_slots * MOBA_BLOCK * MOBA_BLOCK * 4
            + 2 * 2 * SAMPLE_PAGES_PER_TILE * PAGE_SIZE * w * 4
            + N_PAST_BLOCKS * nrow * (3 * LANES + w) * 4 + 4 * MIB)
    return pl.pallas_call(
        _moba_kernel,
        grid_spec=grid_spec,
        out_shape=[jax.ShapeDtypeStruct((bsz * nb * MOBA_BLOCK, w), BF16),
                   jax.ShapeDtypeStruct((DEC_BATCH, DEC_SEQ, w), F32)],
        compiler_params=_params(("arbitrary", "arbitrary"), vmem / MIB + 6),
        name="moba",
    )(page_table, qt, kb, vt, q_s, k_new, v_new,
      *([cache_kt] * SAMPLE_PAGES_PER_TILE), *([cache_vt] * SAMPLE_PAGES_PER_TILE))


def _gmlp_kernel(u_ref, gv_ref, w_ref, bias_ref, avg_ref, lng_ref, lnb_ref, ob_ref, *vn_refs):
    chunks = [slice(c * B_CHUNK, (c + 1) * B_CHUNK) for c in range(u_ref.shape[0] // B_CHUNK)]
    avg = avg_ref[...]
    gvs = [_gelu_tanh(gv_ref[rows, :]) for rows in chunks]
    centred = [gv - _dot_exact_rhs(gv, avg) for gv in gvs]
    variances = [_dot_exact_rhs(gc * gc, avg, pieces=2) for gc in centred]
    pair_w = 2 * B_GD
    lane = lax.broadcasted_iota(jnp.int32, (B_CHUNK, pair_w), 1)
    for rows, gc, var in zip(chunks, centred, variances):
        vn = gc * lax.rsqrt(var + LN_EPS) * lng_ref[...] + lnb_ref[...]
        if vn_refs:
            vn_refs[0][rows, :] = vn
        vb = vn.astype(BF16)
        u = _gelu_tanh(u_ref[rows, :])
        for pr in range(B_GROUPS // 2):
            sl = slice(pr * pair_w, (pr + 1) * pair_w)
            vp = vb[:, sl]
            zero = jnp.zeros_like(vp)
            mixed = (_dot(w_ref[2 * pr], jnp.where(lane < B_GD, vp, zero))
                     + _dot(w_ref[2 * pr + 1], jnp.where(lane >= B_GD, vp, zero)))
            ob_ref[rows, sl] = (u[:, sl] * (mixed + bias_ref[:, sl])).astype(ob_ref.dtype)


def _gmlp(src, u_col, gv_col, w_masked, bias, avg, ln_g, ln_b, *, out_dtype, emit_vn):
    rows = src.shape[0]
    w = B_GROUPS * B_GD
    tile = B_CHUNK * min(GMLP_GROUP, rows // B_CHUNK)
    out_shape = [jax.ShapeDtypeStruct((rows, w), out_dtype)]
    out_specs = [pl.BlockSpec((tile, w), lambda c: (c, 0))]
    if emit_vn:
        out_shape.append(jax.ShapeDtypeStruct((rows, w), F32))
        out_specs.append(pl.BlockSpec((tile, w), lambda c: (c, 0)))
    return pl.pallas_call(
        _gmlp_kernel,
        grid=(rows // tile,),
        in_specs=[pl.BlockSpec((tile, w), lambda c: (c, u_col)),
                  pl.BlockSpec((tile, w), lambda c: (c, gv_col)),
                  _resident((B_GROUPS, B_CHUNK, B_CHUNK)), _resident((B_CHUNK, w)),
                  _resident((w, w)), _resident((1, w)), _resident((1, w))],
        out_specs=out_specs,
        out_shape=out_shape,
        compiler_params=_params(("parallel",), 2 * 4 * tile * w * 4 / MIB + 16),
        name="gmlp_gate",
    )(src, src, w_masked, bias, avg, ln_g.reshape(1, w), ln_b.reshape(1, w))


def _pool_tile(halo, x, pos_first, w_ref, sc_ref, y_ref):
    rows = x.shape[0]
    ext = jnp.concatenate([halo, x], axis=0)
    pos = pos_first + lax.broadcasted_iota(jnp.int32, (rows, C_GD), 0)
    for gi, win in enumerate(POOL_WINDOWS):
        sl = slice(gi * C_GD, (gi + 1) * C_GD)
        s = ext[:, sl]
        sh = 1
        while sh < win:
            s = s + pltpu.roll(s, sh, 0)
            sh *= 2
        cnt = jnp.minimum(win, pos + 1).astype(F32)
        pooled = s[POOL_HALO:, :] / cnt - x[:, sl]
        y_ref[:, sl] = (_dot(pooled.astype(BF16), w_ref[gi]) * sc_ref[:, sl]).astype(y_ref.dtype)


def _pool_kernel(halo_ref, x_ref, w_ref, sc_ref, y_ref, *, pos0):
    _pool_tile(halo_ref[...], x_ref[...], pos0, w_ref, sc_ref, y_ref)


def _proj_pool_kernel(xh_ref, x_ref, w_ref, pw_ref, sc_ref, h_ref, y_ref, *, tiles_per_seq):
    t = pl.program_id(0) % tiles_per_seq
    rows = x_ref.shape[0]
    xb = x_ref[...].astype(BF16)
    xc = _dot(xb, w_ref[:, :HALF_W])
    h_ref[:, :HALF_W] = xc
    for c in range(1, w_ref.shape[1] // HALF_W):
        sl = slice(c * HALF_W, (c + 1) * HALF_W)
        h_ref[:, sl] = _dot(xb, w_ref[:, sl])
    halo = _dot(xh_ref[...].astype(BF16), w_ref[:, :HALF_W])
    _pool_tile(jnp.where(t == 0, 0.0, halo), xc, t * rows, pw_ref, sc_ref, y_ref)


def _pool(halo_src, x_src, w_bf16, scale, *, rows, pos0):
    w = len(POOL_WINDOWS) * C_GD
    n_rows = x_src.shape[0]
    return pl.pallas_call(
        functools.partial(_pool_kernel, pos0=pos0),
        grid=(n_rows // rows,),
        in_specs=[pl.BlockSpec((POOL_HALO, w), lambda i: (i, 0)),
                  pl.BlockSpec((rows, w), lambda i: (i, 0)),
                  _resident((len(POOL_WINDOWS), C_GD, C_GD)), _resident((1, w))],
        out_specs=pl.BlockSpec((rows, w), lambda i: (i, 0)),
        out_shape=jax.ShapeDtypeStruct((n_rows, w), F32),
        compiler_params=_params(("parallel",), 24),
        name="pool_mix",
    )(halo_src, x_src, w_bf16, scale.reshape(1, w))


def _proj_pool(x, w_bf16, pool_w_bf16, scale):
    n, d = x.shape
    n_out = w_bf16.shape[1]
    w = len(POOL_WINDOWS) * C_GD
    tile = PROMPT_TILE
    halo_step = tile // POOL_HALO
    vmem = (2 * (tile * d * 4 + POOL_HALO * d * 4 + tile * n_out * 4 + tile * w * 2)
            + d * n_out * 2 + 6 * tile * w * 4)
    return pl.pallas_call(
        functools.partial(_proj_pool_kernel, tiles_per_seq=SEQ // tile),
        grid=(n // tile,),
        in_specs=[pl.BlockSpec((POOL_HALO, d), lambda i: (jnp.maximum(i * halo_step - 1, 0), 0)),
                  pl.BlockSpec((tile, d), lambda i: (i, 0)),
                  _resident((d, n_out)),
                  _resident((len(POOL_WINDOWS), C_GD, C_GD)), _resident((1, w))],
        out_specs=[pl.BlockSpec((tile, n_out), lambda i: (i, 0)),
                   pl.BlockSpec((tile, w), lambda i: (i, 0))],
        out_shape=[jax.ShapeDtypeStruct((n, n_out), F32), jax.ShapeDtypeStruct((n, w), BF16)],
        compiler_params=_params(("parallel",), vmem / MIB + 8),
        name="in_proj_pool",
    )(x, x, w_bf16, pool_w_bf16, scale.reshape(1, w))


def _hgrn_kernel(q_ref, f_ref, i_ref, g_ref, s0_ref, lb_ref, ng_ref, o_ref, sfin_ref, s_scr,
                 *, rows, in_rows, valid, seqs):
    t = pl.program_id(1)
    c_rows = HGRN_CHUNK

    @pl.when(t == 0)
    def _():
        s_scr[...] = s0_ref[...]

    r_io = lax.broadcasted_iota(jnp.int32, (c_rows, c_rows), 0)
    c_io = lax.broadcasted_iota(jnp.int32, (c_rows, c_rows), 1)
    causal = r_io >= c_io
    ltri = jnp.where(causal, 1.0, 0.0).astype(BF16)
    lgrp = jnp.where(c_io < (r_io // HGRN_SUB) * HGRN_SUB + HGRN_SUB // 2, 1.0, 0.0).astype(BF16)
    lsum = jnp.concatenate([ltri, lgrp], axis=0)
    eye = r_io == c_io
    row_id = lax.broadcasted_iota(jnp.int32, (c_rows, D_HK), 0)

    heads = [slice(hd * D_HK, (hd + 1) * D_HK) for hd in range(D_HEADS)]

    def load(ref, item, sl):
        r0, seq = item
        if in_rows == rows:
            return ref[pl.ds(r0, c_rows), sl]
        return jnp.concatenate([ref[seq * in_rows:(seq + 1) * in_rows, sl],
                                jnp.zeros((c_rows - in_rows, D_HK), F32)], axis=0)

    def decay_sums(item):
        gates = []
        for sl in heads:
            lb = lb_ref[:, sl]
            half_span = 0.5 * (1.0 - lb)
            f = (lb + half_span) + half_span * jnp.tanh(0.5 * load(f_ref, item, sl))
            logf = jnp.log2(f)
            kk = 1.0 - f
            if valid < rows:
                live = (t * rows + item[0] + row_id) < valid
                logf = jnp.where(live, logf, 0.0)
                kk = jnp.where(live, kk, 0.0)
            sums = _dot_exact_lhs(lsum, logf)
            gates.append((kk, sums[:c_rows, :], sums[c_rows:, :]))
        return gates

    def chunk_scores(item, gates):
        attns, queries = [], []
        for sl, (kk, cg, ref) in zip(heads, gates):
            q = _silu(load(q_ref, item, sl))
            qd = (q * jnp.exp2(cg - ref)).astype(BF16)
            blocks = []
            for i in range(c_rows // HGRN_SUB):
                ref_i = ref[i * HGRN_SUB:i * HGRN_SUB + 1, :]
                e = jnp.where(row_id < (i + 1) * HGRN_SUB, ref_i - cg, 0.0)
                k_i = (kk * jnp.exp2(e)).astype(BF16)
                blocks.append(_dot_nt(qd[i * HGRN_SUB:(i + 1) * HGRN_SUB, :], k_i))
            attns.append(jnp.where(causal, jnp.concatenate(blocks, axis=0), 0.0).astype(BF16))
            queries.append((q * jnp.exp2(cg)).astype(BF16))
        return attns, queries

    def advance_state(item, gates, attns, queries):
        seq = item[1]
        outs = []
        for hd, (sl, (kk, cg, _)) in enumerate(zip(heads, gates)):
            vb = load(i_ref, item, sl).astype(BF16)
            state = s_scr[seq, hd]
            outs.append(_dot(queries[hd], state.astype(BF16)) + _dot(attns[hd], vb))
            g_last = cg[c_rows - 1:c_rows, :]
            kd = kk * jnp.exp2(g_last - cg)
            decay_col = jnp.sum(
                jnp.where(eye, jnp.broadcast_to(jnp.exp2(g_last), (c_rows, D_HK)), 0.0),
                axis=1, keepdims=True)
            s_scr[seq, hd] = state * decay_col + _dot(kd.T.astype(BF16), vb)
        return outs

    def finish(item, outs):
        r0, seq = item
        for sl, o in zip(heads, outs):
            o = o * lax.rsqrt(jnp.mean(o * o, axis=-1, keepdims=True) + RMS_EPS) * ng_ref[...]
            o = (o * _silu(load(g_ref, item, sl))).astype(o_ref.dtype)
            if in_rows == rows:
                o_ref[pl.ds(r0, c_rows), sl] = o
            else:
                o_ref[seq * in_rows:(seq + 1) * in_rows, sl] = o[:in_rows, :]

    def run_items(items):
        gates = [decay_sums(item) for item in items]
        outs = []
        for item, gt in zip(items, gates):
            attns, queries = chunk_scores(item, gt)
            outs.append(advance_state(item, gt, attns, queries))
        for item, o in zip(items, outs):
            finish(item, o)

    if in_rows == rows:
        group = min(HGRN_GROUP, rows // c_rows)

        def chunk_group(c, carry):
            run_items([(pl.multiple_of((c * group + k) * c_rows, c_rows), 0) for k in range(group)])
            return carry

        lax.fori_loop(0, rows // (c_rows * group), chunk_group, 0)
    else:
        run_items([(0, seq) for seq in range(seqs)])

    @pl.when(t == pl.num_programs(1) - 1)
    def _():
        sfin_ref[...] = s_scr[...]


def _hgrn(src, s0, lb, norm_g, *, bsz, length, rows, in_rows, valid, seqs, out_dtype):
    w = D_HEADS * D_HK
    tiles = length // rows
    assert seqs == 1 or tiles == 1
    blk_rows = seqs * in_rows
    col = lambda k: pl.BlockSpec((blk_rows, w), lambda b, t: (b * tiles + t, 1 + k))
    state_spec = pl.BlockSpec((seqs, D_HEADS, D_HK, D_HK), lambda b, t: (b, 0, 0, 0))
    return pl.pallas_call(
        functools.partial(_hgrn_kernel, rows=rows, in_rows=in_rows, valid=valid, seqs=seqs),
        grid=(bsz // seqs, tiles),
        in_specs=[col(0), col(1), col(2), col(3), state_spec, _resident((1, w)),
                  _resident((1, D_HK))],
        out_specs=[pl.BlockSpec((blk_rows, w), lambda b, t: (b * tiles + t, 0)), state_spec],
        out_shape=[jax.ShapeDtypeStruct((bsz * tiles * in_rows, w), out_dtype),
                   jax.ShapeDtypeStruct((bsz, D_HEADS, D_HK, D_HK), F32)],
        scratch_shapes=[pltpu.VMEM((seqs, D_HEADS, D_HK, D_HK), F32)],
        compiler_params=_params(("parallel", "arbitrary"), 2 * 5 * blk_rows * w * 4 / MIB + 24),
        name="hgrn2",
    )(src, src, src, src, s0, lb.reshape(1, w), norm_g.reshape(1, D_HK))


def kernel(x_prompt, x_sample, cache_k, cache_v, state_pool, state_hgrn, page_table, w_in_even, w_out_even, gmlp_ws, gmlp_bs, gmlp_ln_g, gmlp_ln_b, w_in_odd, w_out_odd, pool_w, pool_scale, hgrn_lb_param, hgrn_norm_g, ln_mix_g, ln_mix_b, ln_ffn_g, ln_ffn_b, ffn_w1, ffn_w2):
    w = HALF_W
    xp = x_prompt.reshape(N_PROMPT, D_MODEL)
    xs = x_sample.reshape(N_SAMPLE, D_MODEL)
    pad_tokens = ((0, 0), (0, SUBLANES - DEC_SEQ), (0, 0))

    col_scale = jnp.where(jnp.arange(EVEN_IN) < w, SCORE_SCALE, 1.0)
    w_even = (w_in_even[0] * col_scale).astype(BF16)
    kb, ugv, qt, vt, kt32, vt32 = _proj_even_prompt(x_prompt, w_even, w_even[:, :3 * w].T)
    h0s = _proj(xs, w_even, SAMPLE_TILE)
    hs = h0s.reshape(DEC_BATCH, DEC_SEQ, EVEN_IN)
    oa_prompt, oa_sample = _moba(page_table, qt, kb, vt, hs[..., :w],
                                 jnp.pad(hs[..., w:2 * w], pad_tokens), jnp.pad(hs[..., 2 * w:3 * w], pad_tokens),
                                 cache_k[0].transpose(0, 2, 3, 1), cache_v[0].transpose(0, 2, 3, 1))

    tril = jnp.tril(jnp.ones((B_CHUNK, B_CHUNK), bool))
    ws_prompt = jnp.where(tril[None], gmlp_ws[0], 0.0)
    owner = jnp.arange(B_CHUNK) // DEC_SEQ
    ws_sample = jnp.where(owner[:, None] == owner[None, :],
                          jnp.tile(ws_prompt[:, :DEC_SEQ, :DEC_SEQ], (1, DEC_BATCH, DEC_BATCH)), 0.0)
    bias_prompt = jnp.repeat(gmlp_bs[0].T, B_GD, axis=1)
    bias_sample = jnp.tile(bias_prompt[:DEC_SEQ], (DEC_BATCH, 1))
    grp = jnp.arange(w) // B_GD
    avg = jnp.where(grp[:, None] == grp[None, :], 1.0 / B_GD, 0.0).astype(BF16)
    ob_prompt, = _gmlp(ugv, 0, 1, ws_prompt.astype(BF16), bias_prompt, avg, gmlp_ln_g[0], gmlp_ln_b[0],
                       out_dtype=BF16, emit_vn=False)
    ob_sample, vn_sample = _gmlp(h0s, 3, 4, ws_sample.astype(BF16), bias_sample, avg, gmlp_ln_g[0],
                                 gmlp_ln_b[0], out_dtype=F32, emit_vn=True)

    ffn_w1b, ffn_w2b = ffn_w1.astype(BF16), ffn_w2.astype(BF16)
    layer0 = (w_out_even[0].astype(BF16), ln_mix_g[0], ln_mix_b[0],
              ffn_w1b, ffn_w2b, 0, ln_ffn_g[0], ln_ffn_b[0])
    xp = _mix_ffn(oa_prompt, ob_prompt, xp, *layer0, FFN_TILE)
    xs = _mix_ffn(oa_sample.reshape(N_SAMPLE, w), ob_sample, xs, *layer0, SAMPLE_TILE)

    per_head = lambda t: t.reshape(BATCH, A_HEADS, A_HD, SEQ).transpose(0, 3, 1, 2)[None]
    new_k_prompt = per_head(kt32)
    new_v_prompt = per_head(vt32)
    new_k_sample = hs[..., w:2 * w].reshape(1, DEC_BATCH, DEC_SEQ, A_HEADS, A_HD)
    new_v_sample = hs[..., 2 * w:3 * w].reshape(1, DEC_BATCH, DEC_SEQ, A_HEADS, A_HD)
    new_gmlp_v_sample = vn_sample.reshape(1, DEC_BATCH, DEC_SEQ, w)

    w_odd = w_in_odd[0].astype(BF16)
    lb = jax.nn.softmax(hgrn_lb_param.astype(F32), axis=0)[0]
    pool_wb = pool_w[0].astype(BF16)

    h1p, oc_prompt = _proj_pool(xp, w_odd, pool_wb, pool_scale[0])
    od_prompt, s_prompt = _hgrn(h1p, jnp.zeros((BATCH, D_HEADS, D_HK, D_HK), F32), lb, hgrn_norm_g[0],
                                bsz=BATCH, length=SEQ, rows=HGRN_ROWS, in_rows=HGRN_ROWS, valid=SEQ,
                                seqs=1, out_dtype=BF16)

    h1s = _proj(xs, w_odd, SAMPLE_TILE).reshape(DEC_BATCH, DEC_SEQ, ODD_IN)
    h1s_pad = jnp.pad(h1s, pad_tokens).reshape(DEC_BATCH * SUBLANES, ODD_IN)
    halo_s = jnp.pad(state_pool[0], ((0, 0), (POOL_HALO - POOL_BUF, 0), (0, 0)))
    oc_sample = _pool(halo_s.reshape(DEC_BATCH * POOL_HALO, w), h1s_pad, pool_wb, pool_scale[0],
                      rows=SUBLANES, pos0=PAST_LEN)
    od_sample, s_sample = _hgrn(h1s_pad, state_hgrn[0], lb, hgrn_norm_g[0], bsz=DEC_BATCH,
                                length=HGRN_CHUNK, rows=HGRN_CHUNK, in_rows=SUBLANES, valid=DEC_SEQ,
                                seqs=HGRN_SAMPLE_SEQS, out_dtype=F32)
    real_rows = lambda t: t.reshape(DEC_BATCH, SUBLANES, w)[:, :DEC_SEQ].reshape(N_SAMPLE, w)

    layer1 = (w_out_odd[0].astype(BF16), ln_mix_g[1], ln_mix_b[1],
              ffn_w1b, ffn_w2b, 1, ln_ffn_g[1], ln_ffn_b[1])
    xp = _mix_ffn(oc_prompt, od_prompt, xp, *layer1, FFN_TILE)
    xs = _mix_ffn(real_rows(oc_sample), real_rows(od_sample), xs, *layer1, SAMPLE_TILE)

    new_pool_prompt = h1p.reshape(BATCH, SEQ, ODD_IN)[:, SEQ - POOL_BUF:, :w][None]
    new_pool_sample = jnp.concatenate([state_pool[0][:, DEC_SEQ:], h1s[..., :w]], axis=1)[None]
    return (xp.reshape(BATCH, SEQ, D_MODEL), xs.reshape(DEC_BATCH, DEC_SEQ, D_MODEL),
            new_k_prompt, new_v_prompt, new_k_sample, new_v_sample, new_gmlp_v_sample,
            new_pool_prompt, new_pool_sample, s_prompt[None], s_sample[None])
```
